```python
import math
import jax
import jax.numpy as jnp
from jax import lax
import numpy as np

D_MODEL = 4096
BATCH = 2
SEQ = 8192
DEPTH = 2

CTX_LEN = 256
GRID_W = 64
NORM_EPS = 1e-6
N_MOD = 6

RW_WIDTH = D_MODEL // 4
RW_HEAD = 64
RW_HEADS = RW_WIDTH // RW_HEAD
RW_DECAY_LORA = max(32, int(round(D_MODEL ** 0.5 * 1.8 / 32)) * 32)
RW_AAA_LORA = max(32, int(round(D_MODEL ** 0.5 * 1.8 / 32)) * 32)
RW_GATE_LORA = max(32, int(round(D_MODEL ** 0.6 * 0.6 / 32)) * 32)
RW_GN_EPS = 64e-5
RW_COLS = 3 * RW_WIDTH + 2 * RW_DECAY_LORA + 2 * RW_AAA_LORA + RW_GATE_LORA
RW_SPLIT = (RW_WIDTH, 2 * RW_WIDTH, 3 * RW_WIDTH,
            3 * RW_WIDTH + RW_DECAY_LORA, 3 * RW_WIDTH + 2 * RW_DECAY_LORA,
            3 * RW_WIDTH + 2 * RW_DECAY_LORA + RW_AAA_LORA, 3 * RW_WIDTH + 2 * RW_DECAY_LORA + 2 * RW_AAA_LORA)

HG_WIDTH = D_MODEL // 4
HG_EXPAND = 128
HG_HEADS = HG_WIDTH // HG_EXPAND
HG_VDIM = HG_WIDTH // HG_HEADS
HG_CHUNK = 64
HG_COLS = 5 * HG_WIDTH

HY_WIDTH = D_MODEL // 4
HY_EMB = 33
HY_BANDS = (HY_EMB - 1) // 2
HY_FILTER_HIDDEN = 64
HY_SHORT = 3
HY_DECAY_TARGET = 1e-2
HY_FAST_PCT = 0.3
HY_SLOW_PCT = 1.5
HY_COLS = 3 * HY_WIDTH

N_BRANCH = 3
GATE_COLS = N_BRANCH * D_MODEL
HG_END = RW_COLS + HG_COLS
HY_END = HG_END + HY_COLS
IN_COLS = HY_END + GATE_COLS
FFN_HIDDEN = -(-(8 * D_MODEL) // (3 * 256)) * 256

kernel_name = 'hybrid_rwkv7_hgrn2_hyena_prefix_dit'


def _rmsnorm(x, gain):
    xf = x.astype(jnp.float32)
    y = xf * lax.rsqrt(jnp.mean(xf * xf, axis=-1, keepdims=True) + NORM_EPS)
    return (y * gain.astype(jnp.float32)).astype(x.dtype)


def _modulate(x, gain, shift, scale):
    return _rmsnorm(x, gain) * (1.0 + scale) + shift


def _heads(t, n_heads):
    return t.reshape(t.shape[:-1] + (n_heads, t.shape[-1] // n_heads))


def _grid_quad_shift(p):
    b, n, ch = p.shape
    rows = n // GRID_W
    g = p.reshape(b, rows, GRID_W, ch)
    q = ch // 4
    left = jnp.pad(g[:, :, :-1, :q], ((0, 0), (0, 0), (1, 0), (0, 0)))
    right = jnp.pad(g[:, :, 1:, q:2 * q], ((0, 0), (0, 0), (0, 1), (0, 0)))
    up = jnp.pad(g[:, :-1, :, 2 * q:3 * q], ((0, 0), (1, 0), (0, 0), (0, 0)))
    down = jnp.pad(g[:, 1:, :, 3 * q:], ((0, 0), (0, 1), (0, 0), (0, 0)))
    return jnp.concatenate([left, right, up, down], axis=-1).reshape(b, n, ch)


def _seq_bi_shift(p):
    h = p.shape[-1] // 2
    left = jnp.pad(p[:, :-1, :h], ((0, 0), (1, 0), (0, 0)))
    right = jnp.pad(p[:, 1:, h:], ((0, 0), (0, 1), (0, 0)))
    return jnp.concatenate([left, right], axis=-1)


def _rwkv7_scan(r, w, k, v, a, b, s0):
    def step(s, inp):
        rt, wt, kt, vt, at, bt = inp
        sa = jnp.einsum('bhvk,bhk->bhv', s, at)
        s = s * wt[:, :, None, :] + sa[..., None] * bt[:, :, None, :] + vt[..., None] * kt[:, :, None, :]
        return s, jnp.einsum('bhvk,bhk->bhv', s, rt)
    xs = tuple(jnp.moveaxis(t.astype(jnp.float32), 1, 0) for t in (r, w, k, v, a, b))
    s_fin, ys = lax.scan(step, s0, xs)
    return jnp.moveaxis(ys, 0, 1), s_fin


def _rwkv7_prepare(p, shifted, mu, w0, w_up, a0, a_up, g_up, k_k, k_a):
    m = (p + mu * (shifted - p)).astype(jnp.float32)
    r, k, v, wd_f, wd_b, ad_f, ad_b, gd = jnp.split(m, RW_SPLIT, axis=-1)
    g = jax.nn.sigmoid(gd) @ g_up
    kk = _heads(k * k_k, RW_HEADS)
    kk = kk * lax.rsqrt(jnp.maximum(jnp.sum(kk * kk, axis=-1, keepdims=True), 1e-24))
    per_dir = []
    for d, (wd, ad) in enumerate(((wd_f, ad_f), (wd_b, ad_b))):
        w_log = -jax.nn.softplus(-(w0[d] + jnp.tanh(wd) @ w_up[d])) - 0.5
        decay = jnp.exp(-jnp.exp(w_log))
        a = jax.nn.sigmoid(a0[d] + ad @ a_up[d])
        kd = k * (1.0 + (a - 1.0) * k_a)
        per_dir.append((_heads(decay, RW_HEADS), _heads(kd, RW_HEADS), _heads(a, RW_HEADS)))
    return _heads(r, RW_HEADS), _heads(v, RW_HEADS), kk, g, per_dir


def _group_norm(y, gain, bias):
    mean = jnp.mean(y, axis=-1, keepdims=True)
    var = jnp.mean(jnp.square(y - mean), axis=-1, keepdims=True)
    yn = (y - mean) * lax.rsqrt(var + RW_GN_EPS)
    b, n = y.shape[:2]
    return yn.reshape(b, n, -1) * gain + bias


def _rwkv7_mixer(p_c, p_l, params, need_ctx):
    mu, w0, w_up, a0, a_up, g_up, k_k, k_a, r_k, ln_g, ln_b = params
    lora = (mu, w0, w_up, a0, a_up, g_up, k_k, k_a)
    prep_c = _rwkv7_prepare(p_c, _seq_bi_shift(p_c), *lora)
    prep_l = _rwkv7_prepare(p_l, _grid_quad_shift(p_l), *lora)
    s0 = jnp.zeros((p_l.shape[0], RW_HEADS, RW_HEAD, RW_HEAD), jnp.float32)

    def run(prep, d, s_init):
        r, v, kk, _, per_dir = prep
        decay, kd, a = per_dir[d]
        seqs = (r, decay, kd, v, -kk, kk * a)
        if d == 1:
            seqs = tuple(t[:, ::-1] for t in seqs)
        y, s = _rwkv7_scan(*seqs, s_init)
        return (y[:, ::-1] if d == 1 else y), s

    def finish(prep, y):
        r, v, _, g, per_dir = prep
        bonus = sum(jnp.sum(r * pd[1] * r_k, axis=-1, keepdims=True) for pd in per_dir) * v
        b, n = y.shape[:2]
        return (_group_norm(y, ln_g, ln_b) + bonus.reshape(b, n, RW_WIDTH)) * g

    yc_f, sc_f = run(prep_c, 0, s0)
    yc_b, sc_b = run(prep_c, 1, s0)
    yl_f, _ = run(prep_l, 0, sc_f)
    yl_b, _ = run(prep_l, 1, sc_b)
    out_l = finish(prep_l, yl_f + yl_b)
    out_c = finish(prep_c, yc_f + yc_b) if need_ctx else None
    return out_c, out_l


def _hgrn2_lower_bounds(lower_bounds, layer):
    cum = jnp.cumsum(jax.nn.softmax(lower_bounds.astype(jnp.float32), axis=0), axis=0)
    return cum[layer] - cum[0]


def _hgrn2_chunk_scan(q, k, v, log_f, s0):
    b, n, h, _ = q.shape
    nc = n // HG_CHUNK

    def to_chunks(t):
        return t.reshape(b, nc, HG_CHUNK, h, t.shape[-1]).transpose(1, 0, 3, 2, 4)

    mask = jnp.tril(jnp.ones((HG_CHUNK, HG_CHUNK), dtype=bool))[:, :, None]

    def step(s, inp):
        qt, kt, vt, gt = inp
        cb = jnp.cumsum(gt, axis=2)
        rel = cb[:, :, :, None, :] - cb[:, :, None, :, :]
        dec = jnp.exp(jnp.where(mask, rel, -jnp.inf))
        att = jnp.einsum('bhtk,bhtsk,bhsk->bhts', qt, dec, kt)
        o = jnp.einsum('bhts,bhsv->bhtv', att, vt) + jnp.einsum('bhtk,bhkv->bhtv', qt * jnp.exp(cb), s)
        last = cb[:, :, -1:, :]
        s_new = jnp.exp(last[:, :, 0, :])[..., None] * s + jnp.einsum('bhsk,bhsv->bhkv', kt * jnp.exp(last - cb), vt)
        return s_new, o

    xs = tuple(to_chunks(t.astype(jnp.float32)) for t in (q, k, v, log_f))
    s_fin, oc = lax.scan(step, s0, xs)
    return oc.transpose(1, 0, 3, 2, 4).reshape(b, n, h, -1), s_fin


def _hgrn2_prepare(p, lb):
    q, z_f, z_b, i, g = jnp.split(p.astype(jnp.float32), 5, axis=-1)
    gates = []
    for d, z in enumerate((z_f, z_b)):
        z = _heads(z, HG_HEADS)
        l = lb[d].reshape(HG_HEADS, HG_EXPAND)
        log_f = jnp.logaddexp(jnp.log(l), jnp.log1p(-l) + jax.nn.log_sigmoid(z))
        k = (1.0 - l) * jax.nn.sigmoid(-z)
        gates.append((log_f, k))
    return _heads(q, HG_HEADS), _heads(i, HG_HEADS), g, gates


def _hgrn2_mixer(p_c, p_l, lb, norm_g, need_ctx):
    prep_c = _hgrn2_prepare(p_c, lb)
    prep_l = _hgrn2_prepare(p_l, lb)
    s0 = jnp.zeros((p_l.shape[0], HG_HEADS, HG_EXPAND, HG_VDIM), jnp.float32)

    def run(prep, d, s_init):
        q, i, _, gates = prep
        log_f, k = gates[d]
        seqs = (q, k, i, log_f)
        if d == 1:
            seqs = tuple(t[:, ::-1] for t in seqs)
        o, s = _hgrn2_chunk_scan(*seqs, s_init)
        return (o[:, ::-1] if d == 1 else o), s

    def finish(prep, o):
        b, n = o.shape[:2]
        return _rmsnorm(o, norm_g).reshape(b, n, HG_WIDTH) * jax.nn.silu(prep[2])

    oc_f, sc_f = run(prep_c, 0, s0)
    oc_b, sc_b = run(prep_c, 1, s0)
    ol_f, _ = run(prep_l, 0, sc_f)
    ol_b, _ = run(prep_l, 1, sc_b)
    out_l = finish(prep_l, ol_f + ol_b)
    out_c = finish(prep_c, oc_f + oc_b) if need_ctx else None
    return out_c, out_l


def _short_conv(p, w, bias):
    pp = jnp.pad(p, ((0, 0), (1, 1), (0, 0)))
    return w[0] * pp[:, :-2] + w[1] * pp[:, 1:-1] + w[2] * pp[:, 2:] + bias


def _hyena_filters(n, w1, b1, w2, b2, w3, freq):
    t = jnp.linspace(0.0, 1.0, n, dtype=jnp.float32)[:, None]
    lag = jnp.arange(n, dtype=jnp.float32)[:, None]
    bands = jnp.linspace(1e-4, HY_BANDS - 1, HY_BANDS, dtype=jnp.float32)[None, :]
    ang = 2.0 * math.pi * lag * bands / n
    z = jnp.concatenate([t, jnp.cos(ang), -jnp.sin(ang)], axis=-1)
    h = jnp.sin(freq * (z @ w1 + b1))
    h = jnp.sin(freq * (h @ w2 + b2))
    h = (h @ w3).astype(jnp.float32)
    deltas = jnp.abs(jnp.linspace(math.log(HY_DECAY_TARGET) / HY_SLOW_PCT,
                                  math.log(HY_DECAY_TARGET) / HY_FAST_PCT, HY_WIDTH, dtype=jnp.float32))
    window = jnp.exp(-t * deltas[None, :])
    h_f, h_b = jnp.split(h, 2, axis=-1)
    return h_f * window, h_b * window


def _bidir_long_conv(u, h_f, h_b):
    n = u.shape[1]
    taps = jnp.concatenate([0.5 * (h_f[:1] + h_b[:1]), h_f[1:], jnp.zeros_like(h_f[:1]), h_b[1:][::-1]], axis=0)
    taps = taps / jnp.sum(jnp.abs(taps), axis=0, keepdims=True)
    u_f = jnp.fft.rfft(u.astype(jnp.float32), n=2 * n, axis=1)
    t_f = jnp.fft.rfft(taps, n=2 * n, axis=0)
    return jnp.fft.irfft(u_f * t_f[None], n=2 * n, axis=1)[:, :n]


def _hyena_branch(p, conv_w, conv_b, filt, bias):
    u = _short_conv(p, conv_w, conv_b)
    x0, x1, v = jnp.split(u, 3, axis=-1)
    h_f, h_b = _hyena_filters(p.shape[1], *filt)
    z = (x1 * v).astype(jnp.float32)
    z = _bidir_long_conv(z, h_f, h_b) + bias * z
    return x0 * z


def _merge(p_gate, oa, ob, oc, w_ba, w_bb, w_bc, w_out):
    ga, gb, gc = jnp.split(jax.nn.sigmoid(p_gate), N_BRANCH, axis=-1)
    y = ga * (oa @ w_ba) + gb * (ob @ w_bb) + gc * (oc @ w_bc)
    return y @ w_out


def _swiglu(h, w_gate, w_up, w_down):
    return (jax.nn.silu(h @ w_gate) * (h @ w_up)) @ w_down


def _layer(xl, xc, c, c_ctx, ada_w, ada_b, n1, n2, w_in, rw, hg, hy, merge, ffn, need_ctx):
    mod_l = jnp.split((jax.nn.silu(c) @ ada_w + ada_b)[:, None, :], N_MOD, axis=-1)
    mod_c = jnp.split(jax.nn.silu(c_ctx) @ ada_w + ada_b, N_MOD, axis=-1)
    p_l = _modulate(xl, n1, mod_l[0], mod_l[1]) @ w_in
    w_in_c = w_in if need_ctx else w_in[:, :HG_END]
    p_c = _modulate(xc, n1, mod_c[0], mod_c[1]) @ w_in_c
    oa_c, oa_l = _rwkv7_mixer(p_c[..., :RW_COLS], p_l[..., :RW_COLS], rw, need_ctx)
    ob_c, ob_l = _hgrn2_mixer(p_c[..., RW_COLS:HG_END], p_l[..., RW_COLS:HG_END], hg[0], hg[1], need_ctx)
    oc_l = _hyena_branch(p_l[..., HG_END:HY_END], *hy)
    xl = xl + mod_l[2] * _merge(p_l[..., HY_END:], oa_l, ob_l, oc_l, *merge)
    xl = xl + mod_l[5] * _swiglu(_modulate(xl, n2, mod_l[3], mod_l[4]), *ffn)
    if not need_ctx:
        return xl, None
    oc_c = _hyena_branch(p_c[..., HG_END:HY_END], *hy)
    xc = xc + mod_c[2] * _merge(p_c[..., HY_END:], oa_c, ob_c, oc_c, *merge)
    xc = xc + mod_c[5] * _swiglu(_modulate(xc, n2, mod_c[3], mod_c[4]), *ffn)
    return xl, xc


def setup_inputs(seed: int = 0) -> dict:
    key = jax.random.key(seed)
    ks = iter(jax.random.split(key, 48))
    f32 = jnp.float32

    def nrm(shape, scale):
        return jax.random.normal(next(ks), shape, f32) * scale

    L = DEPTH
    w0_base = -6.0 + 5.0 * (jnp.arange(RW_WIDTH, dtype=f32) / (RW_WIDTH - 1)) ** 0.85 + 0.5
    return {
        'x': nrm((BATCH, SEQ, D_MODEL), 1.0),
        'c': nrm((BATCH, D_MODEL), 1.0),
        'ctx': nrm((BATCH, CTX_LEN, D_MODEL), 1.0),
        'c_ctx': nrm((D_MODEL,), 1.0),
        'ada_w': nrm((L, D_MODEL, N_MOD * D_MODEL), 0.5 * D_MODEL ** -0.5),
        'ada_b': nrm((L, N_MOD * D_MODEL), 0.01),
        'norm1_g': 1.0 + nrm((L, D_MODEL), 0.02),
        'norm2_g': 1.0 + nrm((L, D_MODEL), 0.02),
        'w_in': nrm((L, D_MODEL, IN_COLS), D_MODEL ** -0.5),
        'rw_mu': jax.random.uniform(next(ks), (L, RW_COLS), f32),
        'rw_w0': w0_base + nrm((L, 2, RW_WIDTH), 0.1),
        'rw_w_up': nrm((L, 2, RW_DECAY_LORA, RW_WIDTH), 0.1 * RW_DECAY_LORA ** -0.5),
        'rw_a0': nrm((L, 2, RW_WIDTH), 0.1),
        'rw_a_up': nrm((L, 2, RW_AAA_LORA, RW_WIDTH), 0.3 * RW_AAA_LORA ** -0.5),
        'rw_g_up': nrm((L, RW_GATE_LORA, RW_WIDTH), RW_GATE_LORA ** -0.5),
        'rw_k_k': 0.85 + nrm((L, RW_WIDTH), 0.02),
        'rw_k_a': 1.0 + nrm((L, RW_WIDTH), 0.02),
        'rw_r_k': -0.04 + nrm((L, RW_HEADS, RW_HEAD), 0.02),
        'rw_ln_g': 1.0 + nrm((L, RW_WIDTH), 0.02),
        'rw_ln_b': nrm((L, RW_WIDTH), 0.01),
        'hg_lower_bounds': nrm((L, 2, HG_WIDTH), 1.0),
        'hg_norm_g': 1.0 + nrm((L, HG_VDIM), 0.02),
        'hy_conv_w': nrm((L, HY_SHORT, HY_COLS), HY_SHORT ** -0.5),
        'hy_conv_b': nrm((L, HY_COLS), 0.01),
        'hy_f_w1': nrm((L, HY_EMB, HY_FILTER_HIDDEN), HY_EMB ** -0.5),
        'hy_f_b1': nrm((L, HY_FILTER_HIDDEN), 0.1),
        'hy_f_w2': nrm((L, HY_FILTER_HIDDEN, HY_FILTER_HIDDEN), HY_FILTER_HIDDEN ** -0.5),
        'hy_f_b2': nrm((L, HY_FILTER_HIDDEN), 0.1),
        'hy_f_w3': nrm((L, HY_FILTER_HIDDEN, 2 * HY_WIDTH), HY_FILTER_HIDDEN ** -0.5),
        'hy_freq': 1.0 + nrm((L, HY_FILTER_HIDDEN), 0.1),
        'hy_bias': nrm((L, HY_WIDTH), 1.0),
        'w_branch_a': nrm((L, RW_WIDTH, D_MODEL), RW_WIDTH ** -0.5),
        'w_branch_b': nrm((L, HG_WIDTH, D_MODEL), HG_WIDTH ** -0.5),
        'w_branch_c': nrm((L, HY_WIDTH, D_MODEL), HY_WIDTH ** -0.5),
        'w_out': nrm((L, D_MODEL, D_MODEL), D_MODEL ** -0.5),
        'ffn_w_gate': nrm((L, D_MODEL, FFN_HIDDEN), D_MODEL ** -0.5),
        'ffn_w_up': nrm((L, D_MODEL, FFN_HIDDEN), D_MODEL ** -0.5),
        'ffn_w_down': nrm((L, FFN_HIDDEN, D_MODEL), FFN_HIDDEN ** -0.5),
        'final_norm_g': 1.0 + nrm((D_MODEL,), 0.02),
    }


def reference(x, c, ctx, c_ctx, ada_w, ada_b, norm1_g, norm2_g, w_in,
              rw_mu, rw_w0, rw_w_up, rw_a0, rw_a_up, rw_g_up, rw_k_k, rw_k_a, rw_r_k, rw_ln_g, rw_ln_b,
              hg_lower_bounds, hg_norm_g,
              hy_conv_w, hy_conv_b, hy_f_w1, hy_f_b1, hy_f_w2, hy_f_b2, hy_f_w3, hy_freq, hy_bias,
              w_branch_a, w_branch_b, w_branch_c, w_out,
              ffn_w_gate, ffn_w_up, ffn_w_down, final_norm_g):
    xl, xc = x, ctx
    for layer in range(DEPTH):
        need_ctx = layer < DEPTH - 1
        rw = (rw_mu[layer], rw_w0[layer], rw_w_up[layer], rw_a0[layer], rw_a_up[layer], rw_g_up[layer],
              rw_k_k[layer], rw_k_a[layer], rw_r_k[layer], rw_ln_g[layer], rw_ln_b[layer])
        hg = (_hgrn2_lower_bounds(hg_lower_bounds, layer), hg_norm_g[layer])
        hy = (hy_conv_w[layer], hy_conv_b[layer],
              (hy_f_w1[layer], hy_f_b1[layer], hy_f_w2[layer], hy_f_b2[layer], hy_f_w3[layer], hy_freq[layer]),
              hy_bias[layer])
        merge = (w_branch_a[layer], w_branch_b[layer], w_branch_c[layer], w_out[layer])
        ffn = (ffn_w_gate[layer], ffn_w_up[layer], ffn_w_down[layer])
        xl, xc = _layer(xl, xc, c, c_ctx, ada_w[layer], ada_b[layer], norm1_g[layer], norm2_g[layer],
                        w_in[layer], rw, hg, hy, merge, ffn, need_ctx)
    return _rmsnorm(xl, final_norm_g)
```

```python
import functools
import math

import numpy as np
import jax
import jax.numpy as jnp
from jax import lax
from jax.experimental import pallas as pl
from jax.experimental.pallas import tpu as pltpu

F32 = jnp.float32
BF16 = jnp.bfloat16

GRID_W = 64
CHUNK = 64
SUB = 16
NORM_EPS = 1e-6
RW_HEAD = 64
RW_GN_EPS = 64e-5
HG_EXPAND = 128
HY_EMB = 33
HY_DECAY_TARGET = 1e-2
HY_FAST_PCT = 0.3
HY_SLOW_PCT = 1.5
LANE = 128
VMEM_CAP = 56 * 1024 * 1024
VMEM_SLACK = 8 * 1024 * 1024


def _params(sem, vmem_bytes):
    return pltpu.CompilerParams(dimension_semantics=sem,
                                vmem_limit_bytes=int(min(max(vmem_bytes + VMEM_SLACK, 16 << 20), VMEM_CAP)))


def _pick(n, cands):
    for c in cands:
        if n % c == 0:
            return c
    return n


def _rup(n, m):
    return -(-n // m) * m


def _bdot(a, b):
    return jnp.dot(a.astype(BF16), b.astype(BF16), preferred_element_type=F32)


def _bdot_nt(a, b):
    return lax.dot_general(a.astype(BF16), b.astype(BF16), (((1,), (1,)), ((), ())),
                           preferred_element_type=F32)


def _bdot_tn(a, b):
    return lax.dot_general(a.astype(BF16), b.astype(BF16), (((0,), (0,)), ((), ())),
                           preferred_element_type=F32)


def _split(x):
    hi = x.astype(BF16)
    lo = (x - hi.astype(F32)).astype(BF16)
    return hi, lo


def _mask_dot(m, x):
    hi, lo = _split(x)
    return (jnp.dot(m, hi, preferred_element_type=F32) + jnp.dot(m, lo, preferred_element_type=F32))


def _x_mask_dot(x, m):
    hi, lo = _split(x)
    return (jnp.dot(hi, m, preferred_element_type=F32) + jnp.dot(lo, m, preferred_element_type=F32))


def _sigmoid(x):
    return 1.0 / (1.0 + jnp.exp(-x))


def _order_masks(d, n, reps=1):
    row = lax.broadcasted_iota(jnp.int32, (n, reps * n), 0)
    col = lax.broadcasted_iota(jnp.int32, (n, reps * n), 1) % n
    diff = (row - col) * (1 - 2 * d)
    return diff >= 0, diff > 0


def _ada_kernel(c_ref, w_ref, b_ref, o_ref):
    c = c_ref[...]
    o_ref[...] = _bdot(c * _sigmoid(c), w_ref[...]) + b_ref[...]


def _ada(cvec, w, b):
    rows, d = cvec.shape
    n = w.shape[1]
    tn = _pick(n, (512, 256, 128))
    return pl.pallas_call(
        _ada_kernel,
        grid=(n // tn,),
        in_specs=[pl.BlockSpec((rows, d), lambda j: (0, 0)),
                  pl.BlockSpec((d, tn), lambda j: (0, j)),
                  pl.BlockSpec((1, tn), lambda j: (0, j))],
        out_specs=pl.BlockSpec((rows, tn), lambda j: (0, j)),
        out_shape=jax.ShapeDtypeStruct((rows, n), F32),
        compiler_params=_params(("arbitrary",), 3 * d * tn * 4),
    )(cvec, w, b.reshape(1, n))


def _modnorm_kernel(x_ref, g_ref, shl_ref, scl_ref, shc_ref, scc_ref, o_ref, *, n_lat):
    tm = x_ref.shape[1]
    x = x_ref[0]
    y = x * lax.rsqrt(jnp.mean(x * x, axis=-1, keepdims=True) + NORM_EPS) * g_ref[...]
    pos = pl.program_id(1) * tm + lax.broadcasted_iota(jnp.int32, (tm, 1), 0)
    is_ctx = pos >= n_lat
    sc = jnp.where(is_ctx, scc_ref[0], scl_ref[0])
    sh = jnp.where(is_ctx, shc_ref[0], shl_ref[0])
    o_ref[0] = (y * (1.0 + sc) + sh).astype(o_ref.dtype)


def _modnorm(x, gain, shift, scale, n_lat):
    b, t, d = x.shape
    tm = _pick(t, (256, 128, 64))
    vec = lambda f: pl.BlockSpec((1, 1, d), f)
    return pl.pallas_call(
        functools.partial(_modnorm_kernel, n_lat=n_lat),
        grid=(b, t // tm),
        in_specs=[pl.BlockSpec((1, tm, d), lambda bi, i: (bi, i, 0)),
                  pl.BlockSpec((1, d), lambda bi, i: (0, 0)),
                  vec(lambda bi, i: (bi, 0, 0)), vec(lambda bi, i: (bi, 0, 0)),
                  vec(lambda bi, i: (b, 0, 0)), vec(lambda bi, i: (b, 0, 0))],
        out_specs=pl.BlockSpec((1, tm, d), lambda bi, i: (bi, i, 0)),
        out_shape=jax.ShapeDtypeStruct((b, t, d), BF16),
        compiler_params=_params(("parallel", "arbitrary"), 6 * tm * d * 4),
    )(x, gain.reshape(1, d), shift, scale, shift, scale)


def _rmsnorm_kernel(x_ref, g_ref, o_ref):
    x = x_ref[0]
    o_ref[0] = x * lax.rsqrt(jnp.mean(x * x, axis=-1, keepdims=True) + NORM_EPS) * g_ref[...]


def _final_norm(x, gain, n_lat):
    b, _, d = x.shape
    tm = _pick(n_lat, (256, 128, 64))
    return pl.pallas_call(
        _rmsnorm_kernel,
        grid=(b, n_lat // tm),
        in_specs=[pl.BlockSpec((1, tm, d), lambda bi, i: (bi, i, 0)),
                  pl.BlockSpec((1, d), lambda bi, i: (0, 0))],
        out_specs=pl.BlockSpec((1, tm, d), lambda bi, i: (bi, i, 0)),
        out_shape=jax.ShapeDtypeStruct((b, n_lat, d), F32),
        compiler_params=_params(("parallel", "arbitrary"), 6 * tm * d * 4),
    )(x, gain.reshape(1, d))


def _mm_kernel(x_ref, w_ref, o_ref):
    o_ref[0] = jnp.dot(x_ref[0], w_ref[...], preferred_element_type=F32).astype(o_ref.dtype)


def _matmul(x, w, out_dtype):
    b, t, k = x.shape
    n = w.shape[1]
    tm = _pick(t, (768, 384, 256, 128, 64))
    tn = _pick(n, (1024, 512, 256, 128))
    osz = jnp.dtype(out_dtype).itemsize
    return pl.pallas_call(
        _mm_kernel,
        grid=(b, t // tm, n // tn),
        in_specs=[pl.BlockSpec((1, tm, k), lambda bi, i, j: (bi, i, 0)),
                  pl.BlockSpec((k, tn), lambda bi, i, j: (0, j))],
        out_specs=pl.BlockSpec((1, tm, tn), lambda bi, i, j: (bi, i, j)),
        out_shape=jax.ShapeDtypeStruct((b, t, n), out_dtype),
        compiler_params=_params(("parallel", "parallel", "arbitrary"),
                                2 * (tm * k * 2 + k * tn * 2 + tm * tn * osz) + tm * tn * 4),
    )(x, w)


def _swiglu_kernel(x_ref, wg_ref, wu_ref, o_ref):
    x = x_ref[0]
    g = jnp.dot(x, wg_ref[...], preferred_element_type=F32)
    u = jnp.dot(x, wu_ref[...], preferred_element_type=F32)
    o_ref[0] = (g * _sigmoid(g) * u).astype(o_ref.dtype)


def _swiglu_up(x, wg, wu):
    b, t, k = x.shape
    n = wg.shape[1]
    tm = _pick(t, (768, 384, 256, 128, 64))
    tn = _pick(n, (512, 256, 128))
    return pl.pallas_call(
        _swiglu_kernel,
        grid=(b, t // tm, n // tn),
        in_specs=[pl.BlockSpec((1, tm, k), lambda bi, i, j: (bi, i, 0)),
                  pl.BlockSpec((k, tn), lambda bi, i, j: (0, j)),
                  pl.BlockSpec((k, tn), lambda bi, i, j: (0, j))],
        out_specs=pl.BlockSpec((1, tm, tn), lambda bi, i, j: (bi, i, j)),
        out_shape=jax.ShapeDtypeStruct((b, t, n), BF16),
        compiler_params=_params(("parallel", "parallel", "arbitrary"),
                                2 * (tm * k * 2 + 2 * k * tn * 2 + tm * tn * 2) + 3 * tm * tn * 4),
    )(x, wg, wu)


def _resid_kernel(x_ref, w_ref, r_ref, gl_ref, gc_ref, o_ref, acc_ref, *, n_lat):
    kk = pl.program_id(3)

    @pl.when(kk == 0)
    def _():
        acc_ref[...] = jnp.zeros_like(acc_ref)

    acc_ref[...] += jnp.dot(x_ref[0], w_ref[...], preferred_element_type=F32)

    @pl.when(kk == pl.num_programs(3) - 1)
    def _():
        tm = acc_ref.shape[0]
        pos = pl.program_id(1) * tm + lax.broadcasted_iota(jnp.int32, (tm, 1), 0)
        gate = jnp.where(pos >= n_lat, gc_ref[0], gl_ref[0])
        o_ref[0] = r_ref[0] + gate * acc_ref[...]


def _matmul_resid(x, w, res, gate, n_lat):
    b, t, k = x.shape
    n = w.shape[1]
    tm = _pick(t, (768, 384, 256, 128, 64))
    tn = _pick(n, (1024, 512, 256, 128))
    tk = k if k <= 4096 else _pick(k, (2816, 2048, 1024, 512, 256, 128))
    return pl.pallas_call(
        functools.partial(_resid_kernel, n_lat=n_lat),
        grid=(b, t // tm, n // tn, k // tk),
        in_specs=[pl.BlockSpec((1, tm, tk), lambda bi, i, j, l: (bi, i, l)),
                  pl.BlockSpec((tk, tn), lambda bi, i, j, l: (l, j)),
                  pl.BlockSpec((1, tm, tn), lambda bi, i, j, l: (bi, i, j)),
                  pl.BlockSpec((1, 1, tn), lambda bi, i, j, l: (bi, 0, j)),
                  pl.BlockSpec((1, 1, tn), lambda bi, i, j, l: (b, 0, j))],
        out_specs=pl.BlockSpec((1, tm, tn), lambda bi, i, j, l: (bi, i, j)),
        out_shape=jax.ShapeDtypeStruct((b, t, n), F32),
        scratch_shapes=[pltpu.VMEM((tm, tn), F32)],
        compiler_params=_params(("parallel", "parallel", "arbitrary", "arbitrary"),
                                2 * (tm * tk * 2 + tk * tn * 2 + 2 * tm * tn * 4) + 2 * tm * tn * 4),
    )(x, w, res, gate, gate)


def _merge_kernel(oa_ref, ob_ref, oc_ref, wa_ref, wb_ref, wc_ref, ga_ref, gb_ref, gc_ref, o_ref):
    y = _sigmoid(ga_ref[0].astype(F32)) * jnp.dot(oa_ref[0], wa_ref[...], preferred_element_type=F32)
    y += _sigmoid(gb_ref[0].astype(F32)) * jnp.dot(ob_ref[0], wb_ref[...], preferred_element_type=F32)
    y += _sigmoid(gc_ref[0].astype(F32)) * jnp.dot(oc_ref[0], wc_ref[...], preferred_element_type=F32)
    o_ref[0] = y.astype(o_ref.dtype)


def _merge(oa, ob, oc, wa, wb, wc, pgate):
    b, t, kw = oa.shape
    d = wa.shape[1]
    tm = _pick(t, (768, 384, 256, 128, 64))
    tn = _pick(d, (1024, 512, 256, 128))
    nj = d // tn
    br = lambda: pl.BlockSpec((1, tm, kw), lambda bi, i, j: (bi, i, 0))
    wt = lambda: pl.BlockSpec((kw, tn), lambda bi, i, j: (0, j))
    gt = lambda s: pl.BlockSpec((1, tm, tn), lambda bi, i, j: (bi, i, s * nj + j))
    gsz = jnp.dtype(pgate.dtype).itemsize
    return pl.pallas_call(
        _merge_kernel,
        grid=(b, t // tm, nj),
        in_specs=[br(), br(), br(), wt(), wt(), wt(), gt(0), gt(1), gt(2)],
        out_specs=pl.BlockSpec((1, tm, tn), lambda bi, i, j: (bi, i, j)),
        out_shape=jax.ShapeDtypeStruct((b, t, d), BF16),
        compiler_params=_params(("parallel", "parallel", "arbitrary"),
                                2 * (3 * tm * kw * 2 + 3 * kw * tn * 2 + 3 * tm * tn * gsz + tm * tn * 2)
                                + 4 * tm * tn * 4),
    )(oa, ob, oc, wa, wb, wc, pgate, pgate, pgate)


def _pair_sum_matrix():
    r = lax.broadcasted_iota(jnp.int32, (LANE, LANE), 0) // RW_HEAD
    c = lax.broadcasted_iota(jnp.int32, (LANE, LANE), 1) // RW_HEAD
    return jnp.where(r == c, 1.0, 0.0).astype(BF16)


def _head_sums(x, hs):
    rows, w = x.shape
    nt = w // LANE
    stacked = jnp.concatenate([x[:, j * LANE:(j + 1) * LANE] for j in range(nt)], axis=0)
    s = _x_mask_dot(stacked, hs)
    return jnp.concatenate([s[j * rows:(j + 1) * rows] for j in range(nt)], axis=1)


def _rw_prep_kernel(prev_ref, cur_ref, next_ref, grp_ref, mu_ref, w0_ref, wup_ref, a0_ref, aup_ref,
                    gup_ref, kk_ref, ka_ref, rk_ref,
                    r_ref, v_ref, kn_ref, g_ref, bon_ref, lw_ref, kd_ref, bb_ref,
                    *, n_lat_chunks, n_chunks, w, ld, la, lg):
    i = pl.program_id(1)
    c = CHUNK
    cur = cur_ref[0]
    prev = prev_ref[0]
    nxt = next_ref[0]
    is_ctx = i >= n_lat_chunks
    first = jnp.logical_or(i == 0, i == n_lat_chunks)
    last = jnp.logical_or(i == n_lat_chunks - 1, i == n_chunks - 1)
    row = lax.broadcasted_iota(jnp.int32, (c, 1), 0)
    carry_in = jnp.where(jnp.logical_and(is_ctx, jnp.logical_not(first)), 1.0, 0.0)
    carry_out = jnp.where(jnp.logical_and(is_ctx, jnp.logical_not(last)), 1.0, 0.0)
    tm1 = jnp.where(row == 0, prev[c - 1:c] * carry_in, pltpu.roll(cur, 1, 0))
    tp1 = jnp.where(row == c - 1, nxt[0:1] * carry_out, pltpu.roll(cur, c - 1, 0))
    lat_up = jnp.where(jnp.logical_or(is_ctx, first), 0.0, 1.0)
    lat_dn = jnp.where(jnp.logical_or(is_ctx, last), 0.0, 1.0)
    code = jnp.where(is_ctx, grp_ref[1:2], grp_ref[0:1])
    shifted = jnp.where(code == 0, tm1,
                        jnp.where(code == 1, tp1,
                                  jnp.where(code == 2, prev * lat_up, nxt * lat_dn)))
    m = cur + mu_ref[...] * (shifted - cur)

    r = m[:, 0:w]
    k = m[:, w:2 * w]
    v = m[:, 2 * w:3 * w]
    o = 3 * w
    wd = (m[:, o:o + ld], m[:, o + ld:o + 2 * ld])
    ad = (m[:, o + 2 * ld:o + 2 * ld + la], m[:, o + 2 * ld + la:o + 2 * ld + 2 * la])
    gd = m[:, o + 2 * ld + 2 * la:o + 2 * ld + 2 * la + lg]

    hs = _pair_sum_matrix()
    g_ref[0] = _bdot(_sigmoid(gd), gup_ref[...])
    kx = k * kk_ref[...]
    kn = kx * lax.rsqrt(jnp.maximum(_head_sums(kx * kx, hs), 1e-24))
    r_ref[0] = r
    v_ref[0] = v
    kn_ref[0] = kn
    bonus = jnp.zeros_like(r)
    for d in range(2):
        wl = w0_ref[d] + _bdot(jnp.tanh(wd[d]), wup_ref[d])
        sp = jnp.maximum(-wl, 0.0) + jnp.log(1.0 + jnp.exp(-jnp.abs(wl)))
        lw_ref[d, 0] = -jnp.exp(-sp - 0.5)
        a = _sigmoid(a0_ref[d] + _bdot(ad[d], aup_ref[d]))
        kd = k * (1.0 + (a - 1.0) * ka_ref[...])
        kd_ref[d, 0] = kd
        bb_ref[d, 0] = kn * a
        bonus += r * kd * rk_ref[...]
    bon_ref[0] = _head_sums(bonus, hs) * v


def _rw_prep(pa, grp, mu, w0, wup, a0, aup, gup, k_k, k_a, r_k, n_lat, dims):
    b, t, na = pa.shape
    w, ld, la, lg = dims
    nc = t // CHUNK
    nlc = n_lat // CHUNK
    blk = lambda f: pl.BlockSpec((1, CHUNK, na), f)
    full = lambda a: pl.BlockSpec(a.shape, lambda bi, i: (0,) * a.ndim)
    o1 = pl.BlockSpec((1, CHUNK, w), lambda bi, i: (bi, i, 0))
    o2 = pl.BlockSpec((2, 1, CHUNK, w), lambda bi, i: (0, bi, i, 0))
    s1 = jax.ShapeDtypeStruct((b, t, w), F32)
    s2 = jax.ShapeDtypeStruct((2, b, t, w), F32)
    consts = (grp, mu, w0, wup, a0, aup, gup, k_k, k_a, r_k)
    return pl.pallas_call(
        functools.partial(_rw_prep_kernel, n_lat_chunks=nlc, n_chunks=nc, w=w, ld=ld, la=la, lg=lg),
        grid=(b, nc),
        in_specs=[blk(lambda bi, i: (bi, jnp.maximum(i - 1, 0), 0)),
                  blk(lambda bi, i: (bi, i, 0)),
                  blk(lambda bi, i: (bi, jnp.minimum(i + 1, nc - 1), 0))] + [full(a) for a in consts],
        out_specs=[o1, o1, o1, o1, o1, o2, o2, o2],
        out_shape=[s1, s1, s1, s1, s1, s2, s2, s2],
        compiler_params=_params(("parallel", "arbitrary"), 48 << 20),
    )(pa, pa, pa, *consts)


def _rw_scan_kernel(r_ref, v_ref, kn_ref, lw_ref, kd_ref, bb_ref, y_ref, s_ref, *, n_pairs):
    d = pl.program_id(0)
    i = pl.program_id(2)
    c = CHUNK

    @pl.when(i == 0)
    def _():
        s_ref[...] = jnp.zeros_like(s_ref)

    incl_bf = jnp.where(_order_masks(d, c)[0], 1.0, 0.0).astype(BF16)
    incl2, strict2 = _order_masks(d, c, 2)
    lane = lax.broadcasted_iota(jnp.int32, (1, LANE), 1)
    m0 = (lane < RW_HEAD).astype(F32)
    m1 = 1.0 - m0
    rowh = lax.broadcasted_iota(jnp.int32, (LANE, LANE), 0) // RW_HEAD
    colh = lax.broadcasted_iota(jnp.int32, (LANE, LANE), 1) // RW_HEAD
    blockdiag = rowh == colh
    last_row = jnp.where(d == 0, c - 1, 0)
    rsel = lax.broadcasted_iota(jnp.int32, (c, 1), 0) == last_row

    def stack(x):
        return jnp.concatenate([x * m0, x * m1], axis=0)

    for p in range(n_pairs):
        sl = slice(p * LANE, (p + 1) * LANE)
        lw = lw_ref[0, 0, :, sl]
        r = r_ref[0, :, sl]
        v = v_ref[0, :, sl]
        kd = kd_ref[0, 0, :, sl]
        bb = bb_ref[0, 0, :, sl]
        aa = -kn_ref[0, :, sl]
        s0 = s_ref[p]

        cw = _mask_dot(incl_bf, lw)
        tot = jnp.sum(jnp.where(rsel, cw, 0.0), axis=0, keepdims=True)
        w_in = jnp.exp(cw)
        w_inv = jnp.exp(-cw)
        a_t = aa * jnp.exp(cw - lw)
        r_t = r * w_in
        b_t = bb * w_inv
        k_t = kd * w_inv
        w_rem = jnp.exp(tot - cw)
        ar = jnp.concatenate([a_t, r_t], axis=0)
        sc = _bdot_nt(ar, jnp.concatenate([stack(b_t), stack(k_t)], axis=0))
        a_ab = jnp.where(strict2, sc[:c, :2 * c], 0.0)
        a_ak = jnp.where(strict2, sc[:c, 2 * c:], 0.0)
        r_b = jnp.where(incl2, sc[c:, :2 * c], 0.0)
        r_k = jnp.where(incl2, sc[c:, 2 * c:], 0.0)

        abd = stack(a_ab)
        q = _bdot(abd, abd)
        nsum = abd
        for _ in range(5):
            nsum = nsum + q + _bdot(q, nsum)
            q = _bdot(q, q)
        n_side = nsum[:c] + nsum[c:]

        art = _bdot_nt(ar, s0)
        vst = stack(v)
        rhs = art[:c] + _bdot(a_ak, vst)
        u = rhs + _bdot(n_side, stack(rhs))
        y = art[c:] + _bdot(jnp.concatenate([r_b, r_k], axis=1), jnp.concatenate([stack(u), vst], axis=0))
        y_ref[0, 0, :, sl] = y
        upd = _bdot_tn(jnp.concatenate([u, v], axis=0),
                       jnp.concatenate([bb * w_rem, kd * w_rem], axis=0))
        s_ref[p] = s0 * jnp.exp(tot) + jnp.where(blockdiag, upd, 0.0)


def _scan_chunk(d, i, n_lat_chunks, n_chunks):
    nctx = n_chunks - n_lat_chunks
    in_ctx = i < nctx
    fwd = jnp.where(in_ctx, n_lat_chunks + i, i - nctx)
    bwd = jnp.where(in_ctx, n_chunks - 1 - i, n_chunks - 1 - i)
    return jnp.where(d == 0, fwd, bwd)


def _rw_scan(r, v, kn, lw, kd, bb, n_lat):
    b, t, w = r.shape
    nc = t // CHUNK
    nlc = n_lat // CHUNK
    npair = w // LANE
    ch = functools.partial(_scan_chunk, n_lat_chunks=nlc, n_chunks=nc)
    s1 = pl.BlockSpec((1, CHUNK, w), lambda d, bi, i: (bi, ch(d, i), 0))
    s2 = pl.BlockSpec((1, 1, CHUNK, w), lambda d, bi, i: (d, bi, ch(d, i), 0))
    return pl.pallas_call(
        functools.partial(_rw_scan_kernel, n_pairs=npair),
        grid=(2, b, nc),
        in_specs=[s1, s1, s1, s2, s2, s2],
        out_specs=s2,
        out_shape=jax.ShapeDtypeStruct((2, b, t, w), F32),
        scratch_shapes=[pltpu.VMEM((npair, LANE, LANE), F32)],
        compiler_params=_params(("arbitrary", "arbitrary", "arbitrary"), 32 << 20),
    )(r, v, kn, lw, kd, bb)


def _rw_finish_kernel(y_ref, bon_ref, g_ref, lg_ref, lb_ref, o_ref):
    y = y_ref[0, 0] + y_ref[1, 0]
    hs = _pair_sum_matrix()
    mean = _head_sums(y, hs) * (1.0 / RW_HEAD)
    yc = y - mean
    var = _head_sums(yc * yc, hs) * (1.0 / RW_HEAD)
    yn = yc * lax.rsqrt(var + RW_GN_EPS)
    o_ref[0] = ((yn * lg_ref[...] + lb_ref[...] + bon_ref[0]) * g_ref[0]).astype(o_ref.dtype)


def _rw_finish(y, bonus, g, ln_g, ln_b):
    _, b, t, w = y.shape
    tm = _pick(t, (256, 128, 64))
    s1 = pl.BlockSpec((1, tm, w), lambda bi, i: (bi, i, 0))
    vec = pl.BlockSpec((1, w), lambda bi, i: (0, 0))
    return pl.pallas_call(
        _rw_finish_kernel,
        grid=(b, t // tm),
        in_specs=[pl.BlockSpec((2, 1, tm, w), lambda bi, i: (0, bi, i, 0)), s1, s1, vec, vec],
        out_specs=s1,
        out_shape=jax.ShapeDtypeStruct((b, t, w), BF16),
        compiler_params=_params(("parallel", "arbitrary"), 32 << 20),
    )(y, bonus, g, ln_g.reshape(1, w), ln_b.reshape(1, w))


def _hg_scan_kernel(q_ref, z_ref, i_ref, lb_ref, o_ref, s_ref, *, n_heads):
    d = pl.program_id(0)
    step = pl.program_id(2)
    c = CHUNK
    nb = c // SUB

    @pl.when(step == 0)
    def _():
        s_ref[...] = jnp.zeros_like(s_ref)

    incl, _ = _order_masks(d, c)
    incl_bf = jnp.where(incl, 1.0, 0.0).astype(BF16)
    rowb = lax.broadcasted_iota(jnp.int32, (c, c), 0) // SUB
    colb = lax.broadcasted_iota(jnp.int32, (c, c), 1) // SUB
    same_blk = rowb == colb
    after = (rowb - colb) * (1 - 2 * d) > 0
    diag_mask = jnp.logical_and(same_blk, incl)
    last_row = jnp.where(d == 0, c - 1, 0)
    rsel = lax.broadcasted_iota(jnp.int32, (c, 1), 0) == last_row

    for h in range(n_heads):
        sl = slice(h * LANE, (h + 1) * LANE)
        q = q_ref[0, :, sl]
        z = z_ref[0, :, sl]
        val = i_ref[0, :, sl]
        lg_l = lb_ref[0, 0:1, sl]
        lg_1ml = lb_ref[0, 1:2, sl]
        one_ml = lb_ref[0, 2:3, sl]
        s0 = s_ref[h]

        ez = jnp.exp(-jnp.abs(z))
        log_sig = jnp.minimum(z, 0.0) - jnp.log(1.0 + ez)
        x2 = lg_1ml + log_sig
        mx = jnp.maximum(lg_l, x2)
        log_f = mx + jnp.log(jnp.exp(lg_l - mx) + jnp.exp(x2 - mx))
        k = one_ml * jnp.where(z >= 0.0, ez, 1.0) / (1.0 + ez)

        cb = _mask_dot(incl_bf, log_f)
        tot = jnp.sum(jnp.where(rsel, cb, 0.0), axis=0, keepdims=True)
        cb_ex = cb - log_f
        e_end = []
        e_start = []
        for j in range(nb):
            blk = slice(j * SUB, (j + 1) * SUB)
            e_end.append(jnp.min(cb[blk], axis=0, keepdims=True))
            e_start.append(jnp.max(cb_ex[blk], axis=0, keepdims=True))
        end_rows = jnp.concatenate([jnp.broadcast_to(e, (SUB, LANE)) for e in e_end], axis=0)
        start_rows = jnp.concatenate([jnp.broadcast_to(e, (SUB, LANE)) for e in e_start], axis=0)

        k_hat = k * jnp.exp(end_rows - cb)
        att = jnp.where(diag_mask,
                        _bdot_nt(q * jnp.exp(cb - start_rows), k * jnp.exp(start_rows - cb)), 0.0)
        for j in range(nb):
            qj = q * jnp.exp(jnp.minimum(cb - e_end[j], 0.0))
            att += jnp.where(jnp.logical_and(after, colb == j), _bdot_nt(qj, k_hat), 0.0)
        o = _bdot(att, val) + _bdot_nt(q * jnp.exp(cb), s0)
        o_ref[0, 0, :, sl] = o
        s_ref[h] = s0 * jnp.exp(tot) + _bdot_tn(val, k * jnp.exp(tot - cb))


def _hg_scan(phg, lbt, n_lat, w):
    b, t, _ = phg.shape
    nc = t // CHUNK
    nlc = n_lat // CHUNK
    nh = w // LANE
    ch = functools.partial(_scan_chunk, n_lat_chunks=nlc, n_chunks=nc)
    col = lambda f: pl.BlockSpec((1, CHUNK, w), f)
    return pl.pallas_call(
        functools.partial(_hg_scan_kernel, n_heads=nh),
        grid=(2, b, nc),
        in_specs=[col(lambda d, bi, i: (bi, ch(d, i), 0)),
                  col(lambda d, bi, i: (bi, ch(d, i), 1 + d)),
                  col(lambda d, bi, i: (bi, ch(d, i), 3)),
                  pl.BlockSpec((1, 3, w), lambda d, bi, i: (d, 0, 0))],
        out_specs=pl.BlockSpec((1, 1, CHUNK, w), lambda d, bi, i: (d, bi, ch(d, i), 0)),
        out_shape=jax.ShapeDtypeStruct((2, b, t, w), F32),
        scratch_shapes=[pltpu.VMEM((nh, LANE, LANE), F32)],
        compiler_params=_params(("arbitrary", "arbitrary", "arbitrary"), 32 << 20),
    )(phg, phg, phg, lbt)


def _hg_finish_kernel(o_ref, g_ref, ng_ref, out_ref, *, n_heads):
    for h in range(n_heads):
        sl = slice(h * LANE, (h + 1) * LANE)
        o = o_ref[0, 0, :, sl] + o_ref[1, 0, :, sl]
        y = o * lax.rsqrt(jnp.mean(o * o, axis=-1, keepdims=True) + NORM_EPS) * ng_ref[...]
        g = g_ref[0, :, sl]
        out_ref[0, :, sl] = (y * g * _sigmoid(g)).astype(out_ref.dtype)


def _hg_finish(o, phg, norm_g, w):
    _, b, t, _ = o.shape
    tm = _pick(t, (256, 128, 64))
    return pl.pallas_call(
        functools.partial(_hg_finish_kernel, n_heads=w // LANE),
        grid=(b, t // tm),
        in_specs=[pl.BlockSpec((2, 1, tm, w), lambda bi, i: (0, bi, i, 0)),
                  pl.BlockSpec((1, tm, w), lambda bi, i: (bi, i, 4)),
                  pl.BlockSpec((1, LANE), lambda bi, i: (0, 0))],
        out_specs=pl.BlockSpec((1, tm, w), lambda bi, i: (bi, i, 0)),
        out_shape=jax.ShapeDtypeStruct((b, t, w), BF16),
        compiler_params=_params(("parallel", "arbitrary"), 32 << 20),
    )(o, phg, norm_g.reshape(1, LANE))


def _hy_pre_kernel(prev_ref, cur_ref, next_ref, cw_ref, cb_ref, x0_ref, z_ref, *, n_lat_blocks, n_blocks, w):
    i = pl.program_id(1)
    tm = cur_ref.shape[1]
    cur = cur_ref[0]
    first = jnp.logical_or(i == 0, i == n_lat_blocks)
    last = jnp.logical_or(i == n_lat_blocks - 1, i == n_blocks - 1)
    row = lax.broadcasted_iota(jnp.int32, (tm, 1), 0)
    before = jnp.where(row == 0, prev_ref[0, tm - 1:tm] * jnp.where(first, 0.0, 1.0), pltpu.roll(cur, 1, 0))
    after = jnp.where(row == tm - 1, next_ref[0, 0:1] * jnp.where(last, 0.0, 1.0), pltpu.roll(cur, tm - 1, 0))
    u = cw_ref[0:1] * before + cw_ref[1:2] * cur + cw_ref[2:3] * after + cb_ref[...]
    x0_ref[0] = u[:, 0:w]
    z_ref[0] = u[:, w:2 * w] * u[:, 2 * w:3 * w]


def _hy_pre(phy, conv_w, conv_b, n_lat):
    b, t, w3 = phy.shape
    w = w3 // 3
    tm = _pick(math.gcd(n_lat, t - n_lat), (256, 128, 64))
    nb = t // tm
    blk = lambda f: pl.BlockSpec((1, tm, w3), f)
    o = pl.BlockSpec((1, tm, w), lambda bi, i: (bi, i, 0))
    s = jax.ShapeDtypeStruct((b, t, w), F32)
    return pl.pallas_call(
        functools.partial(_hy_pre_kernel, n_lat_blocks=n_lat // tm, n_blocks=nb, w=w),
        grid=(b, nb),
        in_specs=[blk(lambda bi, i: (bi, jnp.maximum(i - 1, 0), 0)),
                  blk(lambda bi, i: (bi, i, 0)),
                  blk(lambda bi, i: (bi, jnp.minimum(i + 1, nb - 1), 0)),
                  pl.BlockSpec((3, w3), lambda bi, i: (0, 0)),
                  pl.BlockSpec((1, w3), lambda bi, i: (0, 0))],
        out_specs=[o, o],
        out_shape=[s, s],
        compiler_params=_params(("parallel", "arbitrary"), 40 << 20),
    )(phy, phy, phy, conv_w, conv_b.reshape(1, w3))


def _hdot(a, b):
    return jnp.dot(a, b, precision=lax.Precision.HIGHEST, preferred_element_type=F32)


def _hy_taps_kernel(ft_ref, w1_ref, b1_ref, w2_ref, b2_ref, w3f_ref, w3b_ref, fr_ref, dl_ref,
                    tap_ref, sum_ref):
    i = pl.program_id(1)
    ft = ft_ref[...]
    fr = fr_ref[...]
    h = jnp.sin(fr * (_hdot(ft, w1_ref[...]) + b1_ref[...]))
    h = jnp.sin(fr * (_hdot(h, w2_ref[...]) + b2_ref[...]))
    cf = ft[:, HY_EMB:HY_EMB + 1]
    cb = ft[:, HY_EMB + 1:HY_EMB + 2]
    tt = ft[:, 0:1]
    tap = (cf * _hdot(h, w3f_ref[...]) + cb * _hdot(h, w3b_ref[...])) * jnp.exp(-tt * dl_ref[...])
    tap_ref[...] = tap

    @pl.when(i == 0)
    def _():
        sum_ref[...] = jnp.zeros_like(sum_ref)

    sum_ref[...] += jnp.sum(jnp.abs(tap), axis=0, keepdims=True)


def _hy_taps(feats, w1p, b1, w2, b2, w3, freq, deltas):
    rws = feats.shape[0]
    hid = w2.shape[0]
    w = w3.shape[1] // 2
    tr = _pick(rws, (1024, 512, 256, 128))
    ct = _pick(w, (256, 128))
    nj = w // ct
    c2 = lambda a: pl.BlockSpec(a.shape, lambda j, i: (0, 0))
    return pl.pallas_call(
        _hy_taps_kernel,
        grid=(nj, rws // tr),
        in_specs=[pl.BlockSpec((tr, LANE), lambda j, i: (i, 0)),
                  c2(w1p), pl.BlockSpec((1, hid), lambda j, i: (0, 0)),
                  c2(w2), pl.BlockSpec((1, hid), lambda j, i: (0, 0)),
                  pl.BlockSpec((hid, ct), lambda j, i: (0, j)),
                  pl.BlockSpec((hid, ct), lambda j, i: (0, nj + j)),
                  pl.BlockSpec((1, hid), lambda j, i: (0, 0)),
                  pl.BlockSpec((1, ct), lambda j, i: (0, j))],
        out_specs=[pl.BlockSpec((tr, ct), lambda j, i: (i, j)),
                   pl.BlockSpec((1, ct), lambda j, i: (0, j))],
        out_shape=[jax.ShapeDtypeStruct((rws, w), F32), jax.ShapeDtypeStruct((1, w), F32)],
        compiler_params=_params(("parallel", "arbitrary"), 32 << 20),
    )(feats, w1p, b1.reshape(1, hid), w2, b2.reshape(1, hid), w3, w3, freq.reshape(1, hid),
      deltas.reshape(1, w))


def _dft1_kernel(g_ref, x_ref, o_ref):
    o_ref[...] = lax.dot_general(g_ref[...], x_ref[...].astype(BF16), (((2,), (1,)), ((0,), (0,))),
                                 preferred_element_type=F32)


def _dft1(g, x):
    nb, m2, k = g.shape
    c = x.shape[2]
    bt = _pick(nb, (8, 4, 2))
    ct = _pick(c, (256, 128))
    return pl.pallas_call(
        _dft1_kernel,
        grid=(c // ct, nb // bt),
        in_specs=[pl.BlockSpec((bt, m2, k), lambda j, i: (i, 0, 0)),
                  pl.BlockSpec((bt, k, ct), lambda j, i: (i, 0, j))],
        out_specs=pl.BlockSpec((bt, m2, ct), lambda j, i: (i, 0, j)),
        out_shape=jax.ShapeDtypeStruct((nb, m2, c), F32),
        compiler_params=_params(("parallel", "arbitrary"), 32 << 20),
    )(g, x)


def _spec_kernel(a_ref, f2_ref, sc_ref, o_ref):
    for l in range(a_ref.shape[0]):
        o_ref[l] = jnp.dot(f2_ref[...], a_ref[l].astype(BF16), preferred_element_type=F32) * sc_ref[...]


def _spec(a, f2, scale):
    na, m2, c = a.shape
    kt = _pick(na, (8, 4, 2))
    ct = _pick(c, (256, 128))
    return pl.pallas_call(
        _spec_kernel,
        grid=(c // ct, na // kt),
        in_specs=[pl.BlockSpec((kt, m2, ct), lambda j, i: (i, 0, j)),
                  pl.BlockSpec((m2, m2), lambda j, i: (0, 0)),
                  pl.BlockSpec((1, ct), lambda j, i: (0, j))],
        out_specs=pl.BlockSpec((kt, m2, ct), lambda j, i: (i, 0, j)),
        out_shape=jax.ShapeDtypeStruct((na, m2, c), F32),
        compiler_params=_params(("parallel", "arbitrary"), 32 << 20),
    )(a, f2, scale)


def _conv_mid_kernel(a_ref, f2_ref, h_ref, g3_ref, o_ref):
    nbh = a_ref.shape[1] // 2
    for l in range(a_ref.shape[0]):
        x = jnp.dot(f2_ref[...], a_ref[l].astype(BF16), preferred_element_type=F32)
        xr, xi = x[:nbh], x[nbh:]
        hr, hi = h_ref[l, :nbh], h_ref[l, nbh:]
        y = jnp.concatenate([xr * hr - xi * hi, xr * hi + xi * hr], axis=0)
        o_ref[l] = jnp.dot(g3_ref[l], y.astype(BF16), preferred_element_type=F32)


def _conv_mid(a, f2, h, g3):
    na, m2, c = a.shape
    kt = _pick(na, (8, 4, 2))
    ct = _pick(c, (256, 128))
    slab = pl.BlockSpec((kt, m2, ct), lambda j, i: (i, 0, j))
    return pl.pallas_call(
        _conv_mid_kernel,
        grid=(c // ct, na // kt),
        in_specs=[slab, pl.BlockSpec((m2, m2), lambda j, i: (0, 0)), slab,
                  pl.BlockSpec((kt, m2, m2), lambda j, i: (i, 0, 0))],
        out_specs=slab,
        out_shape=jax.ShapeDtypeStruct((na, m2, c), F32),
        compiler_params=_params(("parallel", "arbitrary"), 32 << 20),
    )(a, f2, h, g3)


def _conv_out_kernel(b_ref, f4_ref, o_ref, *, scale):
    for l in range(b_ref.shape[0]):
        o_ref[l] = jnp.dot(f4_ref[...], b_ref[l].astype(BF16), preferred_element_type=F32) * scale


def _conv_out(bm, f4, scale):
    nb, m2, c = bm.shape
    mo = f4.shape[0]
    pt = _pick(nb, (8, 4, 2))
    ct = _pick(c, (256, 128))
    return pl.pallas_call(
        functools.partial(_conv_out_kernel, scale=scale),
        grid=(c // ct, nb // pt),
        in_specs=[pl.BlockSpec((pt, m2, ct), lambda j, i: (i, 0, j)),
                  pl.BlockSpec((mo, m2), lambda j, i: (0, 0))],
        out_specs=pl.BlockSpec((pt, mo, ct), lambda j, i: (i, 0, j)),
        out_shape=jax.ShapeDtypeStruct((nb, mo, c), F32),
        compiler_params=_params(("parallel", "arbitrary"), 32 << 20),
    )(bm, f4)


def _ctx_conv_kernel(z_ref, ext_ref, sc_ref, o_ref):
    n = z_ref.shape[1]

    def body(s, acc):
        return acc + ext_ref[pl.ds(n - s, n), :] * z_ref[0, pl.ds(s, 1), :]

    acc = lax.fori_loop(0, n, body, jnp.zeros(o_ref.shape[1:], F32))
    o_ref[0] = acc * sc_ref[...]


def _ctx_conv(z, ext, scale):
    b, n, w = z.shape
    ct = LANE
    return pl.pallas_call(
        _ctx_conv_kernel,
        grid=(b, w // ct),
        in_specs=[pl.BlockSpec((1, n, ct), lambda bi, j: (bi, 0, j)),
                  pl.BlockSpec((2 * n, ct), lambda bi, j: (0, j)),
                  pl.BlockSpec((1, ct), lambda bi, j: (0, j))],
        out_specs=pl.BlockSpec((1, n, ct), lambda bi, j: (bi, 0, j)),
        out_shape=jax.ShapeDtypeStruct((b, n, w), F32),
        compiler_params=_params(("parallel", "arbitrary"), 16 << 20),
    )(z, ext, scale)


def _hy_post_kernel(y_ref, z_ref, x0_ref, bias_ref, o_ref):
    o_ref[0] = (x0_ref[0] * (y_ref[0] + bias_ref[...] * z_ref[0])).astype(o_ref.dtype)


def _hy_post(y, z, x0, bias):
    b, t, w = y.shape
    tm = _pick(t, (768, 384, 256, 128, 64))
    s = pl.BlockSpec((1, tm, w), lambda bi, i: (bi, i, 0))
    return pl.pallas_call(
        _hy_post_kernel,
        grid=(b, t // tm),
        in_specs=[s, s, s, pl.BlockSpec((1, w), lambda bi, i: (0, 0))],
        out_specs=s,
        out_shape=jax.ShapeDtypeStruct((b, t, w), BF16),
        compiler_params=_params(("parallel", "arbitrary"), 32 << 20),
    )(y, z, x0, bias.reshape(1, w))


def _dft_factors(n):
    m = 2 * n
    na = 1 << ((m.bit_length() - 1) // 2)
    return na, m // na


def _cis(num, den, sign):
    ang = (num % den).astype(F32) * (2.0 * math.pi / den)
    return jnp.cos(ang), sign * jnp.sin(ang)


def _cblock(cr, ci):
    return jnp.concatenate([jnp.concatenate([cr, -ci], axis=-1),
                            jnp.concatenate([ci, cr], axis=-1)], axis=-2)


def _dft_tables(n):
    na, nb = _dft_factors(n)
    m = na * nb
    ah = na // 2
    ar = lambda k: jnp.arange(k, dtype=jnp.int32)
    ka, bb = ar(na)[None, :, None], ar(nb)[:, None, None]
    g1c = _cblock(*_cis(ka * (nb * ar(ah)[None, None, :] + bb), m, -1.0))
    g1t = jnp.concatenate(_cis(ka * (nb * ar(na)[None, None, :] + bb), m, -1.0), axis=-2)
    f2 = _cblock(*_cis(ar(nb)[:, None] * ar(nb)[None, :], nb, -1.0))
    g3 = _cblock(*_cis(ar(nb)[None, :, None] * (ar(na)[:, None, None] + na * ar(nb)[None, None, :]), m, 1.0))
    f4 = _cblock(*_cis(ar(ah)[:, None] * ar(na)[None, :], na, 1.0))
    return tuple(t.astype(BF16) for t in (g1c, g1t, f2, g3, f4))


def _filter_feats(n, order):
    bands_n = (HY_EMB - 1) // 2
    t = jnp.linspace(0.0, 1.0, n, dtype=F32)[:, None]
    lag = jnp.arange(n, dtype=F32)[:, None]
    bands = jnp.linspace(1e-4, bands_n - 1, bands_n, dtype=F32)[None, :]
    ang = 2.0 * math.pi * lag * bands / n
    zf = jnp.concatenate([t, jnp.cos(ang), -jnp.sin(ang)], axis=-1)
    if order == "dft":
        na, nb = _dft_factors(n)
        m = (np.arange(na)[None, :] * nb + np.arange(nb)[:, None]).reshape(-1)
    else:
        m = (np.arange(2 * n) - n) % (2 * n)
    lag_of = np.where(m < n, m, np.where(m == n, 0, 2 * n - m))
    cf = np.where(m == 0, 0.5, np.where(m < n, 1.0, 0.0)).astype(np.float32)
    cb = np.where(m == 0, 0.5, np.where(m > n, 1.0, 0.0)).astype(np.float32)
    pad = jnp.zeros((2 * n, LANE - HY_EMB - 2), F32)
    return jnp.concatenate([zf[lag_of], jnp.asarray(cf)[:, None], jnp.asarray(cb)[:, None], pad], axis=-1)


def _hy_deltas(w):
    return jnp.abs(jnp.linspace(math.log(HY_DECAY_TARGET) / HY_SLOW_PCT,
                                math.log(HY_DECAY_TARGET) / HY_FAST_PCT, w, dtype=F32))


def _long_conv_latent(z, taps_perm, inv_l1):
    b, n, w = z.shape
    na, nb = _dft_factors(n)
    ah = na // 2
    g1c, g1t, f2, g3, f4 = _dft_tables(n)
    h = _spec(_dft1(g1t, taps_perm.reshape(nb, na, w)).reshape(nb, 2, na, w).transpose(2, 1, 0, 3)
              .reshape(na, 2 * nb, w), f2, inv_l1)
    outs = []
    for pair in range(b // 2):
        zz = z[2 * pair:2 * pair + 2].astype(BF16).reshape(2, ah, nb, w).transpose(2, 0, 1, 3).reshape(nb, na, w)
        a = _dft1(g1c, zz).reshape(nb, 2, na, w).transpose(2, 1, 0, 3).reshape(na, 2 * nb, w)
        bm = _conv_mid(a, f2, h, g3).reshape(na, 2, nb, w).transpose(2, 1, 0, 3).reshape(nb, 2 * na, w)
        y = _conv_out(bm, f4, 1.0 / (na * nb))
        outs.append(y.reshape(nb, 2, ah, w).transpose(1, 2, 0, 3).reshape(2, n, w))
    return jnp.concatenate(outs, axis=0) if len(outs) > 1 else outs[0]


def _pack_layer(w_in, rw_mu, dims):
    d_model = w_in.shape[0]
    w, dl, al, gl = dims
    ld, la, lg = _rup(dl, LANE), _rup(al, LANE), _rup(gl, LANE)
    rw_cols = 3 * w + 2 * dl + 2 * al + gl
    na = _rup(3 * w + 2 * ld + 2 * la + lg, 512)
    pieces = [(0, 3 * w, 0), (3 * w, dl, 3 * w), (3 * w + dl, dl, 3 * w + ld),
              (3 * w + 2 * dl, al, 3 * w + 2 * ld), (3 * w + 2 * dl + al, al, 3 * w + 2 * ld + la),
              (3 * w + 2 * dl + 2 * al, gl, 3 * w + 2 * ld + 2 * la)]
    wa = jnp.zeros((d_model, na), BF16)
    mu = jnp.zeros((1, na), F32)
    orig = np.full((na,), rw_cols - 1, np.int64)
    for src, width, dst in pieces:
        wa = wa.at[:, dst:dst + width].set(w_in[:, src:src + width].astype(BF16))
        mu = mu.at[0, dst:dst + width].set(rw_mu[src:src + width])
        orig[dst:dst + width] = np.arange(src, src + width)
    grp = np.stack([orig // (rw_cols // 4), orig // (rw_cols // 2)]).astype(np.int32)
    grp = np.minimum(grp, np.array([[3], [1]])).astype(np.int32)
    return wa, mu, jnp.asarray(grp), (w, ld, la, lg), rw_cols


def _pad_rows(a, rows):
    return jnp.pad(a, [(0, 0)] * (a.ndim - 2) + [(0, rows - a.shape[-2]), (0, 0)])


def kernel(x, c, ctx, c_ctx, ada_w, ada_b, norm1_g, norm2_g, w_in, rw_mu, rw_w0, rw_w_up, rw_a0, rw_a_up, rw_g_up, rw_k_k, rw_k_a, rw_r_k, rw_ln_g, rw_ln_b, hg_lower_bounds, hg_norm_g, hy_conv_w, hy_conv_b, hy_f_w1, hy_f_b1, hy_f_w2, hy_f_b2, hy_f_w3, hy_freq, hy_bias, w_branch_a, w_branch_b, w_branch_c, w_out, ffn_w_gate, ffn_w_up, ffn_w_down, final_norm_g):
    bsz, n_lat, d = x.shape
    n_ctx = ctx.shape[1]
    depth = w_in.shape[0]
    w = rw_k_k.shape[1]
    dims = (w, rw_w_up.shape[2], rw_a_up.shape[2], rw_g_up.shape[1])
    assert bsz % 2 == 0 and n_lat % GRID_W == 0 and n_ctx % CHUNK == 0 and w % LANE == 0
    rw_cols = 3 * w + 2 * dims[1] + 2 * dims[2] + dims[3]
    hg_end = rw_cols + 5 * w
    hy_end = hg_end + 3 * w

    xs = jnp.concatenate([x, ctx], axis=1)
    cvec = jnp.zeros((_rup(bsz + 1, 8), d), F32).at[:bsz].set(c).at[bsz].set(c_ctx)
    lb_cum = jnp.cumsum(jax.nn.softmax(hg_lower_bounds.astype(F32), axis=0), axis=0)
    deltas = _hy_deltas(w)

    for layer in range(depth):
        need_ctx = layer < depth - 1
        mod = _ada(cvec, ada_w[layer], ada_b[layer])[:bsz + 1].reshape(bsz + 1, 1, 6, d)
        mods = [mod[:, :, s, :] for s in range(6)]

        wa, mu_p, grp, pdims, _ = _pack_layer(w_in[layer], rw_mu[layer], dims)
        _, ld, la, lg = pdims
        h1 = _modnorm(xs, norm1_g[layer], mods[0], mods[1], n_lat)
        pa = _matmul(h1, wa, F32)
        phg = _matmul(h1, w_in[layer][:, rw_cols:hg_end].astype(BF16), F32)
        phy = _matmul(h1, w_in[layer][:, hg_end:hy_end].astype(BF16), F32)
        pgate = _matmul(h1, w_in[layer][:, hy_end:].astype(BF16), BF16)

        r, v, kn, g, bonus, lw, kd, bb = _rw_prep(
            pa, grp, mu_p, rw_w0[layer].reshape(2, 1, w), _pad_rows(rw_w_up[layer], ld),
            rw_a0[layer].reshape(2, 1, w), _pad_rows(rw_a_up[layer], la), _pad_rows(rw_g_up[layer], lg),
            rw_k_k[layer].reshape(1, w), rw_k_a[layer].reshape(1, w), rw_r_k[layer].reshape(1, w),
            n_lat, pdims)
        y_rw = _rw_scan(r, v, kn, lw, kd, bb, n_lat)
        oa = _rw_finish(y_rw, bonus, g, rw_ln_g[layer], rw_ln_b[layer])

        lb = lb_cum[layer] - lb_cum[0]
        lbt = jnp.stack([jnp.log(lb), jnp.log1p(-lb), 1.0 - lb], axis=1)
        o_hg = _hg_scan(phg, lbt, n_lat, w)
        ob = _hg_finish(o_hg, phg, hg_norm_g[layer], w)

        x0, z = _hy_pre(phy, hy_conv_w[layer], hy_conv_b[layer], n_lat)
        w1p = _pad_rows(hy_f_w1[layer], LANE)
        filt = (w1p, hy_f_b1[layer], hy_f_w2[layer], hy_f_b2[layer], hy_f_w3[layer], hy_freq[layer], deltas)
        taps, l1 = _hy_taps(_filter_feats(n_lat, "dft"), *filt)
        y_lat = _long_conv_latent(z[:, :n_lat], taps, 1.0 / l1)
        if need_ctx:
            ext, l1c = _hy_taps(_filter_feats(n_ctx, "lag"), *filt)
            y_ctx = _ctx_conv(z[:, n_lat:], ext, 1.0 / l1c)
        else:
            y_ctx = jnp.zeros((bsz, n_ctx, w), F32)
        oc = _hy_post(jnp.concatenate([y_lat, y_ctx], axis=1), z, x0, hy_bias[layer])

        ym = _merge(oa, ob, oc, w_branch_a[layer].astype(BF16), w_branch_b[layer].astype(BF16),
                    w_branch_c[layer].astype(BF16), pgate)
        xs = _matmul_resid(ym, w_out[layer].astype(BF16), xs, mods[2], n_lat)

        hid = ffn_w_gate.shape[2]
        hp = _rup(hid, 1024) if hid > 1024 else _rup(hid, LANE)
        padc = lambda a: jnp.pad(a.astype(BF16), ((0, 0), (0, hp - hid)))
        h2 = _modnorm(xs, norm2_g[layer], mods[3], mods[4], n_lat)
        act = _swiglu_up(h2, padc(ffn_w_gate[layer]), padc(ffn_w_up[layer]))
        wd = jnp.pad(ffn_w_down[layer].astype(BF16), ((0, hp - hid), (0, 0)))
        xs = _matmul_resid(act, wd, xs, mods[5], n_lat)

    return _final_norm(xs, final_norm_g, n_lat)
```

```python
import functools
import math

import numpy as np
import jax
import jax.numpy as jnp
from jax import lax
from jax.experimental import pallas as pl
from jax.experimental.pallas import tpu as pltpu

F32 = jnp.float32
BF16 = jnp.bfloat16

GRID_W = 64
CHUNK = 64
SUB = 16
NORM_EPS = 1e-6
RW_HEAD = 64
RW_GN_EPS = 64e-5
HG_EXPAND = 128
HY_EMB = 33
HY_DECAY_TARGET = 1e-2
HY_FAST_PCT = 0.3
HY_SLOW_PCT = 1.5
LANE = 128
VMEM_CAP = 56 * 1024 * 1024
VMEM_SLACK = 8 * 1024 * 1024


def _params(sem, vmem_bytes):
    return pltpu.CompilerParams(dimension_semantics=sem,
                                vmem_limit_bytes=int(min(max(vmem_bytes + VMEM_SLACK, 16 << 20), VMEM_CAP)))


def _pick(n, cands):
    for c in cands:
        if n % c == 0:
            return c
    return n


def _rup(n, m):
    return -(-n // m) * m


def _bdot(a, b):
    return jnp.dot(a.astype(BF16), b.astype(BF16), preferred_element_type=F32)


def _bdot_nt(a, b):
    return lax.dot_general(a.astype(BF16), b.astype(BF16), (((1,), (1,)), ((), ())),
                           preferred_element_type=F32)


def _bdot_tn(a, b):
    return lax.dot_general(a.astype(BF16), b.astype(BF16), (((0,), (0,)), ((), ())),
                           preferred_element_type=F32)


def _split(x):
    hi = x.astype(BF16)
    lo = (x - hi.astype(F32)).astype(BF16)
    return hi, lo


def _mask_dot(m, x):
    hi, lo = _split(x)
    return (jnp.dot(m, hi, preferred_element_type=F32) + jnp.dot(m, lo, preferred_element_type=F32))


def _x_mask_dot(x, m):
    hi, lo = _split(x)
    return (jnp.dot(hi, m, preferred_element_type=F32) + jnp.dot(lo, m, preferred_element_type=F32))


def _sigmoid(x):
    return 1.0 / (1.0 + jnp.exp(-x))


def _order_masks(d, n, reps=1):
    row = lax.broadcasted_iota(jnp.int32, (n, reps * n), 0)
    col = lax.broadcasted_iota(jnp.int32, (n, reps * n), 1) % n
    diff = (row - col) * (1 - 2 * d)
    return diff >= 0, diff > 0


def _ada_kernel(c_ref, w_ref, b_ref, o_ref):
    c = c_ref[...]
    o_ref[...] = _bdot(c * _sigmoid(c), w_ref[...]) + b_ref[...]


def _ada(cvec, w, b):
    rows, d = cvec.shape
    n = w.shape[1]
    tn = _pick(n, (512, 256, 128))
    return pl.pallas_call(
        _ada_kernel, name="ada",
        grid=(n // tn,),
        in_specs=[pl.BlockSpec((rows, d), lambda j: (0, 0)),
                  pl.BlockSpec((d, tn), lambda j: (0, j)),
                  pl.BlockSpec((1, tn), lambda j: (0, j))],
        out_specs=pl.BlockSpec((rows, tn), lambda j: (0, j)),
        out_shape=jax.ShapeDtypeStruct((rows, n), F32),
        compiler_params=_params(("arbitrary",), 3 * d * tn * 4),
    )(cvec, w, b.reshape(1, n))


def _modnorm_kernel(x_ref, g_ref, shl_ref, scl_ref, shc_ref, scc_ref, o_ref, *, n_lat):
    tm = x_ref.shape[1]
    x = x_ref[0]
    y = x * lax.rsqrt(jnp.mean(x * x, axis=-1, keepdims=True) + NORM_EPS) * g_ref[...]
    pos = pl.program_id(1) * tm + lax.broadcasted_iota(jnp.int32, (tm, 1), 0)
    is_ctx = pos >= n_lat
    sc = jnp.where(is_ctx, scc_ref[0], scl_ref[0])
    sh = jnp.where(is_ctx, shc_ref[0], shl_ref[0])
    o_ref[0] = (y * (1.0 + sc) + sh).astype(o_ref.dtype)


def _modnorm(x, gain, shift, scale, n_lat):
    b, t, d = x.shape
    tm = _pick(t, (256, 128, 64))
    vec = lambda f: pl.BlockSpec((1, 1, d), f)
    return pl.pallas_call(
        functools.partial(_modnorm_kernel, n_lat=n_lat), name="modnorm",
        grid=(b, t // tm),
        in_specs=[pl.BlockSpec((1, tm, d), lambda bi, i: (bi, i, 0)),
                  pl.BlockSpec((1, d), lambda bi, i: (0, 0)),
                  vec(lambda bi, i: (bi, 0, 0)), vec(lambda bi, i: (bi, 0, 0)),
                  vec(lambda bi, i: (b, 0, 0)), vec(lambda bi, i: (b, 0, 0))],
        out_specs=pl.BlockSpec((1, tm, d), lambda bi, i: (bi, i, 0)),
        out_shape=jax.ShapeDtypeStruct((b, t, d), BF16),
        compiler_params=_params(("parallel", "arbitrary"), 6 * tm * d * 4),
    )(x, gain.reshape(1, d), shift, scale, shift, scale)


def _rmsnorm_kernel(x_ref, g_ref, o_ref):
    x = x_ref[0]
    o_ref[0] = x * lax.rsqrt(jnp.mean(x * x, axis=-1, keepdims=True) + NORM_EPS) * g_ref[...]


def _final_norm(x, gain, n_lat):
    b, _, d = x.shape
    tm = _pick(n_lat, (256, 128, 64))
    return pl.pallas_call(
        _rmsnorm_kernel, name="final_norm",
        grid=(b, n_lat // tm),
        in_specs=[pl.BlockSpec((1, tm, d), lambda bi, i: (bi, i, 0)),
                  pl.BlockSpec((1, d), lambda bi, i: (0, 0))],
        out_specs=pl.BlockSpec((1, tm, d), lambda bi, i: (bi, i, 0)),
        out_shape=jax.ShapeDtypeStruct((b, n_lat, d), F32),
        compiler_params=_params(("parallel", "arbitrary"), 6 * tm * d * 4),
    )(x, gain.reshape(1, d))


def _mm_kernel(x_ref, w_ref, o_ref):
    o_ref[0] = jnp.dot(x_ref[0], w_ref[...], preferred_element_type=F32).astype(o_ref.dtype)


def _matmul(x, w, out_dtype, col_off=0, n=None):
    b, t, k = x.shape
    n = w.shape[1] if n is None else n
    tm = _pick(t, (768, 384, 256, 128, 64))
    tn = _pick(math.gcd(n, col_off) if col_off else n, (1024, 512, 256, 128))
    osz = jnp.dtype(out_dtype).itemsize
    j0 = col_off // tn
    return pl.pallas_call(
        _mm_kernel, name="in_proj",
        grid=(b, t // tm, n // tn),
        in_specs=[pl.BlockSpec((1, tm, k), lambda bi, i, j: (bi, i, 0)),
                  pl.BlockSpec((k, tn), lambda bi, i, j: (0, j0 + j))],
        out_specs=pl.BlockSpec((1, tm, tn), lambda bi, i, j: (bi, i, j)),
        out_shape=jax.ShapeDtypeStruct((b, t, n), out_dtype),
        compiler_params=_params(("parallel", "parallel", "arbitrary"),
                                2 * (tm * k * 2 + k * tn * 2 + tm * tn * osz) + tm * tn * 4),
    )(x, w)


def _swiglu_kernel(x_ref, wg_ref, wu_ref, o_ref):
    x = x_ref[0]
    g = jnp.dot(x, wg_ref[...], preferred_element_type=F32)
    u = jnp.dot(x, wu_ref[...], preferred_element_type=F32)
    o_ref[0] = (g * _sigmoid(g) * u).astype(o_ref.dtype)


def _swiglu_up(x, wg, wu):
    b, t, k = x.shape
    n = wg.shape[1]
    tm = _pick(t, (768, 384, 256, 128, 64))
    tn = _pick(n, (512, 256, 128))
    return pl.pallas_call(
        _swiglu_kernel, name="swiglu_up",
        grid=(b, t // tm, n // tn),
        in_specs=[pl.BlockSpec((1, tm, k), lambda bi, i, j: (bi, i, 0)),
                  pl.BlockSpec((k, tn), lambda bi, i, j: (0, j)),
                  pl.BlockSpec((k, tn), lambda bi, i, j: (0, j))],
        out_specs=pl.BlockSpec((1, tm, tn), lambda bi, i, j: (bi, i, j)),
        out_shape=jax.ShapeDtypeStruct((b, t, n), BF16),
        compiler_params=_params(("parallel", "parallel", "arbitrary"),
                                2 * (tm * k * 2 + 2 * k * tn * 2 + tm * tn * 2) + 3 * tm * tn * 4),
    )(x, wg, wu)


def _resid_kernel(x_ref, w_ref, r_ref, gl_ref, gc_ref, o_ref, acc_ref, *, n_lat):
    kk = pl.program_id(3)

    @pl.when(kk == 0)
    def _():
        acc_ref[...] = jnp.zeros_like(acc_ref)

    acc_ref[...] += jnp.dot(x_ref[0], w_ref[...], preferred_element_type=F32)

    @pl.when(kk == pl.num_programs(3) - 1)
    def _():
        tm = acc_ref.shape[0]
        pos = pl.program_id(1) * tm + lax.broadcasted_iota(jnp.int32, (tm, 1), 0)
        gate = jnp.where(pos >= n_lat, gc_ref[0], gl_ref[0])
        o_ref[0] = r_ref[0] + gate * acc_ref[...]


def _matmul_resid(x, w, res, gate, n_lat):
    b, t, k = x.shape
    n = w.shape[1]
    tm = _pick(t, (768, 384, 256, 128, 64))
    tn = _pick(n, (1024, 512, 256, 128))
    tk = k if k <= 4096 else _pick(k, (2816, 2048, 1024, 512, 256, 128))
    return pl.pallas_call(
        functools.partial(_resid_kernel, n_lat=n_lat), name="proj_resid",
        grid=(b, t // tm, n // tn, k // tk),
        in_specs=[pl.BlockSpec((1, tm, tk), lambda bi, i, j, l: (bi, i, l)),
                  pl.BlockSpec((tk, tn), lambda bi, i, j, l: (l, j)),
                  pl.BlockSpec((1, tm, tn), lambda bi, i, j, l: (bi, i, j)),
                  pl.BlockSpec((1, 1, tn), lambda bi, i, j, l: (bi, 0, j)),
                  pl.BlockSpec((1, 1, tn), lambda bi, i, j, l: (b, 0, j))],
        out_specs=pl.BlockSpec((1, tm, tn), lambda bi, i, j, l: (bi, i, j)),
        out_shape=jax.ShapeDtypeStruct((b, t, n), F32),
        scratch_shapes=[pltpu.VMEM((tm, tn), F32)],
        compiler_params=_params(("parallel", "parallel", "arbitrary", "arbitrary"),
                                2 * (tm * tk * 2 + tk * tn * 2 + 2 * tm * tn * 4) + 2 * tm * tn * 4),
    )(x, w, res, gate, gate)


def _merge_kernel(oa_ref, ob_ref, oc_ref, wa_ref, wb_ref, wc_ref, ga_ref, gb_ref, gc_ref, o_ref):
    y = _sigmoid(ga_ref[0].astype(F32)) * jnp.dot(oa_ref[0], wa_ref[...], preferred_element_type=F32)
    y += _sigmoid(gb_ref[0].astype(F32)) * jnp.dot(ob_ref[0], wb_ref[...], preferred_element_type=F32)
    y += _sigmoid(gc_ref[0].astype(F32)) * jnp.dot(oc_ref[0], wc_ref[...], preferred_element_type=F32)
    o_ref[0] = y.astype(o_ref.dtype)


def _merge(oa, ob, oc, wa, wb, wc, pgate):
    b, t, kw = oa.shape
    d = wa.shape[1]
    tm = _pick(t, (768, 384, 256, 128, 64))
    tn = _pick(d, (1024, 512, 256, 128))
    nj = d // tn
    br = lambda: pl.BlockSpec((1, tm, kw), lambda bi, i, j: (bi, i, 0))
    wt = lambda: pl.BlockSpec((kw, tn), lambda bi, i, j: (0, j))
    gt = lambda s: pl.BlockSpec((1, tm, tn), lambda bi, i, j: (bi, i, s * nj + j))
    gsz = jnp.dtype(pgate.dtype).itemsize
    return pl.pallas_call(
        _merge_kernel, name="merge",
        grid=(b, t // tm, nj),
        in_specs=[br(), br(), br(), wt(), wt(), wt(), gt(0), gt(1), gt(2)],
        out_specs=pl.BlockSpec((1, tm, tn), lambda bi, i, j: (bi, i, j)),
        out_shape=jax.ShapeDtypeStruct((b, t, d), BF16),
        compiler_params=_params(("parallel", "parallel", "arbitrary"),
                                2 * (3 * tm * kw * 2 + 3 * kw * tn * 2 + 3 * tm * tn * gsz + tm * tn * 2)
                                + 4 * tm * tn * 4),
    )(oa, ob, oc, wa, wb, wc, pgate, pgate, pgate)


def _pair_sum_matrix():
    r = lax.broadcasted_iota(jnp.int32, (LANE, LANE), 0) // RW_HEAD
    c = lax.broadcasted_iota(jnp.int32, (LANE, LANE), 1) // RW_HEAD
    return jnp.where(r == c, 1.0, 0.0).astype(BF16)


def _head_sums(x, hs):
    rows, w = x.shape
    nt = w // LANE
    stacked = jnp.concatenate([x[:, j * LANE:(j + 1) * LANE] for j in range(nt)], axis=0)
    s = _x_mask_dot(stacked, hs)
    return jnp.concatenate([s[j * rows:(j + 1) * rows] for j in range(nt)], axis=1)


def _rw_prep_kernel(prev_ref, cur_ref, next_ref, grp_ref, mu_ref, w0_ref, wup_ref, a0_ref, aup_ref,
                    gup_ref, kk_ref, ka_ref, rk_ref,
                    r_ref, v_ref, kn_ref, g_ref, bon_ref, lw_ref, kd_ref, bb_ref,
                    *, n_lat_chunks, n_chunks, w, ld, la, lg):
    i = pl.program_id(1)
    c = CHUNK
    cur = cur_ref[0]
    prev = prev_ref[0]
    nxt = next_ref[0]
    is_ctx = i >= n_lat_chunks
    first = jnp.logical_or(i == 0, i == n_lat_chunks)
    last = jnp.logical_or(i == n_lat_chunks - 1, i == n_chunks - 1)
    row = lax.broadcasted_iota(jnp.int32, (c, 1), 0)
    carry_in = jnp.where(jnp.logical_and(is_ctx, jnp.logical_not(first)), 1.0, 0.0)
    carry_out = jnp.where(jnp.logical_and(is_ctx, jnp.logical_not(last)), 1.0, 0.0)
    tm1 = jnp.where(row == 0, prev[c - 1:c] * carry_in, pltpu.roll(cur, 1, 0))
    tp1 = jnp.where(row == c - 1, nxt[0:1] * carry_out, pltpu.roll(cur, c - 1, 0))
    lat_up = jnp.where(jnp.logical_or(is_ctx, first), 0.0, 1.0)
    lat_dn = jnp.where(jnp.logical_or(is_ctx, last), 0.0, 1.0)
    code = jnp.where(is_ctx, grp_ref[1:2], grp_ref[0:1])
    shifted = jnp.where(code == 0, tm1,
                        jnp.where(code == 1, tp1,
                                  jnp.where(code == 2, prev * lat_up, nxt * lat_dn)))
    m = cur + mu_ref[...] * (shifted - cur)

    r = m[:, 0:w]
    k = m[:, w:2 * w]
    v = m[:, 2 * w:3 * w]
    o = 3 * w
    wd = (m[:, o:o + ld], m[:, o + ld:o + 2 * ld])
    ad = (m[:, o + 2 * ld:o + 2 * ld + la], m[:, o + 2 * ld + la:o + 2 * ld + 2 * la])
    gd = m[:, o + 2 * ld + 2 * la:o + 2 * ld + 2 * la + lg]

    hs = _pair_sum_matrix()
    g_ref[0] = _bdot(_sigmoid(gd), gup_ref[...])
    kx = k * kk_ref[...]
    kn = kx * lax.rsqrt(jnp.maximum(_head_sums(kx * kx, hs), 1e-24))
    r_ref[0] = r
    v_ref[0] = v
    kn_ref[0] = kn
    bonus = jnp.zeros_like(r)
    for d in range(2):
        wl = w0_ref[d] + _bdot(jnp.tanh(wd[d]), wup_ref[d])
        sp = jnp.maximum(-wl, 0.0) + jnp.log(1.0 + jnp.exp(-jnp.abs(wl)))
        lw_ref[d, 0] = -jnp.exp(-sp - 0.5)
        a = _sigmoid(a0_ref[d] + _bdot(ad[d], aup_ref[d]))
        kd = k * (1.0 + (a - 1.0) * ka_ref[...])
        kd_ref[d, 0] = kd
        bb_ref[d, 0] = kn * a
        bonus += r * kd * rk_ref[...]
    bon_ref[0] = _head_sums(bonus, hs) * v


def _rw_prep(pa, grp, mu, w0, wup, a0, aup, gup, k_k, k_a, r_k, n_lat, dims):
    b, t, na = pa.shape
    w, ld, la, lg = dims
    nc = t // CHUNK
    nlc = n_lat // CHUNK
    blk = lambda f: pl.BlockSpec((1, CHUNK, na), f)
    full = lambda a: pl.BlockSpec(a.shape, lambda bi, i: (0,) * a.ndim)
    o1 = pl.BlockSpec((1, CHUNK, w), lambda bi, i: (bi, i, 0))
    o2 = pl.BlockSpec((2, 1, CHUNK, w), lambda bi, i: (0, bi, i, 0))
    s1 = jax.ShapeDtypeStruct((b, t, w), F32)
    s2 = jax.ShapeDtypeStruct((2, b, t, w), F32)
    consts = (grp, mu, w0, wup, a0, aup, gup, k_k, k_a, r_k)
    return pl.pallas_call(
        functools.partial(_rw_prep_kernel, n_lat_chunks=nlc, n_chunks=nc, w=w, ld=ld, la=la, lg=lg),
        name="rw_prep",
        grid=(b, nc),
        in_specs=[blk(lambda bi, i: (bi, jnp.maximum(i - 1, 0), 0)),
                  blk(lambda bi, i: (bi, i, 0)),
                  blk(lambda bi, i: (bi, jnp.minimum(i + 1, nc - 1), 0))] + [full(a) for a in consts],
        out_specs=[o1, o1, o1, o1, o1, o2, o2, o2],
        out_shape=[s1, s1, s1, s1, s1, s2, s2, s2],
        compiler_params=_params(("parallel", "arbitrary"), 48 << 20),
    )(pa, pa, pa, *consts)


def _pdot(a, b):
    return lax.dot_general(a.astype(BF16), b.astype(BF16), (((2,), (1,)), ((0,), (0,))),
                           preferred_element_type=F32)


def _pdot_nt(a, b):
    return lax.dot_general(a.astype(BF16), b.astype(BF16), (((2,), (2,)), ((0,), (0,))),
                           preferred_element_type=F32)


def _pdot_tn(a, b):
    return lax.dot_general(a.astype(BF16), b.astype(BF16), (((1,), (1,)), ((0,), (0,))),
                           preferred_element_type=F32)


def _rw_scan_kernel(r_ref, v_ref, kn_ref, lw_ref, kd_ref, bb_ref, y_ref, s_ref, *, n_pairs):
    d = pl.program_id(0)
    i = pl.program_id(2)
    c = CHUNK
    wfull = n_pairs * LANE

    @pl.when(i == 0)
    def _():
        s_ref[...] = jnp.zeros_like(s_ref)

    incl_bf = jnp.where(_order_masks(d, c)[0], 1.0, 0.0).astype(BF16)
    incl2, strict2 = _order_masks(d, c, 2)
    lane_w = lax.broadcasted_iota(jnp.int32, (1, wfull), 1) % LANE
    w0 = jnp.where(lane_w < RW_HEAD, 1.0, 0.0)
    w1 = 1.0 - w0
    lane = lax.broadcasted_iota(jnp.int32, (1, 1, LANE), 2)
    m0 = jnp.where(lane < RW_HEAD, 1.0, 0.0)
    m1 = 1.0 - m0
    rowh = lax.broadcasted_iota(jnp.int32, (LANE, LANE), 0) // RW_HEAD
    colh = lax.broadcasted_iota(jnp.int32, (LANE, LANE), 1) // RW_HEAD
    blockdiag = rowh == colh
    last_row = jnp.where(d == 0, c - 1, 0)
    rsel = lax.broadcasted_iota(jnp.int32, (c, 1), 0) == last_row

    def pairs(x):
        return jnp.stack([x[:, p * LANE:(p + 1) * LANE] for p in range(n_pairs)], axis=0)

    def stack(x):
        return jnp.concatenate([x * m0, x * m1], axis=1)

    lw = lw_ref[0, 0]
    r = r_ref[0]
    v = v_ref[0]
    kd = kd_ref[0, 0]
    bb = bb_ref[0, 0]
    cw = _mask_dot(incl_bf, lw)
    tot = jnp.sum(jnp.where(rsel, cw, 0.0), axis=0, keepdims=True)
    w_inv = jnp.exp(-cw)
    w_rem = jnp.exp(tot - cw)
    a_t = -kn_ref[0] * jnp.exp(cw - lw)
    r_t = r * jnp.exp(cw)
    b_t = bb * w_inv
    k_t = kd * w_inv
    ar = pairs(jnp.concatenate([a_t, r_t], axis=0))
    bk = pairs(jnp.concatenate([b_t * w0, b_t * w1, k_t * w0, k_t * w1], axis=0))
    vst = pairs(jnp.concatenate([v * w0, v * w1], axis=0))
    uvr = pairs(jnp.concatenate([bb * w_rem, kd * w_rem], axis=0))
    v3 = pairs(v)
    decay = pairs(jnp.exp(tot))
    s0 = s_ref[...]

    sc = _pdot_nt(ar, bk)
    a_ab = jnp.where(strict2, sc[:, :c, :2 * c], 0.0)
    a_ak = jnp.where(strict2, sc[:, :c, 2 * c:], 0.0)
    r_b = jnp.where(incl2, sc[:, c:, :2 * c], 0.0)
    r_k = jnp.where(incl2, sc[:, c:, 2 * c:], 0.0)

    abd = stack(a_ab)
    q = _pdot(abd, abd)
    nsum = abd
    for _ in range(5):
        nsum = nsum + q + _pdot(q, nsum)
        q = _pdot(q, q)
    n_side = nsum[:, :c] + nsum[:, c:]

    art = _pdot_nt(ar, s0)
    rhs = art[:, :c] + _pdot(a_ak, vst)
    u = rhs + _pdot(n_side, stack(rhs))
    y = art[:, c:] + _pdot(jnp.concatenate([r_b, r_k], axis=2), jnp.concatenate([stack(u), vst], axis=1))
    for p in range(n_pairs):
        y_ref[0, 0, :, p * LANE:(p + 1) * LANE] = y[p]
    upd = _pdot_tn(jnp.concatenate([u, v3], axis=1), uvr)
    s_ref[...] = s0 * decay + jnp.where(blockdiag, upd, 0.0)


def _scan_chunk(d, i, n_lat_chunks, n_chunks):
    nctx = n_chunks - n_lat_chunks
    in_ctx = i < nctx
    fwd = jnp.where(in_ctx, n_lat_chunks + i, i - nctx)
    bwd = jnp.where(in_ctx, n_chunks - 1 - i, n_chunks - 1 - i)
    return jnp.where(d == 0, fwd, bwd)


def _rw_scan(r, v, kn, lw, kd, bb, n_lat):
    b, t, w = r.shape
    nc = t // CHUNK
    nlc = n_lat // CHUNK
    npair = w // LANE
    ch = functools.partial(_scan_chunk, n_lat_chunks=nlc, n_chunks=nc)
    s1 = pl.BlockSpec((1, CHUNK, w), lambda d, bi, i: (bi, ch(d, i), 0))
    s2 = pl.BlockSpec((1, 1, CHUNK, w), lambda d, bi, i: (d, bi, ch(d, i), 0))
    return pl.pallas_call(
        functools.partial(_rw_scan_kernel, n_pairs=npair), name="rw_scan",
        grid=(2, b, nc),
        in_specs=[s1, s1, s1, s2, s2, s2],
        out_specs=s2,
        out_shape=jax.ShapeDtypeStruct((2, b, t, w), F32),
        scratch_shapes=[pltpu.VMEM((npair, LANE, LANE), F32)],
        compiler_params=_params(("arbitrary", "arbitrary", "arbitrary"), 32 << 20),
    )(r, v, kn, lw, kd, bb)


def _rw_finish_kernel(y_ref, bon_ref, g_ref, lg_ref, lb_ref, o_ref):
    y = y_ref[0, 0] + y_ref[1, 0]
    hs = _pair_sum_matrix()
    mean = _head_sums(y, hs) * (1.0 / RW_HEAD)
    yc = y - mean
    var = _head_sums(yc * yc, hs) * (1.0 / RW_HEAD)
    yn = yc * lax.rsqrt(var + RW_GN_EPS)
    o_ref[0] = ((yn * lg_ref[...] + lb_ref[...] + bon_ref[0]) * g_ref[0]).astype(o_ref.dtype)


def _rw_finish(y, bonus, g, ln_g, ln_b):
    _, b, t, w = y.shape
    tm = _pick(t, (256, 128, 64))
    s1 = pl.BlockSpec((1, tm, w), lambda bi, i: (bi, i, 0))
    vec = pl.BlockSpec((1, w), lambda bi, i: (0, 0))
    return pl.pallas_call(
        _rw_finish_kernel, name="rw_finish",
        grid=(b, t // tm),
        in_specs=[pl.BlockSpec((2, 1, tm, w), lambda bi, i: (0, bi, i, 0)), s1, s1, vec, vec],
        out_specs=s1,
        out_shape=jax.ShapeDtypeStruct((b, t, w), BF16),
        compiler_params=_params(("parallel", "arbitrary"), 32 << 20),
    )(y, bonus, g, ln_g.reshape(1, w), ln_b.reshape(1, w))


def _hg_scan_kernel(q_ref, z_ref, i_ref, lb_ref, o_ref, s_ref, *, n_heads):
    d = pl.program_id(0)
    step = pl.program_id(2)
    c = CHUNK
    nb = c // SUB

    @pl.when(step == 0)
    def _():
        s_ref[...] = jnp.zeros_like(s_ref)

    sgn = 1 - 2 * d
    big, small = SUB, SUB // nb
    rr = lax.broadcasted_iota(jnp.int32, (15 * c, c), 0)
    cc = lax.broadcasted_iota(jnp.int32, (15 * c, c), 1)
    blk = rr // c
    r = rr % c
    jj = (blk + 3) % 4
    end_s = jnp.where(d == 0, small - 1, 0)
    end_b = jnp.where(d == 0, big - 1, 0)
    g_b = (r // big) * big
    g_s = (r // small) * small
    src = jnp.where(blk == 0, r,
                    jnp.where(blk <= 4, big * jj + end_b,
                              jnp.where(blk <= 8, g_b + small * jj + end_s,
                                        jnp.where(blk <= 12, g_s + jj,
                                                  jnp.where(blk == 13, g_s + end_s, g_b + end_b)))))
    m15 = jnp.where((src - cc) * sgn >= 0, 1.0, 0.0).astype(BF16)

    row = lax.broadcasted_iota(jnp.int32, (c, c), 0)
    col = lax.broadcasted_iota(jnp.int32, (c, c), 1)
    rb, cbk = row // big, col // big
    rg, cg = row // small, col // small
    lvl_b = (rb - cbk) * sgn > 0
    lvl_s = jnp.logical_and(rb == cbk, (rg - cg) * sgn > 0)
    lvl_1 = jnp.logical_and(rg == cg, (row - col) * sgn >= 0)
    last_row = jnp.where(d == 0, c - 1, 0)
    rsel = lax.broadcasted_iota(jnp.int32, (c, 1), 0) == last_row

    def heads(x):
        return jnp.stack([x[:, h * LANE:(h + 1) * LANE] for h in range(n_heads)], axis=0)

    q = q_ref[0]
    z = z_ref[0]
    val = i_ref[0]
    lg_l = lb_ref[0, 0:1]
    lg_1ml = lb_ref[0, 1:2]
    one_ml = lb_ref[0, 2:3]
    s0 = s_ref[...]

    ez = jnp.exp(-jnp.abs(z))
    log_sig = jnp.minimum(z, 0.0) - jnp.log(1.0 + ez)
    x2 = lg_1ml + log_sig
    mx = jnp.maximum(lg_l, x2)
    log_f = mx + jnp.log(jnp.exp(lg_l - mx) + jnp.exp(x2 - mx))
    k = one_ml * jnp.where(z >= 0.0, ez, 1.0) / (1.0 + ez)

    hi = log_f.astype(BF16)
    rem = log_f - hi.astype(F32)
    mid = rem.astype(BF16)
    lo = (rem - mid.astype(F32)).astype(BF16)
    ee = (jnp.dot(m15, hi, preferred_element_type=F32) + jnp.dot(m15, mid, preferred_element_type=F32)
          + jnp.dot(m15, lo, preferred_element_type=F32))
    cb = ee[:c]
    tot = jnp.sum(jnp.where(rsel, cb, 0.0), axis=0, keepdims=True)

    def q_refd(first):
        return heads(jnp.concatenate(
            [q * jnp.exp(jnp.minimum(cb - ee[(first + j) * c:(first + j + 1) * c], 0.0)) for j in range(4)],
            axis=0))

    s_b = _pdot_nt(q_refd(1), heads(k * jnp.exp(ee[14 * c:15 * c] - cb)))
    s_s = _pdot_nt(q_refd(5), heads(k * jnp.exp(ee[13 * c:14 * c] - cb)))
    s_1 = _pdot_nt(q_refd(9), heads(k))
    att = jnp.zeros((n_heads, c, c), F32)
    for j in range(4):
        rows = slice(j * c, (j + 1) * c)
        att += jnp.where(jnp.logical_and(lvl_b, cbk == j), s_b[:, rows], 0.0)
        att += jnp.where(jnp.logical_and(lvl_s, cg % 4 == j), s_s[:, rows], 0.0)
        att += jnp.where(jnp.logical_and(lvl_1, col % small == j), s_1[:, rows], 0.0)
    v3 = heads(val)
    o = _pdot(att, v3) + _pdot_nt(heads(q * jnp.exp(cb)), s0)
    for h in range(n_heads):
        o_ref[0, 0, :, h * LANE:(h + 1) * LANE] = o[h]
    s_ref[...] = s0 * heads(jnp.exp(tot)) + _pdot_tn(v3, heads(k * jnp.exp(tot - cb)))


def _hg_scan(phg, lbt, n_lat, w):
    b, t, _ = phg.shape
    nc = t // CHUNK
    nlc = n_lat // CHUNK
    nh = w // LANE
    ch = functools.partial(_scan_chunk, n_lat_chunks=nlc, n_chunks=nc)
    col = lambda f: pl.BlockSpec((1, CHUNK, w), f)
    return pl.pallas_call(
        functools.partial(_hg_scan_kernel, n_heads=nh), name="hg_scan",
        grid=(2, b, nc),
        in_specs=[col(lambda d, bi, i: (bi, ch(d, i), 0)),
                  col(lambda d, bi, i: (bi, ch(d, i), 1 + d)),
                  col(lambda d, bi, i: (bi, ch(d, i), 3)),
                  pl.BlockSpec((1, 3, w), lambda d, bi, i: (d, 0, 0))],
        out_specs=pl.BlockSpec((1, 1, CHUNK, w), lambda d, bi, i: (d, bi, ch(d, i), 0)),
        out_shape=jax.ShapeDtypeStruct((2, b, t, w), F32),
        scratch_shapes=[pltpu.VMEM((nh, LANE, LANE), F32)],
        compiler_params=_params(("arbitrary", "arbitrary", "arbitrary"), 32 << 20),
    )(phg, phg, phg, lbt)


def _hg_finish_kernel(o_ref, g_ref, ng_ref, out_ref, *, n_heads):
    for h in range(n_heads):
        sl = slice(h * LANE, (h + 1) * LANE)
        o = o_ref[0, 0, :, sl] + o_ref[1, 0, :, sl]
        y = o * lax.rsqrt(jnp.mean(o * o, axis=-1, keepdims=True) + NORM_EPS) * ng_ref[...]
        g = g_ref[0, :, sl]
        out_ref[0, :, sl] = (y * g * _sigmoid(g)).astype(out_ref.dtype)


def _hg_finish(o, phg, norm_g, w):
    _, b, t, _ = o.shape
    tm = _pick(t, (256, 128, 64))
    return pl.pallas_call(
        functools.partial(_hg_finish_kernel, n_heads=w // LANE), name="hg_finish",
        grid=(b, t // tm),
        in_specs=[pl.BlockSpec((2, 1, tm, w), lambda bi, i: (0, bi, i, 0)),
                  pl.BlockSpec((1, tm, w), lambda bi, i: (bi, i, 4)),
                  pl.BlockSpec((1, LANE), lambda bi, i: (0, 0))],
        out_specs=pl.BlockSpec((1, tm, w), lambda bi, i: (bi, i, 0)),
        out_shape=jax.ShapeDtypeStruct((b, t, w), BF16),
        compiler_params=_params(("parallel", "arbitrary"), 32 << 20),
    )(o, phg, norm_g.reshape(1, LANE))


def _hy_pre_kernel(prev_ref, cur_ref, next_ref, cw_ref, cb_ref, x0_ref, z_ref, *, n_lat_blocks, n_blocks, w):
    i = pl.program_id(1)
    tm = cur_ref.shape[1]
    cur = cur_ref[0]
    first = jnp.logical_or(i == 0, i == n_lat_blocks)
    last = jnp.logical_or(i == n_lat_blocks - 1, i == n_blocks - 1)
    row = lax.broadcasted_iota(jnp.int32, (tm, 1), 0)
    before = jnp.where(row == 0, prev_ref[0, tm - 1:tm] * jnp.where(first, 0.0, 1.0), pltpu.roll(cur, 1, 0))
    after = jnp.where(row == tm - 1, next_ref[0, 0:1] * jnp.where(last, 0.0, 1.0), pltpu.roll(cur, tm - 1, 0))
    u = cw_ref[0:1] * before + cw_ref[1:2] * cur + cw_ref[2:3] * after + cb_ref[...]
    x0_ref[0] = u[:, 0:w]
    z_ref[0] = u[:, w:2 * w] * u[:, 2 * w:3 * w]


def _hy_pre(phy, conv_w, conv_b, n_lat):
    b, t, w3 = phy.shape
    w = w3 // 3
    tm = _pick(math.gcd(n_lat, t - n_lat), (256, 128, 64))
    nb = t // tm
    blk = lambda f: pl.BlockSpec((1, tm, w3), f)
    o = pl.BlockSpec((1, tm, w), lambda bi, i: (bi, i, 0))
    s = jax.ShapeDtypeStruct((b, t, w), F32)
    return pl.pallas_call(
        functools.partial(_hy_pre_kernel, n_lat_blocks=n_lat // tm, n_blocks=nb, w=w), name="hy_pre",
        grid=(b, nb),
        in_specs=[blk(lambda bi, i: (bi, jnp.maximum(i - 1, 0), 0)),
                  blk(lambda bi, i: (bi, i, 0)),
                  blk(lambda bi, i: (bi, jnp.minimum(i + 1, nb - 1), 0)),
                  pl.BlockSpec((3, w3), lambda bi, i: (0, 0)),
                  pl.BlockSpec((1, w3), lambda bi, i: (0, 0))],
        out_specs=[o, o],
        out_shape=[s, s],
        compiler_params=_params(("parallel", "arbitrary"), 40 << 20),
    )(phy, phy, phy, conv_w, conv_b.reshape(1, w3))


def _hdot(a, b):
    return jnp.dot(a, b, precision=lax.Precision.HIGHEST, preferred_element_type=F32)


def _hy_taps_kernel(ft_ref, w1_ref, b1_ref, w2_ref, b2_ref, w3f_ref, w3b_ref, fr_ref, dl_ref,
                    tap_ref, sum_ref):
    i = pl.program_id(1)
    ft = ft_ref[...]
    fr = fr_ref[...]
    h = jnp.sin(fr * (_hdot(ft, w1_ref[...]) + b1_ref[...]))
    h = jnp.sin(fr * (_hdot(h, w2_ref[...]) + b2_ref[...]))
    cf = ft[:, HY_EMB:HY_EMB + 1]
    cb = ft[:, HY_EMB + 1:HY_EMB + 2]
    tt = ft[:, 0:1]
    tap = (cf * _hdot(h, w3f_ref[...]) + cb * _hdot(h, w3b_ref[...])) * jnp.exp(-tt * dl_ref[...])
    tap_ref[...] = tap

    @pl.when(i == 0)
    def _():
        sum_ref[...] = jnp.zeros_like(sum_ref)

    sum_ref[...] += jnp.sum(jnp.abs(tap), axis=0, keepdims=True)


def _hy_taps(feats, w1p, b1, w2, b2, w3, freq, deltas):
    rws = feats.shape[0]
    hid = w2.shape[0]
    w = w3.shape[1] // 2
    tr = _pick(rws, (1024, 512, 256, 128))
    ct = w
    nj = w // ct
    c2 = lambda a: pl.BlockSpec(a.shape, lambda j, i: (0, 0))
    return pl.pallas_call(
        _hy_taps_kernel, name="hy_taps",
        grid=(nj, rws // tr),
        in_specs=[pl.BlockSpec((tr, LANE), lambda j, i: (i, 0)),
                  c2(w1p), pl.BlockSpec((1, hid), lambda j, i: (0, 0)),
                  c2(w2), pl.BlockSpec((1, hid), lambda j, i: (0, 0)),
                  pl.BlockSpec((hid, ct), lambda j, i: (0, j)),
                  pl.BlockSpec((hid, ct), lambda j, i: (0, nj + j)),
                  pl.BlockSpec((1, hid), lambda j, i: (0, 0)),
                  pl.BlockSpec((1, ct), lambda j, i: (0, j))],
        out_specs=[pl.BlockSpec((tr, ct), lambda j, i: (i, j)),
                   pl.BlockSpec((1, ct), lambda j, i: (0, j))],
        out_shape=[jax.ShapeDtypeStruct((rws, w), F32), jax.ShapeDtypeStruct((1, w), F32)],
        compiler_params=_params(("parallel", "arbitrary"), 32 << 20),
    )(feats, w1p, b1.reshape(1, hid), w2, b2.reshape(1, hid), w3, w3, freq.reshape(1, hid),
      deltas.reshape(1, w))


def _dft1_kernel(g_ref, x_ref, o_ref):
    o_ref[...] = lax.dot_general(g_ref[...], x_ref[...].astype(BF16), (((2,), (1,)), ((0,), (0,))),
                                 preferred_element_type=F32)


def _dft1(g, x):
    nb, m2, k = g.shape
    c = x.shape[2]
    bt = _pick(nb, (8, 4, 2))
    ct = _pick(c, (256, 128))
    return pl.pallas_call(
        _dft1_kernel, name="dft1",
        grid=(c // ct, nb // bt),
        in_specs=[pl.BlockSpec((bt, m2, k), lambda j, i: (i, 0, 0)),
                  pl.BlockSpec((bt, k, ct), lambda j, i: (i, 0, j))],
        out_specs=pl.BlockSpec((bt, m2, ct), lambda j, i: (i, 0, j)),
        out_shape=jax.ShapeDtypeStruct((nb, m2, c), F32),
        compiler_params=_params(("parallel", "arbitrary"), 32 << 20),
    )(g, x)


def _spec_kernel(a_ref, f2_ref, sc_ref, o_ref):
    for l in range(a_ref.shape[0]):
        o_ref[l] = jnp.dot(f2_ref[...], a_ref[l].astype(BF16), preferred_element_type=F32) * sc_ref[...]


def _spec(a, f2, scale):
    na, m2, c = a.shape
    kt = _pick(na, (8, 4, 2))
    ct = _pick(c, (256, 128))
    return pl.pallas_call(
        _spec_kernel, name="hy_spec",
        grid=(c // ct, na // kt),
        in_specs=[pl.BlockSpec((kt, m2, ct), lambda j, i: (i, 0, j)),
                  pl.BlockSpec((m2, m2), lambda j, i: (0, 0)),
                  pl.BlockSpec((1, ct), lambda j, i: (0, j))],
        out_specs=pl.BlockSpec((kt, m2, ct), lambda j, i: (i, 0, j)),
        out_shape=jax.ShapeDtypeStruct((na, m2, c), F32),
        compiler_params=_params(("parallel", "arbitrary"), 32 << 20),
    )(a, f2, scale)


def _conv_mid_kernel(a_ref, f2_ref, h_ref, g3_ref, o_ref):
    nbh = a_ref.shape[1] // 2
    for l in range(a_ref.shape[0]):
        x = jnp.dot(f2_ref[...], a_ref[l].astype(BF16), preferred_element_type=F32)
        xr, xi = x[:nbh], x[nbh:]
        hr, hi = h_ref[l, :nbh], h_ref[l, nbh:]
        y = jnp.concatenate([xr * hr - xi * hi, xr * hi + xi * hr], axis=0)
        o_ref[l] = jnp.dot(g3_ref[l], y.astype(BF16), preferred_element_type=F32)


def _conv_mid(a, f2, h, g3):
    na, m2, c = a.shape
    kt = _pick(na, (8, 4, 2))
    ct = _pick(c, (256, 128))
    slab = pl.BlockSpec((kt, m2, ct), lambda j, i: (i, 0, j))
    return pl.pallas_call(
        _conv_mid_kernel, name="conv_mid",
        grid=(c // ct, na // kt),
        in_specs=[slab, pl.BlockSpec((m2, m2), lambda j, i: (0, 0)), slab,
                  pl.BlockSpec((kt, m2, m2), lambda j, i: (i, 0, 0))],
        out_specs=slab,
        out_shape=jax.ShapeDtypeStruct((na, m2, c), F32),
        compiler_params=_params(("parallel", "arbitrary"), 32 << 20),
    )(a, f2, h, g3)


def _conv_out_kernel(b_ref, f4_ref, o_ref, *, scale):
    for l in range(b_ref.shape[0]):
        o_ref[l] = jnp.dot(f4_ref[...], b_ref[l].astype(BF16), preferred_element_type=F32) * scale


def _conv_out(bm, f4, scale):
    nb, m2, c = bm.shape
    mo = f4.shape[0]
    pt = _pick(nb, (8, 4, 2))
    ct = _pick(c, (256, 128))
    return pl.pallas_call(
        functools.partial(_conv_out_kernel, scale=scale), name="conv_out",
        grid=(c // ct, nb // pt),
        in_specs=[pl.BlockSpec((pt, m2, ct), lambda j, i: (i, 0, j)),
                  pl.BlockSpec((mo, m2), lambda j, i: (0, 0))],
        out_specs=pl.BlockSpec((pt, mo, ct), lambda j, i: (i, 0, j)),
        out_shape=jax.ShapeDtypeStruct((nb, mo, c), F32),
        compiler_params=_params(("parallel", "arbitrary"), 32 << 20),
    )(bm, f4)


def _ctx_conv_kernel(z_ref, ext_ref, sc_ref, o_ref):
    n = z_ref.shape[1]

    def body(s, acc):
        return acc + ext_ref[pl.ds(n - s, n), :] * z_ref[0, pl.ds(s, 1), :]

    acc = lax.fori_loop(0, n, body, jnp.zeros(o_ref.shape[1:], F32))
    o_ref[0] = acc * sc_ref[...]


def _ctx_conv(z, ext, scale):
    b, n, w = z.shape
    ct = LANE
    return pl.pallas_call(
        _ctx_conv_kernel, name="ctx_conv",
        grid=(b, w // ct),
        in_specs=[pl.BlockSpec((1, n, ct), lambda bi, j: (bi, 0, j)),
                  pl.BlockSpec((2 * n, ct), lambda bi, j: (0, j)),
                  pl.BlockSpec((1, ct), lambda bi, j: (0, j))],
        out_specs=pl.BlockSpec((1, n, ct), lambda bi, j: (bi, 0, j)),
        out_shape=jax.ShapeDtypeStruct((b, n, w), F32),
        compiler_params=_params(("parallel", "arbitrary"), 16 << 20),
    )(z, ext, scale)


def _hy_post_kernel(y_ref, z_ref, x0_ref, bias_ref, o_ref):
    o_ref[0] = (x0_ref[0] * (y_ref[0] + bias_ref[...] * z_ref[0])).astype(o_ref.dtype)


def _hy_post(y, z, x0, bias):
    b, t, w = y.shape
    tm = _pick(t, (768, 384, 256, 128, 64))
    s = pl.BlockSpec((1, tm, w), lambda bi, i: (bi, i, 0))
    return pl.pallas_call(
        _hy_post_kernel, name="hy_post",
        grid=(b, t // tm),
        in_specs=[s, s, s, pl.BlockSpec((1, w), lambda bi, i: (0, 0))],
        out_specs=s,
        out_shape=jax.ShapeDtypeStruct((b, t, w), BF16),
        compiler_params=_params(("parallel", "arbitrary"), 32 << 20),
    )(y, z, x0, bias.reshape(1, w))


def _dft_factors(n):
    m = 2 * n
    na = 1 << ((m.bit_length() - 1) // 2)
    return na, m // na


def _cis(num, den, sign):
    ang = (num % den).astype(F32) * (2.0 * math.pi / den)
    return jnp.cos(ang), sign * jnp.sin(ang)


def _cblock(cr, ci):
    return jnp.concatenate([jnp.concatenate([cr, -ci], axis=-1),
                            jnp.concatenate([ci, cr], axis=-1)], axis=-2)


def _dft_tables(n):
    na, nb = _dft_factors(n)
    m = na * nb
    ah = na // 2
    ar = lambda k: jnp.arange(k, dtype=jnp.int32)
    ka, bb = ar(na)[None, :, None], ar(nb)[:, None, None]
    g1c = _cblock(*_cis(ka * (nb * ar(ah)[None, None, :] + bb), m, -1.0))
    g1t = jnp.concatenate(_cis(ka * (nb * ar(na)[None, None, :] + bb), m, -1.0), axis=-2)
    f2 = _cblock(*_cis(ar(nb)[:, None] * ar(nb)[None, :], nb, -1.0))
    g3 = _cblock(*_cis(ar(nb)[None, :, None] * (ar(na)[:, None, None] + na * ar(nb)[None, None, :]), m, 1.0))
    f4 = _cblock(*_cis(ar(ah)[:, None] * ar(na)[None, :], na, 1.0))
    return tuple(t.astype(BF16) for t in (g1c, g1t, f2, g3, f4))


def _filter_feats(n, order):
    bands_n = (HY_EMB - 1) // 2
    t = jnp.linspace(0.0, 1.0, n, dtype=F32)[:, None]
    lag = jnp.arange(n, dtype=F32)[:, None]
    bands = jnp.linspace(1e-4, bands_n - 1, bands_n, dtype=F32)[None, :]
    ang = 2.0 * math.pi * lag * bands / n
    zf = jnp.concatenate([t, jnp.cos(ang), -jnp.sin(ang)], axis=-1)
    if order == "dft":
        na, nb = _dft_factors(n)
        m = (np.arange(na)[None, :] * nb + np.arange(nb)[:, None]).reshape(-1)
    else:
        m = (np.arange(2 * n) - n) % (2 * n)
    lag_of = np.where(m < n, m, np.where(m == n, 0, 2 * n - m))
    cf = np.where(m == 0, 0.5, np.where(m < n, 1.0, 0.0)).astype(np.float32)
    cb = np.where(m == 0, 0.5, np.where(m > n, 1.0, 0.0)).astype(np.float32)
    pad = jnp.zeros((2 * n, LANE - HY_EMB - 2), F32)
    return jnp.concatenate([zf[lag_of], jnp.asarray(cf)[:, None], jnp.asarray(cb)[:, None], pad], axis=-1)


def _hy_deltas(w):
    return jnp.abs(jnp.linspace(math.log(HY_DECAY_TARGET) / HY_SLOW_PCT,
                                math.log(HY_DECAY_TARGET) / HY_FAST_PCT, w, dtype=F32))


def _long_conv_latent(z, taps_perm, inv_l1):
    b, n, w = z.shape
    na, nb = _dft_factors(n)
    ah = na // 2
    g1c, g1t, f2, g3, f4 = _dft_tables(n)
    h = _spec(_dft1(g1t, taps_perm.reshape(nb, na, w)).reshape(nb, 2, na, w).transpose(2, 1, 0, 3)
              .reshape(na, 2 * nb, w), f2, inv_l1)
    outs = []
    for pair in range(b // 2):
        zz = z[2 * pair:2 * pair + 2].astype(BF16).reshape(2, ah, nb, w).transpose(2, 0, 1, 3).reshape(nb, na, w)
        a = _dft1(g1c, zz).reshape(nb, 2, na, w).transpose(2, 1, 0, 3).reshape(na, 2 * nb, w)
        bm = _conv_mid(a, f2, h, g3).reshape(na, 2, nb, w).transpose(2, 1, 0, 3).reshape(nb, 2 * na, w)
        y = _conv_out(bm, f4, 1.0 / (na * nb))
        outs.append(y.reshape(nb, 2, ah, w).transpose(1, 2, 0, 3).reshape(2, n, w))
    return jnp.concatenate(outs, axis=0) if len(outs) > 1 else outs[0]


def _pack_layer(w_in, rw_mu, dims):
    d_model = w_in.shape[0]
    w, dl, al, gl = dims
    ld, la, lg = _rup(dl, LANE), _rup(al, LANE), _rup(gl, LANE)
    rw_cols = 3 * w + 2 * dl + 2 * al + gl
    unit = 512 if d_model >= 2048 else LANE
    na = _rup(3 * w + 2 * ld + 2 * la + lg, unit)
    pieces = [(0, 3 * w, 0), (3 * w, dl, 3 * w), (3 * w + dl, dl, 3 * w + ld),
              (3 * w + 2 * dl, al, 3 * w + 2 * ld), (3 * w + 2 * dl + al, al, 3 * w + 2 * ld + la),
              (3 * w + 2 * dl + 2 * al, gl, 3 * w + 2 * ld + 2 * la)]
    cols, mus = [], []
    orig = np.full((na,), rw_cols - 1, np.int64)
    pos = 0
    for src, width, dst in pieces + [(rw_cols, 0, na)]:
        if dst > pos:
            cols.append(jnp.zeros((d_model, dst - pos), w_in.dtype))
            mus.append(jnp.zeros((dst - pos,), F32))
        cols.append(w_in[:, src:src + width])
        mus.append(rw_mu[src:src + width])
        orig[dst:dst + width] = np.arange(src, src + width)
        pos = dst + width
    offs = []
    src = rw_cols
    for width in (5 * w, 3 * w, w_in.shape[1] - rw_cols - 8 * w):
        offs.append(pos)
        cols.append(w_in[:, src:src + width])
        pad = _rup(width, unit) - width
        if pad:
            cols.append(jnp.zeros((d_model, pad), w_in.dtype))
        src += width
        pos += width + pad
    wp = jnp.concatenate(cols, axis=1).astype(BF16)
    mu = jnp.concatenate(mus).reshape(1, na)
    grp = np.stack([orig // (rw_cols // 4), orig // (rw_cols // 2)]).astype(np.int32)
    grp = np.minimum(grp, np.array([[3], [1]])).astype(np.int32)
    return wp, offs, mu, jnp.asarray(grp), (w, ld, la, lg), na


def _pad_rows(a, rows):
    return jnp.pad(a, [(0, 0)] * (a.ndim - 2) + [(0, rows - a.shape[-2]), (0, 0)])


def kernel(x, c, ctx, c_ctx, ada_w, ada_b, norm1_g, norm2_g, w_in, rw_mu, rw_w0, rw_w_up, rw_a0, rw_a_up, rw_g_up, rw_k_k, rw_k_a, rw_r_k, rw_ln_g, rw_ln_b, hg_lower_bounds, hg_norm_g, hy_conv_w, hy_conv_b, hy_f_w1, hy_f_b1, hy_f_w2, hy_f_b2, hy_f_w3, hy_freq, hy_bias, w_branch_a, w_branch_b, w_branch_c, w_out, ffn_w_gate, ffn_w_up, ffn_w_down, final_norm_g):
    bsz, n_lat, d = x.shape
    n_ctx = ctx.shape[1]
    depth = w_in.shape[0]
    w = rw_k_k.shape[1]
    dims = (w, rw_w_up.shape[2], rw_a_up.shape[2], rw_g_up.shape[1])
    assert bsz % 2 == 0 and n_lat % GRID_W == 0 and n_ctx % CHUNK == 0 and w % LANE == 0
    rw_cols = 3 * w + 2 * dims[1] + 2 * dims[2] + dims[3]
    hg_end = rw_cols + 5 * w
    hy_end = hg_end + 3 * w

    xs = jnp.concatenate([x, ctx], axis=1)
    cvec = jnp.zeros((_rup(bsz + 1, 8), d), F32).at[:bsz].set(c).at[bsz].set(c_ctx)
    lb_cum = jnp.cumsum(jax.nn.softmax(hg_lower_bounds.astype(F32), axis=0), axis=0)
    deltas = _hy_deltas(w)

    for layer in range(depth):
        need_ctx = layer < depth - 1
        mod = _ada(cvec, ada_w[layer], ada_b[layer])[:bsz + 1].reshape(bsz + 1, 1, 6, d)
        mods = [mod[:, :, s, :] for s in range(6)]

        wp, offs, mu_p, grp, pdims, na = _pack_layer(w_in[layer], rw_mu[layer], dims)
        _, ld, la, lg = pdims
        h1 = _modnorm(xs, norm1_g[layer], mods[0], mods[1], n_lat)
        pa = _matmul(h1, wp, F32, 0, na)
        phg = _matmul(h1, wp, F32, offs[0], 5 * w)
        phy = _matmul(h1, wp, F32, offs[1], 3 * w)
        pgate = _matmul(h1, wp, BF16, offs[2], 3 * d)

        r, v, kn, g, bonus, lw, kd, bb = _rw_prep(
            pa, grp, mu_p, rw_w0[layer].reshape(2, 1, w), _pad_rows(rw_w_up[layer], ld),
            rw_a0[layer].reshape(2, 1, w), _pad_rows(rw_a_up[layer], la), _pad_rows(rw_g_up[layer], lg),
            rw_k_k[layer].reshape(1, w), rw_k_a[layer].reshape(1, w), rw_r_k[layer].reshape(1, w),
            n_lat, pdims)
        y_rw = _rw_scan(r, v, kn, lw, kd, bb, n_lat)
        oa = _rw_finish(y_rw, bonus, g, rw_ln_g[layer], rw_ln_b[layer])

        lb = lb_cum[layer] - lb_cum[0]
        lbt = jnp.stack([jnp.log(lb), jnp.log1p(-lb), 1.0 - lb], axis=1)
        o_hg = _hg_scan(phg, lbt, n_lat, w)
        ob = _hg_finish(o_hg, phg, hg_norm_g[layer], w)

        x0, z = _hy_pre(phy, hy_conv_w[layer], hy_conv_b[layer], n_lat)
        w1p = _pad_rows(hy_f_w1[layer], LANE)
        filt = (w1p, hy_f_b1[layer], hy_f_w2[layer], hy_f_b2[layer], hy_f_w3[layer], hy_freq[layer], deltas)
        taps, l1 = _hy_taps(_filter_feats(n_lat, "dft"), *filt)
        y_lat = _long_conv_latent(z[:, :n_lat], taps, 1.0 / l1)
        if need_ctx:
            ext, l1c = _hy_taps(_filter_feats(n_ctx, "lag"), *filt)
            y_ctx = _ctx_conv(z[:, n_lat:], ext, 1.0 / l1c)
        else:
            y_ctx = jnp.zeros((bsz, n_ctx, w), F32)
        oc = _hy_post(jnp.concatenate([y_lat, y_ctx], axis=1), z, x0, hy_bias[layer])

        ym = _merge(oa, ob, oc, w_branch_a[layer].astype(BF16), w_branch_b[layer].astype(BF16),
                    w_branch_c[layer].astype(BF16), pgate)
        xs = _matmul_resid(ym, w_out[layer].astype(BF16), xs, mods[2], n_lat)

        hid = ffn_w_gate.shape[2]
        hp = _rup(hid, 1024) if hid > 1024 else _rup(hid, LANE)
        padc = lambda a: jnp.pad(a.astype(BF16), ((0, 0), (0, hp - hid)))
        h2 = _modnorm(xs, norm2_g[layer], mods[3], mods[4], n_lat)
        act = _swiglu_up(h2, padc(ffn_w_gate[layer]), padc(ffn_w_up[layer]))
        wd = jnp.pad(ffn_w_down[layer].astype(BF16), ((0, hp - hid), (0, 0)))
        xs = _matmul_resid(act, wd, xs, mods[5], n_lat)

    return _final_norm(xs, final_norm_g, n_lat)
```

```python
import functools
import math

import numpy as np
import jax
import jax.numpy as jnp
from jax import lax
from jax.experimental import pallas as pl
from jax.experimental.pallas import tpu as pltpu

F32 = jnp.float32
BF16 = jnp.bfloat16

GRID_W = 64
CHUNK = 64
SUB = 16
NORM_EPS = 1e-6
RW_HEAD = 64
RW_GN_EPS = 64e-5
HG_EXPAND = 128
HY_EMB = 33
HY_DECAY_TARGET = 1e-2
HY_FAST_PCT = 0.3
HY_SLOW_PCT = 1.5
LANE = 128
VMEM_CAP = 56 * 1024 * 1024
VMEM_SLACK = 8 * 1024 * 1024


def _params(sem, vmem_bytes):
    return pltpu.CompilerParams(dimension_semantics=sem,
                                vmem_limit_bytes=int(min(max(vmem_bytes + VMEM_SLACK, 16 << 20), VMEM_CAP)))


def _pick(n, cands):
    for c in cands:
        if n % c == 0:
            return c
    return n


def _rup(n, m):
    return -(-n // m) * m


def _bdot(a, b):
    return jnp.dot(a.astype(BF16), b.astype(BF16), preferred_element_type=F32)


def _bdot_nt(a, b):
    return lax.dot_general(a.astype(BF16), b.astype(BF16), (((1,), (1,)), ((), ())),
                           preferred_element_type=F32)


def _bdot_tn(a, b):
    return lax.dot_general(a.astype(BF16), b.astype(BF16), (((0,), (0,)), ((), ())),
                           preferred_element_type=F32)


def _split(x):
    hi = x.astype(BF16)
    lo = (x - hi.astype(F32)).astype(BF16)
    return hi, lo


def _mask_dot(m, x):
    hi, lo = _split(x)
    return (jnp.dot(m, hi, preferred_element_type=F32) + jnp.dot(m, lo, preferred_element_type=F32))


def _x_mask_dot(x, m):
    hi, lo = _split(x)
    return (jnp.dot(hi, m, preferred_element_type=F32) + jnp.dot(lo, m, preferred_element_type=F32))


def _sigmoid(x):
    return 1.0 / (1.0 + jnp.exp(-x))


def _order_masks(d, n, reps=1):
    row = lax.broadcasted_iota(jnp.int32, (n, reps * n), 0)
    col = lax.broadcasted_iota(jnp.int32, (n, reps * n), 1) % n
    diff = (row - col) * (1 - 2 * d)
    return diff >= 0, diff > 0


def _ada_kernel(c_ref, w_ref, b_ref, o_ref):
    c = c_ref[...]
    o_ref[...] = _bdot(c * _sigmoid(c), w_ref[...]) + b_ref[...]


def _ada(cvec, w, b, layer):
    rows, d = cvec.shape
    n = w.shape[2]
    tn = _pick(n, (512, 256, 128))
    return pl.pallas_call(
        _ada_kernel, name="ada",
        grid=(n // tn,),
        in_specs=[pl.BlockSpec((rows, d), lambda j: (0, 0)),
                  pl.BlockSpec((None, d, tn), lambda j: (layer, 0, j)),
                  pl.BlockSpec((None, 1, tn), lambda j: (layer, 0, j))],
        out_specs=pl.BlockSpec((rows, tn), lambda j: (0, j)),
        out_shape=jax.ShapeDtypeStruct((rows, n), F32),
        compiler_params=_params(("arbitrary",), 3 * d * tn * 4),
    )(cvec, w, b.reshape(b.shape[0], 1, n))


def _modnorm_kernel(x_ref, g_ref, shl_ref, scl_ref, shc_ref, scc_ref, o_ref, *, n_lat):
    tm = x_ref.shape[1]
    x = x_ref[0]
    y = x * lax.rsqrt(jnp.mean(x * x, axis=-1, keepdims=True) + NORM_EPS) * g_ref[...]
    pos = pl.program_id(1) * tm + lax.broadcasted_iota(jnp.int32, (tm, 1), 0)
    is_ctx = pos >= n_lat
    sc = jnp.where(is_ctx, scc_ref[0], scl_ref[0])
    sh = jnp.where(is_ctx, shc_ref[0], shl_ref[0])
    o_ref[0] = (y * (1.0 + sc) + sh).astype(o_ref.dtype)


def _modnorm(x, gain, shift, scale, n_lat):
    b, t, d = x.shape
    tm = _pick(t, (256, 128, 64))
    vec = lambda f: pl.BlockSpec((1, 1, d), f)
    return pl.pallas_call(
        functools.partial(_modnorm_kernel, n_lat=n_lat), name="modnorm",
        grid=(b, t // tm),
        in_specs=[pl.BlockSpec((1, tm, d), lambda bi, i: (bi, i, 0)),
                  pl.BlockSpec((1, d), lambda bi, i: (0, 0)),
                  vec(lambda bi, i: (bi, 0, 0)), vec(lambda bi, i: (bi, 0, 0)),
                  vec(lambda bi, i: (b, 0, 0)), vec(lambda bi, i: (b, 0, 0))],
        out_specs=pl.BlockSpec((1, tm, d), lambda bi, i: (bi, i, 0)),
        out_shape=jax.ShapeDtypeStruct((b, t, d), BF16),
        compiler_params=_params(("parallel", "arbitrary"), 6 * tm * d * 4),
    )(x, gain.reshape(1, d), shift, scale, shift, scale)


def _rmsnorm_kernel(x_ref, g_ref, o_ref):
    x = x_ref[0]
    o_ref[0] = x * lax.rsqrt(jnp.mean(x * x, axis=-1, keepdims=True) + NORM_EPS) * g_ref[...]


def _final_norm(x, gain, n_lat):
    b, _, d = x.shape
    tm = _pick(n_lat, (256, 128, 64))
    return pl.pallas_call(
        _rmsnorm_kernel, name="final_norm",
        grid=(b, n_lat // tm),
        in_specs=[pl.BlockSpec((1, tm, d), lambda bi, i: (bi, i, 0)),
                  pl.BlockSpec((1, d), lambda bi, i: (0, 0))],
        out_specs=pl.BlockSpec((1, tm, d), lambda bi, i: (bi, i, 0)),
        out_shape=jax.ShapeDtypeStruct((b, n_lat, d), F32),
        compiler_params=_params(("parallel", "arbitrary"), 6 * tm * d * 4),
    )(x, gain.reshape(1, d))


def _mm_kernel(x_ref, w_ref, o_ref):
    o_ref[0] = jnp.dot(x_ref[0], w_ref[...], preferred_element_type=F32).astype(o_ref.dtype)


def _tn_wide(n):
    return _pick(n, (1024, 512, 256, 128))


def _tn_ffn(n):
    return _pick(n, (512, 256, 128))


def _tk_of(k):
    return k if k <= 4096 else _pick(k, (2816, 2048, 1024, 512, 256, 128))


def _tile_w(w, tn, tk=None):
    l, k, n = w.shape
    tk = k if tk is None else tk
    return w.astype(BF16).reshape(l, k // tk, tk, n // tn, tn).transpose(0, 3, 1, 2, 4)


def _wspec(tk, tn, f):
    return pl.BlockSpec((None, None, None, tk, tn), f)


def _matmul(x, wt, layer, j0, nj, out_dtype):
    b, t, k = x.shape
    tn = wt.shape[-1]
    n = nj * tn
    tm = _pick(t, (768, 384, 256, 128, 64))
    osz = jnp.dtype(out_dtype).itemsize
    return pl.pallas_call(
        _mm_kernel, name="in_proj",
        grid=(b, t // tm, n // tn),
        in_specs=[pl.BlockSpec((1, tm, k), lambda bi, i, j: (bi, i, 0)),
                  _wspec(k, tn, lambda bi, i, j: (layer, j0 + j, 0, 0, 0))],
        out_specs=pl.BlockSpec((1, tm, tn), lambda bi, i, j: (bi, i, j)),
        out_shape=jax.ShapeDtypeStruct((b, t, n), out_dtype),
        compiler_params=_params(("parallel", "parallel", "arbitrary"),
                                2 * (tm * k * 2 + k * tn * 2 + tm * tn * osz) + tm * tn * 4),
    )(x, wt)


def _swiglu_kernel(x_ref, wg_ref, wu_ref, o_ref):
    x = x_ref[0]
    g = jnp.dot(x, wg_ref[...], preferred_element_type=F32)
    u = jnp.dot(x, wu_ref[...], preferred_element_type=F32)
    o_ref[0] = (g * _sigmoid(g) * u).astype(o_ref.dtype)


def _swiglu_up(x, wg, wu, layer):
    b, t, k = x.shape
    tn = wg.shape[-1]
    n = wg.shape[1] * tn
    tm = _pick(t, (768, 384, 256, 128, 64))
    wsp = lambda: _wspec(k, tn, lambda bi, i, j: (layer, j, 0, 0, 0))
    return pl.pallas_call(
        _swiglu_kernel, name="swiglu_up",
        grid=(b, t // tm, n // tn),
        in_specs=[pl.BlockSpec((1, tm, k), lambda bi, i, j: (bi, i, 0)), wsp(), wsp()],
        out_specs=pl.BlockSpec((1, tm, tn), lambda bi, i, j: (bi, i, j)),
        out_shape=jax.ShapeDtypeStruct((b, t, n), BF16),
        compiler_params=_params(("parallel", "parallel", "arbitrary"),
                                2 * (tm * k * 2 + 2 * k * tn * 2 + tm * tn * 2) + 3 * tm * tn * 4),
    )(x, wg, wu)


def _resid_kernel(x_ref, w_ref, r_ref, gl_ref, gc_ref, o_ref, acc_ref, *, n_lat):
    kk = pl.program_id(3)

    @pl.when(kk == 0)
    def _():
        acc_ref[...] = jnp.zeros_like(acc_ref)

    acc_ref[...] += jnp.dot(x_ref[0], w_ref[...], preferred_element_type=F32)

    @pl.when(kk == pl.num_programs(3) - 1)
    def _():
        tm = acc_ref.shape[0]
        pos = pl.program_id(1) * tm + lax.broadcasted_iota(jnp.int32, (tm, 1), 0)
        gate = jnp.where(pos >= n_lat, gc_ref[0], gl_ref[0])
        o_ref[0] = r_ref[0] + gate * acc_ref[...]


def _matmul_resid(x, w, layer, res, gate, n_lat):
    b, t, k = x.shape
    tk, tn = w.shape[-2:]
    n = w.shape[1] * tn
    tm = _pick(t, (768, 384, 256, 128, 64))
    return pl.pallas_call(
        functools.partial(_resid_kernel, n_lat=n_lat), name="proj_resid",
        grid=(b, t // tm, n // tn, k // tk),
        in_specs=[pl.BlockSpec((1, tm, tk), lambda bi, i, j, l: (bi, i, l)),
                  _wspec(tk, tn, lambda bi, i, j, l: (layer, j, l, 0, 0)),
                  pl.BlockSpec((1, tm, tn), lambda bi, i, j, l: (bi, i, j)),
                  pl.BlockSpec((1, 1, tn), lambda bi, i, j, l: (bi, 0, j)),
                  pl.BlockSpec((1, 1, tn), lambda bi, i, j, l: (b, 0, j))],
        out_specs=pl.BlockSpec((1, tm, tn), lambda bi, i, j, l: (bi, i, j)),
        out_shape=jax.ShapeDtypeStruct((b, t, n), F32),
        scratch_shapes=[pltpu.VMEM((tm, tn), F32)],
        compiler_params=_params(("parallel", "parallel", "arbitrary", "arbitrary"),
                                2 * (tm * tk * 2 + tk * tn * 2 + 2 * tm * tn * 4) + 2 * tm * tn * 4),
    )(x, w, res, gate, gate)


def _merge_kernel(oa_ref, ob_ref, oc_ref, wa_ref, wb_ref, wc_ref, ga_ref, gb_ref, gc_ref, o_ref):
    y = _sigmoid(ga_ref[0].astype(F32)) * jnp.dot(oa_ref[0], wa_ref[...], preferred_element_type=F32)
    y += _sigmoid(gb_ref[0].astype(F32)) * jnp.dot(ob_ref[0], wb_ref[...], preferred_element_type=F32)
    y += _sigmoid(gc_ref[0].astype(F32)) * jnp.dot(oc_ref[0], wc_ref[...], preferred_element_type=F32)
    o_ref[0] = y.astype(o_ref.dtype)


def _merge(oa, ob, oc, wa, wb, wc, layer, pgate):
    b, t, kw = oa.shape
    tn = wa.shape[-1]
    nj = wa.shape[1]
    d = nj * tn
    tm = _pick(t, (768, 384, 256, 128, 64))
    br = lambda: pl.BlockSpec((1, tm, kw), lambda bi, i, j: (bi, i, 0))
    wt = lambda: _wspec(kw, tn, lambda bi, i, j: (layer, j, 0, 0, 0))
    gt = lambda s: pl.BlockSpec((1, tm, tn), lambda bi, i, j: (bi, i, s * nj + j))
    gsz = jnp.dtype(pgate.dtype).itemsize
    return pl.pallas_call(
        _merge_kernel, name="merge",
        grid=(b, t // tm, nj),
        in_specs=[br(), br(), br(), wt(), wt(), wt(), gt(0), gt(1), gt(2)],
        out_specs=pl.BlockSpec((1, tm, tn), lambda bi, i, j: (bi, i, j)),
        out_shape=jax.ShapeDtypeStruct((b, t, d), BF16),
        compiler_params=_params(("parallel", "parallel", "arbitrary"),
                                2 * (3 * tm * kw * 2 + 3 * kw * tn * 2 + 3 * tm * tn * gsz + tm * tn * 2)
                                + 4 * tm * tn * 4),
    )(oa, ob, oc, wa, wb, wc, pgate, pgate, pgate)


def _pair_sum_matrix():
    r = lax.broadcasted_iota(jnp.int32, (LANE, LANE), 0) // RW_HEAD
    c = lax.broadcasted_iota(jnp.int32, (LANE, LANE), 1) // RW_HEAD
    return jnp.where(r == c, 1.0, 0.0).astype(BF16)


def _head_sums(x, hs):
    rows, w = x.shape
    nt = w // LANE
    stacked = jnp.concatenate([x[:, j * LANE:(j + 1) * LANE] for j in range(nt)], axis=0)
    s = _x_mask_dot(stacked, hs)
    return jnp.concatenate([s[j * rows:(j + 1) * rows] for j in range(nt)], axis=1)


def _rw_prep_kernel(prev_ref, cur_ref, next_ref, grp_ref, mu_ref, w0_ref, wup_ref, a0_ref, aup_ref,
                    gup_ref, kk_ref, ka_ref, rk_ref,
                    r_ref, v_ref, kn_ref, g_ref, bon_ref, lw_ref, kd_ref, bb_ref,
                    *, n_lat_chunks, n_chunks, w, ld, la, lg):
    i = pl.program_id(1)
    c = CHUNK
    cur = cur_ref[0].astype(F32)
    prev = prev_ref[0].astype(F32)
    nxt = next_ref[0].astype(F32)
    is_ctx = i >= n_lat_chunks
    first = jnp.logical_or(i == 0, i == n_lat_chunks)
    last = jnp.logical_or(i == n_lat_chunks - 1, i == n_chunks - 1)
    row = lax.broadcasted_iota(jnp.int32, (c, 1), 0)
    carry_in = jnp.where(jnp.logical_and(is_ctx, jnp.logical_not(first)), 1.0, 0.0)
    carry_out = jnp.where(jnp.logical_and(is_ctx, jnp.logical_not(last)), 1.0, 0.0)
    tm1 = jnp.where(row == 0, prev[c - 1:c] * carry_in, pltpu.roll(cur, 1, 0))
    tp1 = jnp.where(row == c - 1, nxt[0:1] * carry_out, pltpu.roll(cur, c - 1, 0))
    lat_up = jnp.where(jnp.logical_or(is_ctx, first), 0.0, 1.0)
    lat_dn = jnp.where(jnp.logical_or(is_ctx, last), 0.0, 1.0)
    code = jnp.where(is_ctx, grp_ref[1:2], grp_ref[0:1])
    shifted = jnp.where(code == 0, tm1,
                        jnp.where(code == 1, tp1,
                                  jnp.where(code == 2, prev * lat_up, nxt * lat_dn)))
    m = cur + mu_ref[...] * (shifted - cur)

    r = m[:, 0:w]
    k = m[:, w:2 * w]
    v = m[:, 2 * w:3 * w]
    o = 3 * w
    wd = (m[:, o:o + ld], m[:, o + ld:o + 2 * ld])
    ad = (m[:, o + 2 * ld:o + 2 * ld + la], m[:, o + 2 * ld + la:o + 2 * ld + 2 * la])
    gd = m[:, o + 2 * ld + 2 * la:o + 2 * ld + 2 * la + lg]

    hs = _pair_sum_matrix()
    g_ref[0] = _bdot(_sigmoid(gd), gup_ref[...]).astype(g_ref.dtype)
    kx = k * kk_ref[...]
    kn = kx * lax.rsqrt(jnp.maximum(_head_sums(kx * kx, hs), 1e-24))
    r_ref[0] = r.astype(r_ref.dtype)
    v_ref[0] = v.astype(v_ref.dtype)
    kn_ref[0] = kn.astype(kn_ref.dtype)
    bonus = jnp.zeros_like(r)
    for d in range(2):
        wl = w0_ref[d] + _bdot(jnp.tanh(wd[d]), wup_ref[d])
        sp = jnp.maximum(-wl, 0.0) + jnp.log(1.0 + jnp.exp(-jnp.abs(wl)))
        lw_ref[d, 0] = -jnp.exp(-sp - 0.5)
        a = _sigmoid(a0_ref[d] + _bdot(ad[d], aup_ref[d]))
        kd = k * (1.0 + (a - 1.0) * ka_ref[...])
        kd_ref[d, 0] = kd.astype(kd_ref.dtype)
        bb_ref[d, 0] = (kn * a).astype(bb_ref.dtype)
        bonus += r * kd * rk_ref[...]
    bon_ref[0] = (_head_sums(bonus, hs) * v).astype(bon_ref.dtype)


def _rw_prep(pa, grp, mu, w0, wup, a0, aup, gup, k_k, k_a, r_k, n_lat, dims):
    b, t, na = pa.shape
    w, ld, la, lg = dims
    nc = t // CHUNK
    nlc = n_lat // CHUNK
    blk = lambda f: pl.BlockSpec((1, CHUNK, na), f)
    full = lambda a: pl.BlockSpec(a.shape, lambda bi, i: (0,) * a.ndim)
    o1 = pl.BlockSpec((1, CHUNK, w), lambda bi, i: (bi, i, 0))
    o2 = pl.BlockSpec((2, 1, CHUNK, w), lambda bi, i: (0, bi, i, 0))
    s1 = jax.ShapeDtypeStruct((b, t, w), BF16)
    s2 = jax.ShapeDtypeStruct((2, b, t, w), BF16)
    s2f = jax.ShapeDtypeStruct((2, b, t, w), F32)
    consts = (grp, mu, w0, wup, a0, aup, gup, k_k, k_a, r_k)
    return pl.pallas_call(
        functools.partial(_rw_prep_kernel, n_lat_chunks=nlc, n_chunks=nc, w=w, ld=ld, la=la, lg=lg),
        name="rw_prep",
        grid=(b, nc),
        in_specs=[blk(lambda bi, i: (bi, jnp.maximum(i - 1, 0), 0)),
                  blk(lambda bi, i: (bi, i, 0)),
                  blk(lambda bi, i: (bi, jnp.minimum(i + 1, nc - 1), 0))] + [full(a) for a in consts],
        out_specs=[o1, o1, o1, o1, o1, o2, o2, o2],
        out_shape=[s1, s1, s1, s1, s1, s2f, s2, s2],
        compiler_params=_params(("parallel", "arbitrary"), 32 << 20),
    )(pa, pa, pa, *consts)


def _pdot(a, b):
    return lax.dot_general(a.astype(BF16), b.astype(BF16), (((2,), (1,)), ((0,), (0,))),
                           preferred_element_type=F32)


def _pdot_nt(a, b):
    return lax.dot_general(a.astype(BF16), b.astype(BF16), (((2,), (2,)), ((0,), (0,))),
                           preferred_element_type=F32)


def _pdot_tn(a, b):
    return lax.dot_general(a.astype(BF16), b.astype(BF16), (((1,), (1,)), ((0,), (0,))),
                           preferred_element_type=F32)


def _rw_scan_kernel(r_ref, v_ref, kn_ref, lw_ref, kd_ref, bb_ref, y_ref, s_ref, *, n_pairs):
    d = pl.program_id(0)
    i = pl.program_id(2)
    c = CHUNK
    wfull = n_pairs * LANE

    @pl.when(i == 0)
    def _():
        s_ref[...] = jnp.zeros_like(s_ref)

    incl_bf = jnp.where(_order_masks(d, c)[0], 1.0, 0.0).astype(BF16)
    incl2, strict2 = _order_masks(d, c, 2)
    lane_w = lax.broadcasted_iota(jnp.int32, (1, wfull), 1) % LANE
    w0 = jnp.where(lane_w < RW_HEAD, 1.0, 0.0)
    w1 = 1.0 - w0
    lane = lax.broadcasted_iota(jnp.int32, (1, 1, LANE), 2)
    m0 = jnp.where(lane < RW_HEAD, 1.0, 0.0)
    m1 = 1.0 - m0
    rowh = lax.broadcasted_iota(jnp.int32, (LANE, LANE), 0) // RW_HEAD
    colh = lax.broadcasted_iota(jnp.int32, (LANE, LANE), 1) // RW_HEAD
    blockdiag = rowh == colh
    last_row = jnp.where(d == 0, c - 1, 0)
    rsel = lax.broadcasted_iota(jnp.int32, (c, 1), 0) == last_row

    def pairs(x):
        return jnp.stack([x[:, p * LANE:(p + 1) * LANE] for p in range(n_pairs)], axis=0)

    def stack(x):
        return jnp.concatenate([x * m0, x * m1], axis=1)

    lw = lw_ref[0, 0]
    r = r_ref[0].astype(F32)
    v = v_ref[0].astype(F32)
    kd = kd_ref[0, 0].astype(F32)
    bb = bb_ref[0, 0].astype(F32)
    cw = _mask_dot(incl_bf, lw)
    tot = jnp.sum(jnp.where(rsel, cw, 0.0), axis=0, keepdims=True)
    w_inv = jnp.exp(-cw)
    w_rem = jnp.exp(tot - cw)
    a_t = -kn_ref[0].astype(F32) * jnp.exp(cw - lw)
    r_t = r * jnp.exp(cw)
    b_t = bb * w_inv
    k_t = kd * w_inv
    ar = pairs(jnp.concatenate([a_t, r_t], axis=0))
    bk = pairs(jnp.concatenate([b_t * w0, b_t * w1, k_t * w0, k_t * w1], axis=0))
    vst = pairs(jnp.concatenate([v * w0, v * w1], axis=0))
    uvr = pairs(jnp.concatenate([bb * w_rem, kd * w_rem], axis=0))
    v3 = pairs(v)
    decay = pairs(jnp.exp(tot))
    s0 = s_ref[...]

    sc = _pdot_nt(ar, bk)
    a_ab = jnp.where(strict2, sc[:, :c, :2 * c], 0.0)
    a_ak = jnp.where(strict2, sc[:, :c, 2 * c:], 0.0)
    r_b = jnp.where(incl2, sc[:, c:, :2 * c], 0.0)
    r_k = jnp.where(incl2, sc[:, c:, 2 * c:], 0.0)

    abd = stack(a_ab)
    q = _pdot(abd, abd)
    nsum = abd
    for _ in range(5):
        nsum = nsum + q + _pdot(q, nsum)
        q = _pdot(q, q)
    n_side = nsum[:, :c] + nsum[:, c:]

    art = _pdot_nt(ar, s0)
    rhs = art[:, :c] + _pdot(a_ak, vst)
    u = rhs + _pdot(n_side, stack(rhs))
    y = art[:, c:] + _pdot(jnp.concatenate([r_b, r_k], axis=2), jnp.concatenate([stack(u), vst], axis=1))
    for p in range(n_pairs):
        y_ref[0, 0, :, p * LANE:(p + 1) * LANE] = y[p]
    upd = _pdot_tn(jnp.concatenate([u, v3], axis=1), uvr)
    s_ref[...] = s0 * decay + jnp.where(blockdiag, upd, 0.0)


def _scan_chunk(d, i, n_lat_chunks, n_chunks):
    nctx = n_chunks - n_lat_chunks
    in_ctx = i < nctx
    fwd = jnp.where(in_ctx, n_lat_chunks + i, i - nctx)
    bwd = jnp.where(in_ctx, n_chunks - 1 - i, n_chunks - 1 - i)
    return jnp.where(d == 0, fwd, bwd)


def _rw_scan(r, v, kn, lw, kd, bb, n_lat):
    b, t, w = r.shape
    nc = t // CHUNK
    nlc = n_lat // CHUNK
    npair = w // LANE
    ch = functools.partial(_scan_chunk, n_lat_chunks=nlc, n_chunks=nc)
    s1 = pl.BlockSpec((1, CHUNK, w), lambda d, bi, i: (bi, ch(d, i), 0))
    s2 = pl.BlockSpec((1, 1, CHUNK, w), lambda d, bi, i: (d, bi, ch(d, i), 0))
    return pl.pallas_call(
        functools.partial(_rw_scan_kernel, n_pairs=npair), name="rw_scan",
        grid=(2, b, nc),
        in_specs=[s1, s1, s1, s2, s2, s2],
        out_specs=s2,
        out_shape=jax.ShapeDtypeStruct((2, b, t, w), F32),
        scratch_shapes=[pltpu.VMEM((npair, LANE, LANE), F32)],
        compiler_params=_params(("arbitrary", "arbitrary", "arbitrary"), 32 << 20),
    )(r, v, kn, lw, kd, bb)


def _rw_finish_kernel(y_ref, bon_ref, g_ref, lg_ref, lb_ref, o_ref):
    y = y_ref[0, 0] + y_ref[1, 0]
    hs = _pair_sum_matrix()
    mean = _head_sums(y, hs) * (1.0 / RW_HEAD)
    yc = y - mean
    var = _head_sums(yc * yc, hs) * (1.0 / RW_HEAD)
    yn = yc * lax.rsqrt(var + RW_GN_EPS)
    o_ref[0] = ((yn * lg_ref[...] + lb_ref[...] + bon_ref[0].astype(F32)) * g_ref[0].astype(F32)).astype(o_ref.dtype)


def _rw_finish(y, bonus, g, ln_g, ln_b):
    _, b, t, w = y.shape
    tm = _pick(t, (256, 128, 64))
    s1 = pl.BlockSpec((1, tm, w), lambda bi, i: (bi, i, 0))
    vec = pl.BlockSpec((1, w), lambda bi, i: (0, 0))
    return pl.pallas_call(
        _rw_finish_kernel, name="rw_finish",
        grid=(b, t // tm),
        in_specs=[pl.BlockSpec((2, 1, tm, w), lambda bi, i: (0, bi, i, 0)), s1, s1, vec, vec],
        out_specs=s1,
        out_shape=jax.ShapeDtypeStruct((b, t, w), BF16),
        compiler_params=_params(("parallel", "arbitrary"), 32 << 20),
    )(y, bonus, g, ln_g.reshape(1, w), ln_b.reshape(1, w))


def _hg_scan_kernel(q_ref, z_ref, i_ref, lb_ref, o_ref, s_ref, *, n_heads):
    d = pl.program_id(0)
    step = pl.program_id(2)
    c = CHUNK
    nb = c // SUB

    @pl.when(step == 0)
    def _():
        s_ref[...] = jnp.zeros_like(s_ref)

    sgn = 1 - 2 * d
    big, small = SUB, SUB // nb
    rr = lax.broadcasted_iota(jnp.int32, (15 * c, c), 0)
    cc = lax.broadcasted_iota(jnp.int32, (15 * c, c), 1)
    blk = rr // c
    r = rr % c
    jj = (blk + 3) % 4
    end_s = jnp.where(d == 0, small - 1, 0)
    end_b = jnp.where(d == 0, big - 1, 0)
    g_b = (r // big) * big
    g_s = (r // small) * small
    src = jnp.where(blk == 0, r,
                    jnp.where(blk <= 4, big * jj + end_b,
                              jnp.where(blk <= 8, g_b + small * jj + end_s,
                                        jnp.where(blk <= 12, g_s + jj,
                                                  jnp.where(blk == 13, g_s + end_s, g_b + end_b)))))
    m15 = jnp.where((src - cc) * sgn >= 0, 1.0, 0.0).astype(BF16)

    row = lax.broadcasted_iota(jnp.int32, (c, c), 0)
    col = lax.broadcasted_iota(jnp.int32, (c, c), 1)
    rb, cbk = row // big, col // big
    rg, cg = row // small, col // small
    lvl_b = (rb - cbk) * sgn > 0
    lvl_s = jnp.logical_and(rb == cbk, (rg - cg) * sgn > 0)
    lvl_1 = jnp.logical_and(rg == cg, (row - col) * sgn >= 0)
    last_row = jnp.where(d == 0, c - 1, 0)
    rsel = lax.broadcasted_iota(jnp.int32, (c, 1), 0) == last_row

    def heads(x):
        return jnp.stack([x[:, h * LANE:(h + 1) * LANE] for h in range(n_heads)], axis=0)

    q = q_ref[0]
    z = z_ref[0]
    val = i_ref[0]
    lg_l = lb_ref[0, 0:1]
    lg_1ml = lb_ref[0, 1:2]
    one_ml = lb_ref[0, 2:3]
    s0 = s_ref[...]

    ez = jnp.exp(-jnp.abs(z))
    log_sig = jnp.minimum(z, 0.0) - jnp.log(1.0 + ez)
    x2 = lg_1ml + log_sig
    mx = jnp.maximum(lg_l, x2)
    log_f = mx + jnp.log(jnp.exp(lg_l - mx) + jnp.exp(x2 - mx))
    k = one_ml * jnp.where(z >= 0.0, ez, 1.0) / (1.0 + ez)

    ee = _mask_dot(m15, log_f)
    cb = ee[:c]
    tot = jnp.sum(jnp.where(rsel, cb, 0.0), axis=0, keepdims=True)

    def q_refd(first):
        return heads(jnp.concatenate(
            [q * jnp.exp(jnp.minimum(cb - ee[(first + j) * c:(first + j + 1) * c], 0.0)) for j in range(4)],
            axis=0))

    s_b = _pdot_nt(q_refd(1), heads(k * jnp.exp(ee[14 * c:15 * c] - cb)))
    s_s = _pdot_nt(q_refd(5), heads(k * jnp.exp(ee[13 * c:14 * c] - cb)))
    s_1 = _pdot_nt(q_refd(9), heads(k))
    att = jnp.zeros((n_heads, c, c), F32)
    for j in range(4):
        rows = slice(j * c, (j + 1) * c)
        att += jnp.where(jnp.logical_and(lvl_b, cbk == j), s_b[:, rows], 0.0)
        att += jnp.where(jnp.logical_and(lvl_s, cg % 4 == j), s_s[:, rows], 0.0)
        att += jnp.where(jnp.logical_and(lvl_1, col % small == j), s_1[:, rows], 0.0)
    v3 = heads(val)
    o = _pdot(att, v3) + _pdot_nt(heads(q * jnp.exp(cb)), s0)
    for h in range(n_heads):
        o_ref[0, 0, :, h * LANE:(h + 1) * LANE] = o[h]
    s_ref[...] = s0 * heads(jnp.exp(tot)) + _pdot_tn(v3, heads(k * jnp.exp(tot - cb)))


def _hg_scan(phg, lbt, n_lat, w):
    b, t, _ = phg.shape
    nc = t // CHUNK
    nlc = n_lat // CHUNK
    nh = w // LANE
    ch = functools.partial(_scan_chunk, n_lat_chunks=nlc, n_chunks=nc)
    col = lambda f: pl.BlockSpec((1, CHUNK, w), f)
    return pl.pallas_call(
        functools.partial(_hg_scan_kernel, n_heads=nh), name="hg_scan",
        grid=(2, b, nc),
        in_specs=[col(lambda d, bi, i: (bi, ch(d, i), 0)),
                  col(lambda d, bi, i: (bi, ch(d, i), 1 + d)),
                  col(lambda d, bi, i: (bi, ch(d, i), 3)),
                  pl.BlockSpec((1, 3, w), lambda d, bi, i: (d, 0, 0))],
        out_specs=pl.BlockSpec((1, 1, CHUNK, w), lambda d, bi, i: (d, bi, ch(d, i), 0)),
        out_shape=jax.ShapeDtypeStruct((2, b, t, w), F32),
        scratch_shapes=[pltpu.VMEM((nh, LANE, LANE), F32)],
        compiler_params=_params(("arbitrary", "arbitrary", "arbitrary"), 32 << 20),
    )(phg, phg, phg, lbt)


def _hg_finish_kernel(o_ref, g_ref, ng_ref, out_ref, *, n_heads):
    for h in range(n_heads):
        sl = slice(h * LANE, (h + 1) * LANE)
        o = o_ref[0, 0, :, sl] + o_ref[1, 0, :, sl]
        y = o * lax.rsqrt(jnp.mean(o * o, axis=-1, keepdims=True) + NORM_EPS) * ng_ref[...]
        g = g_ref[0, :, sl]
        out_ref[0, :, sl] = (y * g * _sigmoid(g)).astype(out_ref.dtype)


def _hg_finish(o, phg, norm_g, w):
    _, b, t, _ = o.shape
    tm = _pick(t, (256, 128, 64))
    return pl.pallas_call(
        functools.partial(_hg_finish_kernel, n_heads=w // LANE), name="hg_finish",
        grid=(b, t // tm),
        in_specs=[pl.BlockSpec((2, 1, tm, w), lambda bi, i: (0, bi, i, 0)),
                  pl.BlockSpec((1, tm, w), lambda bi, i: (bi, i, 4)),
                  pl.BlockSpec((1, LANE), lambda bi, i: (0, 0))],
        out_specs=pl.BlockSpec((1, tm, w), lambda bi, i: (bi, i, 0)),
        out_shape=jax.ShapeDtypeStruct((b, t, w), BF16),
        compiler_params=_params(("parallel", "arbitrary"), 32 << 20),
    )(o, phg, norm_g.reshape(1, LANE))


def _hy_pre_kernel(prev_ref, cur_ref, next_ref, cw_ref, cb_ref, x0_ref, z_ref, *, n_lat_blocks, n_blocks, w):
    i = pl.program_id(1)
    tm = cur_ref.shape[1]
    hb = prev_ref.shape[1]
    cur = cur_ref[0].astype(F32)
    first = jnp.logical_or(i == 0, i == n_lat_blocks)
    last = jnp.logical_or(i == n_lat_blocks - 1, i == n_blocks - 1)
    row = lax.broadcasted_iota(jnp.int32, (tm, 1), 0)
    p_last = prev_ref[0, hb - 1:hb].astype(F32) * jnp.where(first, 0.0, 1.0)
    n_first = next_ref[0, 0:1].astype(F32) * jnp.where(last, 0.0, 1.0)
    before = jnp.where(row == 0, p_last, pltpu.roll(cur, 1, 0))
    after = jnp.where(row == tm - 1, n_first, pltpu.roll(cur, tm - 1, 0))
    u = cw_ref[0:1] * before + cw_ref[1:2] * cur + cw_ref[2:3] * after + cb_ref[...]
    x0_ref[0] = u[:, 0:w].astype(x0_ref.dtype)
    z_ref[0] = (u[:, w:2 * w] * u[:, 2 * w:3 * w]).astype(z_ref.dtype)


def _hy_pre(phy, conv_w, conv_b, n_lat):
    b, t, w3 = phy.shape
    w = w3 // 3
    tm = _pick(math.gcd(n_lat, t - n_lat), (256, 128, 64))
    nb = t // tm
    hb = 16
    per = tm // hb
    o = pl.BlockSpec((1, tm, w), lambda bi, i: (bi, i, 0))
    s = jax.ShapeDtypeStruct((b, t, w), BF16)
    return pl.pallas_call(
        functools.partial(_hy_pre_kernel, n_lat_blocks=n_lat // tm, n_blocks=nb, w=w), name="hy_pre",
        grid=(b, nb),
        in_specs=[pl.BlockSpec((1, hb, w3), lambda bi, i: (bi, jnp.maximum(i * per - 1, 0), 0)),
                  pl.BlockSpec((1, tm, w3), lambda bi, i: (bi, i, 0)),
                  pl.BlockSpec((1, hb, w3), lambda bi, i: (bi, jnp.minimum((i + 1) * per, t // hb - 1), 0)),
                  pl.BlockSpec((3, w3), lambda bi, i: (0, 0)),
                  pl.BlockSpec((1, w3), lambda bi, i: (0, 0))],
        out_specs=[o, o],
        out_shape=[s, s],
        compiler_params=_params(("parallel", "arbitrary"), 40 << 20),
    )(phy, phy, phy, conv_w, conv_b.reshape(1, w3))


def _hdot(a, b):
    return jnp.dot(a, b, precision=lax.Precision.HIGHEST, preferred_element_type=F32)


def _hy_taps_kernel(ft_ref, w1_ref, b1_ref, w2_ref, b2_ref, w3f_ref, w3b_ref, fr_ref, dl_ref,
                    tap_ref, sum_ref):
    i = pl.program_id(1)
    ft = ft_ref[...]
    fr = fr_ref[...]
    h = jnp.sin(fr * (_hdot(ft, w1_ref[...]) + b1_ref[...]))
    h = jnp.sin(fr * (_hdot(h, w2_ref[...]) + b2_ref[...]))
    cf = ft[:, HY_EMB:HY_EMB + 1]
    cb = ft[:, HY_EMB + 1:HY_EMB + 2]
    tt = ft[:, 0:1]
    tap = (cf * _hdot(h, w3f_ref[...]) + cb * _hdot(h, w3b_ref[...])) * jnp.exp(-tt * dl_ref[...])
    tap_ref[...] = tap

    @pl.when(i == 0)
    def _():
        sum_ref[...] = jnp.zeros_like(sum_ref)

    sum_ref[...] += jnp.sum(jnp.abs(tap), axis=0, keepdims=True)


def _hy_taps(feats, w1p, b1, w2, b2, w3, freq, deltas):
    rws = feats.shape[0]
    hid = w2.shape[0]
    w = w3.shape[1] // 2
    tr = _pick(rws, (1024, 512, 256, 128))
    ct = w
    nj = w // ct
    c2 = lambda a: pl.BlockSpec(a.shape, lambda j, i: (0, 0))
    return pl.pallas_call(
        _hy_taps_kernel, name="hy_taps",
        grid=(nj, rws // tr),
        in_specs=[pl.BlockSpec((tr, LANE), lambda j, i: (i, 0)),
                  c2(w1p), pl.BlockSpec((1, hid), lambda j, i: (0, 0)),
                  c2(w2), pl.BlockSpec((1, hid), lambda j, i: (0, 0)),
                  pl.BlockSpec((hid, ct), lambda j, i: (0, j)),
                  pl.BlockSpec((hid, ct), lambda j, i: (0, nj + j)),
                  pl.BlockSpec((1, hid), lambda j, i: (0, 0)),
                  pl.BlockSpec((1, ct), lambda j, i: (0, j))],
        out_specs=[pl.BlockSpec((tr, ct), lambda j, i: (i, j)),
                   pl.BlockSpec((1, ct), lambda j, i: (0, j))],
        out_shape=[jax.ShapeDtypeStruct((rws, w), F32), jax.ShapeDtypeStruct((1, w), F32)],
        compiler_params=_params(("parallel", "arbitrary"), 32 << 20),
    )(feats, w1p, b1.reshape(1, hid), w2, b2.reshape(1, hid), w3, w3, freq.reshape(1, hid),
      deltas.reshape(1, w))


def _dft1_kernel(g_ref, x_ref, o_ref):
    o_ref[...] = lax.dot_general(g_ref[...], x_ref[...].astype(BF16), (((2,), (1,)), ((0,), (0,))),
                                 preferred_element_type=F32)


def _dft1(g, x):
    nb, m2, k = g.shape
    c = x.shape[2]
    bt = _pick(nb, (8, 4, 2))
    ct = _pick(c, (256, 128))
    return pl.pallas_call(
        _dft1_kernel, name="dft1",
        grid=(c // ct, nb // bt),
        in_specs=[pl.BlockSpec((bt, m2, k), lambda j, i: (i, 0, 0)),
                  pl.BlockSpec((bt, k, ct), lambda j, i: (i, 0, j))],
        out_specs=pl.BlockSpec((bt, m2, ct), lambda j, i: (i, 0, j)),
        out_shape=jax.ShapeDtypeStruct((nb, m2, c), F32),
        compiler_params=_params(("parallel", "arbitrary"), 32 << 20),
    )(g, x)


def _spec_kernel(a_ref, f2_ref, sc_ref, o_ref):
    for l in range(a_ref.shape[0]):
        o_ref[l] = jnp.dot(f2_ref[...], a_ref[l].astype(BF16), preferred_element_type=F32) * sc_ref[...]


def _spec(a, f2, scale):
    na, m2, c = a.shape
    kt = _pick(na, (8, 4, 2))
    ct = _pick(c, (256, 128))
    return pl.pallas_call(
        _spec_kernel, name="hy_spec",
        grid=(c // ct, na // kt),
        in_specs=[pl.BlockSpec((kt, m2, ct), lambda j, i: (i, 0, j)),
                  pl.BlockSpec((m2, m2), lambda j, i: (0, 0)),
                  pl.BlockSpec((1, ct), lambda j, i: (0, j))],
        out_specs=pl.BlockSpec((kt, m2, ct), lambda j, i: (i, 0, j)),
        out_shape=jax.ShapeDtypeStruct((na, m2, c), F32),
        compiler_params=_params(("parallel", "arbitrary"), 32 << 20),
    )(a, f2, scale)


def _conv_mid_kernel(a_ref, f2_ref, h_ref, g3_ref, o_ref):
    nbh = a_ref.shape[1] // 2
    for l in range(a_ref.shape[0]):
        x = jnp.dot(f2_ref[...], a_ref[l].astype(BF16), preferred_element_type=F32)
        xr, xi = x[:nbh], x[nbh:]
        hr, hi = h_ref[l, :nbh], h_ref[l, nbh:]
        y = jnp.concatenate([xr * hr - xi * hi, xr * hi + xi * hr], axis=0)
        o_ref[l] = jnp.dot(g3_ref[l], y.astype(BF16), preferred_element_type=F32)


def _conv_mid(a, f2, h, g3):
    na, m2, c = a.shape
    kt = _pick(na, (8, 4, 2))
    ct = _pick(c, (256, 128))
    slab = pl.BlockSpec((kt, m2, ct), lambda j, i: (i, 0, j))
    return pl.pallas_call(
        _conv_mid_kernel, name="conv_mid",
        grid=(c // ct, na // kt),
        in_specs=[slab, pl.BlockSpec((m2, m2), lambda j, i: (0, 0)), slab,
                  pl.BlockSpec((kt, m2, m2), lambda j, i: (i, 0, 0))],
        out_specs=slab,
        out_shape=jax.ShapeDtypeStruct((na, m2, c), F32),
        compiler_params=_params(("parallel", "arbitrary"), 32 << 20),
    )(a, f2, h, g3)


def _conv_out_kernel(b_ref, f4_ref, o_ref, *, scale):
    for l in range(b_ref.shape[0]):
        o_ref[l] = jnp.dot(f4_ref[...], b_ref[l].astype(BF16), preferred_element_type=F32) * scale


def _conv_out(bm, f4, scale):
    nb, m2, c = bm.shape
    mo = f4.shape[0]
    pt = _pick(nb, (8, 4, 2))
    ct = _pick(c, (256, 128))
    return pl.pallas_call(
        functools.partial(_conv_out_kernel, scale=scale), name="conv_out",
        grid=(c // ct, nb // pt),
        in_specs=[pl.BlockSpec((pt, m2, ct), lambda j, i: (i, 0, j)),
                  pl.BlockSpec((mo, m2), lambda j, i: (0, 0))],
        out_specs=pl.BlockSpec((pt, mo, ct), lambda j, i: (i, 0, j)),
        out_shape=jax.ShapeDtypeStruct((nb, mo, c), F32),
        compiler_params=_params(("parallel", "arbitrary"), 32 << 20),
    )(bm, f4)


def _ctx_conv_kernel(z_ref, ext_ref, sc_ref, o_ref):
    n = z_ref.shape[1]

    def body(s, acc):
        return acc + ext_ref[pl.ds(n - s, n), :] * z_ref[0, pl.ds(s, 1), :]

    acc = lax.fori_loop(0, n, body, jnp.zeros(o_ref.shape[1:], F32))
    o_ref[0] = acc * sc_ref[...]


def _ctx_conv(z, ext, scale):
    b, n, w = z.shape
    ct = LANE
    return pl.pallas_call(
        _ctx_conv_kernel, name="ctx_conv",
        grid=(b, w // ct),
        in_specs=[pl.BlockSpec((1, n, ct), lambda bi, j: (bi, 0, j)),
                  pl.BlockSpec((2 * n, ct), lambda bi, j: (0, j)),
                  pl.BlockSpec((1, ct), lambda bi, j: (0, j))],
        out_specs=pl.BlockSpec((1, n, ct), lambda bi, j: (bi, 0, j)),
        out_shape=jax.ShapeDtypeStruct((b, n, w), F32),
        compiler_params=_params(("parallel", "arbitrary"), 16 << 20),
    )(z, ext, scale)


def _hy_post_kernel(y_ref, z_ref, x0_ref, bias_ref, o_ref):
    o_ref[0] = (x0_ref[0].astype(F32) * (y_ref[0] + bias_ref[...] * z_ref[0].astype(F32))).astype(o_ref.dtype)


def _hy_post(y, z, x0, bias):
    b, t, w = y.shape
    tm = _pick(t, (768, 384, 256, 128, 64))
    s = pl.BlockSpec((1, tm, w), lambda bi, i: (bi, i, 0))
    return pl.pallas_call(
        _hy_post_kernel, name="hy_post",
        grid=(b, t // tm),
        in_specs=[s, s, s, pl.BlockSpec((1, w), lambda bi, i: (0, 0))],
        out_specs=s,
        out_shape=jax.ShapeDtypeStruct((b, t, w), BF16),
        compiler_params=_params(("parallel", "arbitrary"), 32 << 20),
    )(y, z, x0, bias.reshape(1, w))


def _dft_factors(n):
    m = 2 * n
    na = 1 << ((m.bit_length() - 1) // 2)
    return na, m // na


def _cis(num, den, sign):
    ang = (num % den).astype(F32) * (2.0 * math.pi / den)
    return jnp.cos(ang), sign * jnp.sin(ang)


def _cblock(cr, ci):
    return jnp.concatenate([jnp.concatenate([cr, -ci], axis=-1),
                            jnp.concatenate([ci, cr], axis=-1)], axis=-2)


def _dft_tables(n):
    na, nb = _dft_factors(n)
    m = na * nb
    ah = na // 2
    ar = lambda k: jnp.arange(k, dtype=jnp.int32)
    ka, bb = ar(na)[None, :, None], ar(nb)[:, None, None]
    g1c = _cblock(*_cis(ka * (nb * ar(ah)[None, None, :] + bb), m, -1.0))
    g1t = jnp.concatenate(_cis(ka * (nb * ar(na)[None, None, :] + bb), m, -1.0), axis=-2)
    f2 = _cblock(*_cis(ar(nb)[:, None] * ar(nb)[None, :], nb, -1.0))
    g3 = _cblock(*_cis(ar(nb)[None, :, None] * (ar(na)[:, None, None] + na * ar(nb)[None, None, :]), m, 1.0))
    f4 = _cblock(*_cis(ar(ah)[:, None] * ar(na)[None, :], na, 1.0))
    return tuple(t.astype(BF16) for t in (g1c, g1t, f2, g3, f4))


def _filter_feats(n, order):
    bands_n = (HY_EMB - 1) // 2
    t = jnp.linspace(0.0, 1.0, n, dtype=F32)[:, None]
    lag = jnp.arange(n, dtype=F32)[:, None]
    bands = jnp.linspace(1e-4, bands_n - 1, bands_n, dtype=F32)[None, :]
    ang = 2.0 * math.pi * lag * bands / n
    zf = jnp.concatenate([t, jnp.cos(ang), -jnp.sin(ang)], axis=-1)
    if order == "dft":
        na, nb = _dft_factors(n)
        m = (np.arange(na)[None, :] * nb + np.arange(nb)[:, None]).reshape(-1)
    else:
        m = (np.arange(2 * n) - n) % (2 * n)
    lag_of = np.where(m < n, m, np.where(m == n, 0, 2 * n - m))
    cf = np.where(m == 0, 0.5, np.where(m < n, 1.0, 0.0)).astype(np.float32)
    cb = np.where(m == 0, 0.5, np.where(m > n, 1.0, 0.0)).astype(np.float32)
    pad = jnp.zeros((2 * n, LANE - HY_EMB - 2), F32)
    return jnp.concatenate([zf[lag_of], jnp.asarray(cf)[:, None], jnp.asarray(cb)[:, None], pad], axis=-1)


def _hy_deltas(w):
    return jnp.abs(jnp.linspace(math.log(HY_DECAY_TARGET) / HY_SLOW_PCT,
                                math.log(HY_DECAY_TARGET) / HY_FAST_PCT, w, dtype=F32))


def _long_conv_latent(z, taps_perm, inv_l1):
    b, n, w = z.shape
    na, nb = _dft_factors(n)
    ah = na // 2
    g1c, g1t, f2, g3, f4 = _dft_tables(n)
    h = _spec(_dft1(g1t, taps_perm.reshape(nb, na, w)).reshape(nb, 2, na, w).transpose(2, 1, 0, 3)
              .reshape(na, 2 * nb, w), f2, inv_l1)
    outs = []
    for pair in range(b // 2):
        zz = z[2 * pair:2 * pair + 2].astype(BF16).reshape(2, ah, nb, w).transpose(2, 0, 1, 3).reshape(nb, na, w)
        a = _dft1(g1c, zz).reshape(nb, 2, na, w).transpose(2, 1, 0, 3).reshape(na, 2 * nb, w)
        bm = _conv_mid(a, f2, h, g3).reshape(na, 2, nb, w).transpose(2, 1, 0, 3).reshape(nb, 2 * na, w)
        y = _conv_out(bm, f4, 1.0 / (na * nb))
        outs.append(y.reshape(nb, 2, ah, w).transpose(1, 2, 0, 3).reshape(2, n, w))
    return jnp.concatenate(outs, axis=0) if len(outs) > 1 else outs[0]


def _pack_in_proj(w_in, rw_mu, dims):
    depth, d_model, _ = w_in.shape
    w, dl, al, gl = dims
    ld, la, lg = _rup(dl, LANE), _rup(al, LANE), _rup(gl, LANE)
    rw_cols = 3 * w + 2 * dl + 2 * al + gl
    tile = 1024 if d_model >= 2048 else LANE
    na = _rup(3 * w + 2 * ld + 2 * la + lg, tile)
    pieces = [(0, 3 * w, 0), (3 * w, dl, 3 * w), (3 * w + dl, dl, 3 * w + ld),
              (3 * w + 2 * dl, al, 3 * w + 2 * ld), (3 * w + 2 * dl + al, al, 3 * w + 2 * ld + la),
              (3 * w + 2 * dl + 2 * al, gl, 3 * w + 2 * ld + 2 * la)]
    cols, mus = [], []
    orig = np.full((na,), rw_cols - 1, np.int64)
    pos = 0
    for src, width, dst in pieces + [(rw_cols, 0, na)]:
        if dst > pos:
            cols.append(jnp.zeros((depth, d_model, dst - pos), BF16))
            mus.append(jnp.zeros((depth, dst - pos), F32))
        cols.append(w_in[:, :, src:src + width].astype(BF16))
        mus.append(rw_mu[:, src:src + width])
        orig[dst:dst + width] = np.arange(src, src + width)
        pos = dst + width
    offs = []
    src = rw_cols
    for width in (5 * w, 3 * w, w_in.shape[2] - rw_cols - 8 * w):
        offs.append(pos // tile)
        cols.append(w_in[:, :, src:src + width].astype(BF16))
        pad = _rup(width, tile) - width
        if pad:
            cols.append(jnp.zeros((depth, d_model, pad), BF16))
        src += width
        pos += width + pad
    wt = _tile_w(jnp.concatenate(cols, axis=2), tile)
    mu = jnp.concatenate(mus, axis=1).reshape(depth, 1, na)
    grp = np.stack([orig // (rw_cols // 4), orig // (rw_cols // 2)]).astype(np.int32)
    grp = np.minimum(grp, np.array([[3], [1]])).astype(np.int32)
    return wt, tile, offs, mu, jnp.asarray(grp), (w, ld, la, lg), na


def _pad_rows(a, rows):
    return jnp.pad(a, [(0, 0)] * (a.ndim - 2) + [(0, rows - a.shape[-2]), (0, 0)])


def kernel(x, c, ctx, c_ctx, ada_w, ada_b, norm1_g, norm2_g, w_in, rw_mu, rw_w0, rw_w_up, rw_a0, rw_a_up, rw_g_up, rw_k_k, rw_k_a, rw_r_k, rw_ln_g, rw_ln_b, hg_lower_bounds, hg_norm_g, hy_conv_w, hy_conv_b, hy_f_w1, hy_f_b1, hy_f_w2, hy_f_b2, hy_f_w3, hy_freq, hy_bias, w_branch_a, w_branch_b, w_branch_c, w_out, ffn_w_gate, ffn_w_up, ffn_w_down, final_norm_g):
    bsz, n_lat, d = x.shape
    n_ctx = ctx.shape[1]
    depth = w_in.shape[0]
    w = rw_k_k.shape[1]
    dims = (w, rw_w_up.shape[2], rw_a_up.shape[2], rw_g_up.shape[1])
    assert bsz % 2 == 0 and n_lat % GRID_W == 0 and n_ctx % CHUNK == 0 and w % LANE == 0
    rw_cols = 3 * w + 2 * dims[1] + 2 * dims[2] + dims[3]
    hg_end = rw_cols + 5 * w
    hy_end = hg_end + 3 * w

    xs = jnp.concatenate([x, ctx], axis=1)
    cvec = jnp.zeros((_rup(bsz + 1, 8), d), F32).at[:bsz].set(c).at[bsz].set(c_ctx)
    lb_cum = jnp.cumsum(jax.nn.softmax(hg_lower_bounds.astype(F32), axis=0), axis=0)
    deltas = _hy_deltas(w)

    wt_in, tile, offs, mu_all, grp, pdims, na = _pack_in_proj(w_in, rw_mu, dims)
    assert (5 * w) % tile == 0 and (3 * w) % tile == 0 and (3 * d) % tile == 0
    _, ld, la, lg = pdims
    hid = ffn_w_gate.shape[2]
    hp = _rup(hid, 1024) if hid > 1024 else _rup(hid, LANE)
    wt_g = _tile_w(jnp.pad(ffn_w_gate.astype(BF16), ((0, 0), (0, 0), (0, hp - hid))), _tn_ffn(hp))
    wt_u = _tile_w(jnp.pad(ffn_w_up.astype(BF16), ((0, 0), (0, 0), (0, hp - hid))), _tn_ffn(hp))
    wt_d = _tile_w(jnp.pad(ffn_w_down.astype(BF16), ((0, 0), (0, hp - hid), (0, 0))), _tn_wide(d), _tk_of(hp))
    wt_o = _tile_w(w_out, _tn_wide(d), _tk_of(d))
    wt_a, wt_b, wt_c = (_tile_w(t, _tn_wide(d)) for t in (w_branch_a, w_branch_b, w_branch_c))

    for layer in range(depth):
        need_ctx = layer < depth - 1
        mod = _ada(cvec, ada_w, ada_b, layer)[:bsz + 1].reshape(bsz + 1, 1, 6, d)
        mods = [mod[:, :, s, :] for s in range(6)]

        h1 = _modnorm(xs, norm1_g[layer], mods[0], mods[1], n_lat)
        pa = _matmul(h1, wt_in, layer, 0, na // tile, BF16)
        phg = _matmul(h1, wt_in, layer, offs[0], 5 * w // tile, F32)
        phy = _matmul(h1, wt_in, layer, offs[1], 3 * w // tile, BF16)
        pgate = _matmul(h1, wt_in, layer, offs[2], 3 * d // tile, BF16)

        r, v, kn, g, bonus, lw, kd, bb = _rw_prep(
            pa, grp, mu_all[layer], rw_w0[layer].reshape(2, 1, w), _pad_rows(rw_w_up[layer], ld),
            rw_a0[layer].reshape(2, 1, w), _pad_rows(rw_a_up[layer], la), _pad_rows(rw_g_up[layer], lg),
            rw_k_k[layer].reshape(1, w), rw_k_a[layer].reshape(1, w), rw_r_k[layer].reshape(1, w),
            n_lat, pdims)
        y_rw = _rw_scan(r, v, kn, lw, kd, bb, n_lat)
        oa = _rw_finish(y_rw, bonus, g, rw_ln_g[layer], rw_ln_b[layer])

        lb = lb_cum[layer] - lb_cum[0]
        lbt = jnp.stack([jnp.log(lb), jnp.log1p(-lb), 1.0 - lb], axis=1)
        o_hg = _hg_scan(phg, lbt, n_lat, w)
        ob = _hg_finish(o_hg, phg, hg_norm_g[layer], w)

        x0, z = _hy_pre(phy, hy_conv_w[layer], hy_conv_b[layer], n_lat)
        w1p = _pad_rows(hy_f_w1[layer], LANE)
        filt = (w1p, hy_f_b1[layer], hy_f_w2[layer], hy_f_b2[layer], hy_f_w3[layer], hy_freq[layer], deltas)
        taps, l1 = _hy_taps(_filter_feats(n_lat, "dft"), *filt)
        y_lat = _long_conv_latent(z[:, :n_lat], taps, 1.0 / l1)
        if need_ctx:
            ext, l1c = _hy_taps(_filter_feats(n_ctx, "lag"), *filt)
            y_ctx = _ctx_conv(z[:, n_lat:].astype(F32), ext, 1.0 / l1c)
        else:
            y_ctx = jnp.zeros((bsz, n_ctx, w), F32)
        oc = _hy_post(jnp.concatenate([y_lat, y_ctx], axis=1), z, x0, hy_bias[layer])

        ym = _merge(oa, ob, oc, wt_a, wt_b, wt_c, layer, pgate)
        xs = _matmul_resid(ym, wt_o, layer, xs, mods[2], n_lat)

        h2 = _modnorm(xs, norm2_g[layer], mods[3], mods[4], n_lat)
        act = _swiglu_up(h2, wt_g, wt_u, layer)
        xs = _matmul_resid(act, wt_d, layer, xs, mods[5], n_lat)

    return _final_norm(xs, final_norm_g, n_lat)
```

```python
import functools
import math

import numpy as np
import jax
import jax.numpy as jnp
from jax import lax
from jax.experimental import pallas as pl
from jax.experimental.pallas import tpu as pltpu

F32 = jnp.float32
BF16 = jnp.bfloat16

GRID_W = 64
CHUNK = 64
SUB = 8
NORM_EPS = 1e-6
RW_HEAD = 64
RW_GN_EPS = 64e-5
HG_EXPAND = 128
HY_EMB = 33
HY_DECAY_TARGET = 1e-2
HY_FAST_PCT = 0.3
HY_SLOW_PCT = 1.5
LANE = 128
VMEM_CAP = 56 * 1024 * 1024
VMEM_SLACK = 8 * 1024 * 1024


def _params(sem, vmem_bytes):
    return pltpu.CompilerParams(dimension_semantics=sem,
                                vmem_limit_bytes=int(min(max(vmem_bytes + VMEM_SLACK, 16 << 20), VMEM_CAP)))


def _pick(n, cands):
    for c in cands:
        if n % c == 0:
            return c
    return n


def _rup(n, m):
    return -(-n // m) * m


def _bdot(a, b):
    return jnp.dot(a.astype(BF16), b.astype(BF16), preferred_element_type=F32)


def _bdot_nt(a, b):
    return lax.dot_general(a.astype(BF16), b.astype(BF16), (((1,), (1,)), ((), ())),
                           preferred_element_type=F32)


def _bdot_tn(a, b):
    return lax.dot_general(a.astype(BF16), b.astype(BF16), (((0,), (0,)), ((), ())),
                           preferred_element_type=F32)


def _split(x):
    hi = x.astype(BF16)
    lo = (x - hi.astype(F32)).astype(BF16)
    return hi, lo


def _mask_dot(m, x):
    hi, lo = _split(x)
    return (jnp.dot(m, hi, preferred_element_type=F32) + jnp.dot(m, lo, preferred_element_type=F32))


def _x_mask_dot(x, m):
    hi, lo = _split(x)
    return (jnp.dot(hi, m, preferred_element_type=F32) + jnp.dot(lo, m, preferred_element_type=F32))


def _sigmoid(x):
    return 1.0 / (1.0 + jnp.exp(-x))


def _order_masks(d, n, reps=1):
    row = lax.broadcasted_iota(jnp.int32, (n, reps * n), 0)
    col = lax.broadcasted_iota(jnp.int32, (n, reps * n), 1) % n
    diff = (row - col) * (1 - 2 * d)
    return diff >= 0, diff > 0


def _ada_kernel(c_ref, w_ref, b_ref, o_ref):
    c = c_ref[...]
    o_ref[...] = _bdot(c * _sigmoid(c), w_ref[...]) + b_ref[...]


def _ada(cvec, w, b, layer):
    rows, d = cvec.shape
    n = w.shape[2]
    tn = _pick(n, (1024, 512, 256, 128))
    return pl.pallas_call(
        _ada_kernel, name="ada",
        grid=(n // tn,),
        in_specs=[pl.BlockSpec((rows, d), lambda j: (0, 0)),
                  pl.BlockSpec((None, d, tn), lambda j: (layer, 0, j)),
                  pl.BlockSpec((None, 1, tn), lambda j: (layer, 0, j))],
        out_specs=pl.BlockSpec((rows, tn), lambda j: (0, j)),
        out_shape=jax.ShapeDtypeStruct((rows, n), F32),
        compiler_params=_params(("arbitrary",), 3 * d * tn * 4),
    )(cvec, w, b.reshape(b.shape[0], 1, n))


def _modnorm_kernel(x_ref, g_ref, shl_ref, scl_ref, shc_ref, scc_ref, o_ref, *, n_lat):
    tm = x_ref.shape[1]
    x = x_ref[0]
    y = x * lax.rsqrt(jnp.mean(x * x, axis=-1, keepdims=True) + NORM_EPS) * g_ref[...]
    pos = pl.program_id(1) * tm + lax.broadcasted_iota(jnp.int32, (tm, 1), 0)
    is_ctx = pos >= n_lat
    sc = jnp.where(is_ctx, scc_ref[0], scl_ref[0])
    sh = jnp.where(is_ctx, shc_ref[0], shl_ref[0])
    o_ref[0] = (y * (1.0 + sc) + sh).astype(o_ref.dtype)


def _modnorm(x, gain, shift, scale, n_lat):
    b, t, d = x.shape
    tm = _pick(t, (256, 128, 64))
    vec = lambda f: pl.BlockSpec((1, 1, d), f)
    return pl.pallas_call(
        functools.partial(_modnorm_kernel, n_lat=n_lat), name="modnorm",
        grid=(b, t // tm),
        in_specs=[pl.BlockSpec((1, tm, d), lambda bi, i: (bi, i, 0)),
                  pl.BlockSpec((1, d), lambda bi, i: (0, 0)),
                  vec(lambda bi, i: (bi, 0, 0)), vec(lambda bi, i: (bi, 0, 0)),
                  vec(lambda bi, i: (b, 0, 0)), vec(lambda bi, i: (b, 0, 0))],
        out_specs=pl.BlockSpec((1, tm, d), lambda bi, i: (bi, i, 0)),
        out_shape=jax.ShapeDtypeStruct((b, t, d), BF16),
        compiler_params=_params(("parallel", "arbitrary"), 6 * tm * d * 4),
    )(x, gain.reshape(1, d), shift, scale, shift, scale)


def _rmsnorm_kernel(x_ref, g_ref, o_ref):
    x = x_ref[0]
    o_ref[0] = x * lax.rsqrt(jnp.mean(x * x, axis=-1, keepdims=True) + NORM_EPS) * g_ref[...]


def _final_norm(x, gain, n_lat):
    b, _, d = x.shape
    tm = _pick(n_lat, (256, 128, 64))
    return pl.pallas_call(
        _rmsnorm_kernel, name="final_norm",
        grid=(b, n_lat // tm),
        in_specs=[pl.BlockSpec((1, tm, d), lambda bi, i: (bi, i, 0)),
                  pl.BlockSpec((1, d), lambda bi, i: (0, 0))],
        out_specs=pl.BlockSpec((1, tm, d), lambda bi, i: (bi, i, 0)),
        out_shape=jax.ShapeDtypeStruct((b, n_lat, d), F32),
        compiler_params=_params(("parallel", "arbitrary"), 6 * tm * d * 4),
    )(x, gain.reshape(1, d))


def _mm_kernel(x_ref, w_ref, o_ref):
    o_ref[0] = jnp.dot(x_ref[0], w_ref[...], preferred_element_type=F32).astype(o_ref.dtype)


def _tn_wide(n):
    return _pick(n, (1024, 512, 256, 128))


def _tn_ffn(n):
    return _pick(n, (512, 256, 128))


def _tk_of(k):
    return k if k <= 4096 else _pick(k, (2816, 2048, 1024, 512, 256, 128))


class _TiledW:
    def __init__(self, w, tn, tk=None):
        self.w = w.astype(BF16)
        self.tk = w.shape[1] if tk is None else tk
        self.tn = tn
        self.n = w.shape[2]


def _tile_w(w, tn, tk=None):
    return _TiledW(w, tn, tk)


def _wspec(tk, tn, f):
    def idx(*a):
        lay, j, l = f(*a)[:3]
        return lay, l, j
    return pl.BlockSpec((None, tk, tn), idx)


_TM_ROWS = (1024, 768, 512, 384, 256, 128, 64)


def _matmul(x, wt, layer, j0, nj, out_dtype, rows=None):
    b, _, k = x.shape
    t = x.shape[1] if rows is None else rows
    tn = wt.tn
    n = nj * tn
    tm = _pick(t, _TM_ROWS)
    osz = jnp.dtype(out_dtype).itemsize
    return pl.pallas_call(
        _mm_kernel, name="in_proj",
        grid=(b, t // tm, n // tn),
        in_specs=[pl.BlockSpec((1, tm, k), lambda bi, i, j: (bi, i, 0)),
                  _wspec(k, tn, lambda bi, i, j: (layer, j0 + j, 0, 0, 0))],
        out_specs=pl.BlockSpec((1, tm, tn), lambda bi, i, j: (bi, i, j)),
        out_shape=jax.ShapeDtypeStruct((b, t, n), out_dtype),
        compiler_params=_params(("parallel", "parallel", "arbitrary"),
                                2 * (tm * k * 2 + k * tn * 2 + tm * tn * osz) + tm * tn * 4),
    )(x, wt.w)


def _swiglu_kernel(x_ref, wg_ref, wu_ref, o_ref):
    x = x_ref[0]
    g = jnp.dot(x, wg_ref[...], preferred_element_type=F32)
    u = jnp.dot(x, wu_ref[...], preferred_element_type=F32)
    o_ref[0] = (g * _sigmoid(g) * u).astype(o_ref.dtype)


def _swiglu_up(x, wg, wu, layer):
    b, t, k = x.shape
    tn = wg.tn
    n = wg.n
    tm = _pick(t, _TM_ROWS)
    wsp = lambda: _wspec(k, tn, lambda bi, i, j: (layer, j, 0, 0, 0))
    return pl.pallas_call(
        _swiglu_kernel, name="swiglu_up",
        grid=(b, t // tm, n // tn),
        in_specs=[pl.BlockSpec((1, tm, k), lambda bi, i, j: (bi, i, 0)), wsp(), wsp()],
        out_specs=pl.BlockSpec((1, tm, tn), lambda bi, i, j: (bi, i, j)),
        out_shape=jax.ShapeDtypeStruct((b, t, n), BF16),
        compiler_params=_params(("parallel", "parallel", "arbitrary"),
                                2 * (tm * k * 2 + 2 * k * tn * 2 + tm * tn * 2) + 3 * tm * tn * 4),
    )(x, wg.w, wu.w)


def _resid_kernel(x_ref, w_ref, r_ref, gl_ref, gc_ref, o_ref, acc_ref, *, n_lat):
    kk = pl.program_id(3)

    @pl.when(kk == 0)
    def _():
        acc_ref[...] = jnp.zeros_like(acc_ref)

    acc_ref[...] += jnp.dot(x_ref[0], w_ref[...], preferred_element_type=F32)

    @pl.when(kk == pl.num_programs(3) - 1)
    def _():
        tm = acc_ref.shape[0]
        pos = pl.program_id(1) * tm + lax.broadcasted_iota(jnp.int32, (tm, 1), 0)
        gate = jnp.where(pos >= n_lat, gc_ref[0], gl_ref[0])
        o_ref[0] = r_ref[0] + gate * acc_ref[...]


def _matmul_resid(x, w, layer, res, gate, n_lat):
    b, t, k = x.shape
    tk, tn = w.tk, w.tn
    n = w.n
    tm = _pick(t, (768, 512, 384, 256, 128, 64))
    return pl.pallas_call(
        functools.partial(_resid_kernel, n_lat=n_lat), name="proj_resid",
        grid=(b, t // tm, n // tn, k // tk),
        in_specs=[pl.BlockSpec((1, tm, tk), lambda bi, i, j, l: (bi, i, l)),
                  _wspec(tk, tn, lambda bi, i, j, l: (layer, j, l, 0, 0)),
                  pl.BlockSpec((1, tm, tn), lambda bi, i, j, l: (bi, i, j)),
                  pl.BlockSpec((1, 1, tn), lambda bi, i, j, l: (bi, 0, j)),
                  pl.BlockSpec((1, 1, tn), lambda bi, i, j, l: (b, 0, j))],
        out_specs=pl.BlockSpec((1, tm, tn), lambda bi, i, j, l: (bi, i, j)),
        out_shape=jax.ShapeDtypeStruct((b, t, n), F32),
        scratch_shapes=[pltpu.VMEM((tm, tn), F32)],
        compiler_params=_params(("parallel", "parallel", "arbitrary", "arbitrary"),
                                2 * (tm * tk * 2 + tk * tn * 2 + 2 * tm * tn * 4) + 2 * tm * tn * 4),
    )(x, w.w, res, gate, gate)


def _merge_kernel(oa_ref, ob_ref, oc_ref, wa_ref, wb_ref, wc_ref, ga_ref, gb_ref, gc_ref, o_ref):
    y = _sigmoid(ga_ref[0].astype(F32)) * jnp.dot(oa_ref[0], wa_ref[...], preferred_element_type=F32)
    y += _sigmoid(gb_ref[0].astype(F32)) * jnp.dot(ob_ref[0], wb_ref[...], preferred_element_type=F32)
    y += _sigmoid(gc_ref[0].astype(F32)) * jnp.dot(oc_ref[0], wc_ref[...], preferred_element_type=F32)
    o_ref[0] = y.astype(o_ref.dtype)


def _merge(oa, ob, oc, wa, wb, wc, layer, pgate):
    b, _, kw = oa.shape
    t = pgate.shape[1]
    tn = wa.tn
    nj = wa.n // tn
    d = nj * tn
    tm = _pick(t, (768, 512, 384, 256, 128, 64))
    br = lambda: pl.BlockSpec((1, tm, kw), lambda bi, i, j: (bi, i, 0))
    wt = lambda: _wspec(kw, tn, lambda bi, i, j: (layer, j, 0, 0, 0))
    gt = lambda s: pl.BlockSpec((1, tm, tn), lambda bi, i, j: (bi, i, s * nj + j))
    gsz = jnp.dtype(pgate.dtype).itemsize
    return pl.pallas_call(
        _merge_kernel, name="merge",
        grid=(b, t // tm, nj),
        in_specs=[br(), br(), br(), wt(), wt(), wt(), gt(0), gt(1), gt(2)],
        out_specs=pl.BlockSpec((1, tm, tn), lambda bi, i, j: (bi, i, j)),
        out_shape=jax.ShapeDtypeStruct((b, t, d), BF16),
        compiler_params=_params(("parallel", "parallel", "arbitrary"),
                                2 * (3 * tm * kw * 2 + 3 * kw * tn * 2 + 3 * tm * tn * gsz + tm * tn * 2)
                                + 4 * tm * tn * 4),
    )(oa, ob, oc, wa.w, wb.w, wc.w, pgate, pgate, pgate)


def _pair_sum_matrix():
    r = lax.broadcasted_iota(jnp.int32, (LANE, LANE), 0) // RW_HEAD
    c = lax.broadcasted_iota(jnp.int32, (LANE, LANE), 1) // RW_HEAD
    return jnp.where(r == c, 1.0, 0.0).astype(BF16)


def _head_sums(x, hs):
    rows, w = x.shape
    nt = w // LANE
    stacked = jnp.concatenate([x[:, j * LANE:(j + 1) * LANE] for j in range(nt)], axis=0)
    s = _x_mask_dot(stacked, hs)
    return jnp.concatenate([s[j * rows:(j + 1) * rows] for j in range(nt)], axis=1)


def _rw_prep_kernel(prev_ref, cur_ref, next_ref, grp_ref, mu_ref, w0_ref, wup_ref, a0_ref, aup_ref,
                    gup_ref, kk_ref, ka_ref, rk_ref,
                    r_ref, v_ref, kn_ref, g_ref, bon_ref, lw_ref, kd_ref, bb_ref,
                    *, n_lat_chunks, n_chunks, w, ld, la, lg):
    i = pl.program_id(1)
    c = CHUNK
    cur = cur_ref[0].astype(F32)
    prev = prev_ref[0].astype(F32)
    nxt = next_ref[0].astype(F32)
    is_ctx = i >= n_lat_chunks
    first = jnp.logical_or(i == 0, i == n_lat_chunks)
    last = jnp.logical_or(i == n_lat_chunks - 1, i == n_chunks - 1)
    row = lax.broadcasted_iota(jnp.int32, (c, 1), 0)
    carry_in = jnp.where(jnp.logical_and(is_ctx, jnp.logical_not(first)), 1.0, 0.0)
    carry_out = jnp.where(jnp.logical_and(is_ctx, jnp.logical_not(last)), 1.0, 0.0)
    tm1 = jnp.where(row == 0, prev[c - 1:c] * carry_in, pltpu.roll(cur, 1, 0))
    tp1 = jnp.where(row == c - 1, nxt[0:1] * carry_out, pltpu.roll(cur, c - 1, 0))
    lat_up = jnp.where(jnp.logical_or(is_ctx, first), 0.0, 1.0)
    lat_dn = jnp.where(jnp.logical_or(is_ctx, last), 0.0, 1.0)
    code = jnp.where(is_ctx, grp_ref[1:2], grp_ref[0:1])
    shifted = jnp.where(code == 0, tm1,
                        jnp.where(code == 1, tp1,
                                  jnp.where(code == 2, prev * lat_up, nxt * lat_dn)))
    m = cur + mu_ref[...] * (shifted - cur)

    r = m[:, 0:w]
    k = m[:, w:2 * w]
    v = m[:, 2 * w:3 * w]
    o = 3 * w
    wd = (m[:, o:o + ld], m[:, o + ld:o + 2 * ld])
    ad = (m[:, o + 2 * ld:o + 2 * ld + la], m[:, o + 2 * ld + la:o + 2 * ld + 2 * la])
    gd = m[:, o + 2 * ld + 2 * la:o + 2 * ld + 2 * la + lg]

    hs = _pair_sum_matrix()
    g_ref[0] = _bdot(_sigmoid(gd), gup_ref[...]).astype(g_ref.dtype)
    kx = k * kk_ref[...]
    kn = kx * lax.rsqrt(jnp.maximum(_head_sums(kx * kx, hs), 1e-24))
    r_ref[0] = r.astype(r_ref.dtype)
    v_ref[0] = v.astype(v_ref.dtype)
    kn_ref[0] = kn.astype(kn_ref.dtype)
    bonus = jnp.zeros_like(r)
    for d in range(2):
        wl = w0_ref[d] + _bdot(jnp.tanh(wd[d]), wup_ref[d])
        sp = jnp.maximum(-wl, 0.0) + jnp.log(1.0 + jnp.exp(-jnp.abs(wl)))
        lw_ref[d, 0] = -jnp.exp(-sp - 0.5)
        a = _sigmoid(a0_ref[d] + _bdot(ad[d], aup_ref[d]))
        kd = k * (1.0 + (a - 1.0) * ka_ref[...])
        kd_ref[d, 0] = kd.astype(kd_ref.dtype)
        bb_ref[d, 0] = (kn * a).astype(bb_ref.dtype)
        bonus += r * kd * rk_ref[...]
    bon_ref[0] = (_head_sums(bonus, hs) * v).astype(bon_ref.dtype)


def _rw_prep(pa, grp, mu, w0, wup, a0, aup, gup, k_k, k_a, r_k, n_lat, dims):
    b, t, na = pa.shape
    w, ld, la, lg = dims
    nc = t // CHUNK
    nlc = n_lat // CHUNK
    blk = lambda f: pl.BlockSpec((1, CHUNK, na), f)
    full = lambda a: pl.BlockSpec(a.shape, lambda bi, i: (0,) * a.ndim)
    o1 = pl.BlockSpec((1, CHUNK, w), lambda bi, i: (bi, i, 0))
    o2 = pl.BlockSpec((2, 1, CHUNK, w), lambda bi, i: (0, bi, i, 0))
    s1 = jax.ShapeDtypeStruct((b, t, w), BF16)
    s2 = jax.ShapeDtypeStruct((2, b, t, w), BF16)
    s2f = jax.ShapeDtypeStruct((2, b, t, w), F32)
    consts = (grp, mu, w0, wup, a0, aup, gup, k_k, k_a, r_k)
    return pl.pallas_call(
        functools.partial(_rw_prep_kernel, n_lat_chunks=nlc, n_chunks=nc, w=w, ld=ld, la=la, lg=lg),
        name="rw_prep",
        grid=(b, nc),
        in_specs=[blk(lambda bi, i: (bi, jnp.maximum(i - 1, 0), 0)),
                  blk(lambda bi, i: (bi, i, 0)),
                  blk(lambda bi, i: (bi, jnp.minimum(i + 1, nc - 1), 0))] + [full(a) for a in consts],
        out_specs=[o1, o1, o1, o1, o1, o2, o2, o2],
        out_shape=[s1, s1, s1, s1, s1, s2f, s2, s2],
        compiler_params=_params(("parallel", "arbitrary"), 32 << 20),
    )(pa, pa, pa, *consts)


def _pdot(a, b):
    return lax.dot_general(a.astype(BF16), b.astype(BF16), (((2,), (1,)), ((0,), (0,))),
                           preferred_element_type=F32)


def _pdot_nt(a, b):
    return lax.dot_general(a.astype(BF16), b.astype(BF16), (((2,), (2,)), ((0,), (0,))),
                           preferred_element_type=F32)


def _pdot_tn(a, b):
    return lax.dot_general(a.astype(BF16), b.astype(BF16), (((1,), (1,)), ((0,), (0,))),
                           preferred_element_type=F32)


def _rw_scan_kernel(*refs, n_pairs):
    s_ref = refs[14]

    @pl.when(pl.program_id(1) == 0)
    def _():
        s_ref[...] = jnp.zeros_like(s_ref)

    _rw_scan_dir(0, *refs[0:6], refs[12], s_ref.at[0], n_pairs)
    _rw_scan_dir(1, *refs[6:12], refs[13], s_ref.at[1], n_pairs)


def _rw_scan_dir(d, r_ref, v_ref, kn_ref, lw_ref, kd_ref, bb_ref, y_ref, s_ref, n_pairs):
    c = CHUNK
    wfull = n_pairs * LANE
    incl_bf = jnp.where(_order_masks(d, c)[0], 1.0, 0.0).astype(BF16)
    incl2, strict2 = _order_masks(d, c, 2)
    lane_w = lax.broadcasted_iota(jnp.int32, (1, wfull), 1) % LANE
    w0 = jnp.where(lane_w < RW_HEAD, 1.0, 0.0)
    w1 = 1.0 - w0
    lane = lax.broadcasted_iota(jnp.int32, (1, 1, LANE), 2)
    m0 = jnp.where(lane < RW_HEAD, 1.0, 0.0)
    m1 = 1.0 - m0
    rowh = lax.broadcasted_iota(jnp.int32, (LANE, LANE), 0) // RW_HEAD
    colh = lax.broadcasted_iota(jnp.int32, (LANE, LANE), 1) // RW_HEAD
    blockdiag = rowh == colh
    last_row = jnp.where(d == 0, c - 1, 0)
    rsel = lax.broadcasted_iota(jnp.int32, (c, 1), 0) == last_row

    def pairs(x):
        return jnp.stack([x[:, p * LANE:(p + 1) * LANE] for p in range(n_pairs)], axis=0)

    def stack(x):
        return jnp.concatenate([x * m0, x * m1], axis=1)

    lw = lw_ref[0, 0]
    r = r_ref[0].astype(F32)
    v = v_ref[0].astype(F32)
    kd = kd_ref[0, 0].astype(F32)
    bb = bb_ref[0, 0].astype(F32)
    cw = _mask_dot(incl_bf, lw)
    tot = jnp.sum(jnp.where(rsel, cw, 0.0), axis=0, keepdims=True)
    w_inv = jnp.exp(-cw)
    w_rem = jnp.exp(tot - cw)
    a_t = -kn_ref[0].astype(F32) * jnp.exp(cw - lw)
    r_t = r * jnp.exp(cw)
    b_t = bb * w_inv
    k_t = kd * w_inv
    ar = pairs(jnp.concatenate([a_t, r_t], axis=0))
    bk = pairs(jnp.concatenate([b_t * w0, b_t * w1, k_t * w0, k_t * w1], axis=0))
    vst = pairs(jnp.concatenate([v * w0, v * w1], axis=0))
    uvr = pairs(jnp.concatenate([bb * w_rem, kd * w_rem], axis=0))
    v3 = pairs(v)
    decay = pairs(jnp.exp(tot))
    s0 = s_ref[...]

    sc = _pdot_nt(ar, bk)
    a_ab = jnp.where(strict2, sc[:, :c, :2 * c], 0.0)
    a_ak = jnp.where(strict2, sc[:, :c, 2 * c:], 0.0)
    r_b = jnp.where(incl2, sc[:, c:, :2 * c], 0.0)
    r_k = jnp.where(incl2, sc[:, c:, 2 * c:], 0.0)

    abd = stack(a_ab)
    q = _pdot(abd, abd)
    nsum = abd
    for _ in range(5):
        nsum = nsum + q + _pdot(q, nsum)
        q = _pdot(q, q)
    n_side = nsum[:, :c] + nsum[:, c:]

    art = _pdot_nt(ar, s0)
    rhs = art[:, :c] + _pdot(a_ak, vst)
    u = rhs + _pdot(n_side, stack(rhs))
    y = art[:, c:] + _pdot(jnp.concatenate([r_b, r_k], axis=2), jnp.concatenate([stack(u), vst], axis=1))
    for p in range(n_pairs):
        y_ref[0, :, p * LANE:(p + 1) * LANE] = y[p]
    upd = _pdot_tn(jnp.concatenate([u, v3], axis=1), uvr)
    s_ref[...] = s0 * decay + jnp.where(blockdiag, upd, 0.0)


def _scan_chunk(d, i, n_lat_chunks, n_chunks):
    nctx = n_chunks - n_lat_chunks
    in_ctx = i < nctx
    fwd = jnp.where(in_ctx, n_lat_chunks + i, i - nctx)
    bwd = jnp.where(in_ctx, n_chunks - 1 - i, n_chunks - 1 - i)
    return jnp.where(d == 0, fwd, bwd)


def _rw_scan(r, v, kn, lw, kd, bb, n_lat):
    b, t, w = r.shape
    nc = t // CHUNK
    nlc = n_lat // CHUNK
    npair = w // LANE
    ch = functools.partial(_scan_chunk, n_lat_chunks=nlc, n_chunks=nc)
    s1 = lambda d: pl.BlockSpec((1, CHUNK, w), lambda bi, i: (bi, ch(d, i), 0))
    s2 = lambda d: pl.BlockSpec((1, 1, CHUNK, w), lambda bi, i: (d, bi, ch(d, i), 0))
    per_dir = lambda d: [s1(d), s1(d), s1(d), s2(d), s2(d), s2(d)]
    out = jax.ShapeDtypeStruct((b, t, w), F32)
    return pl.pallas_call(
        functools.partial(_rw_scan_kernel, n_pairs=npair), name="rw_scan",
        grid=(b, nc),
        in_specs=per_dir(0) + per_dir(1),
        out_specs=[s1(0), s1(1)],
        out_shape=[out, out],
        scratch_shapes=[pltpu.VMEM((2, npair, LANE, LANE), F32)],
        compiler_params=_params(("arbitrary", "arbitrary"), 32 << 20),
    )(r, v, kn, lw, kd, bb, r, v, kn, lw, kd, bb)


def _rw_finish_kernel(yf_ref, yb_ref, bon_ref, g_ref, lg_ref, lb_ref, o_ref):
    y = yf_ref[0] + yb_ref[0]
    hs = _pair_sum_matrix()
    mean = _head_sums(y, hs) * (1.0 / RW_HEAD)
    yc = y - mean
    var = _head_sums(yc * yc, hs) * (1.0 / RW_HEAD)
    yn = yc * lax.rsqrt(var + RW_GN_EPS)
    o_ref[0] = ((yn * lg_ref[...] + lb_ref[...] + bon_ref[0].astype(F32)) * g_ref[0].astype(F32)).astype(o_ref.dtype)


def _rw_finish(y_f, y_b, bonus, g, ln_g, ln_b):
    b, t, w = y_f.shape
    tm = _pick(t, (256, 128, 64))
    s1 = pl.BlockSpec((1, tm, w), lambda bi, i: (bi, i, 0))
    vec = pl.BlockSpec((1, w), lambda bi, i: (0, 0))
    return pl.pallas_call(
        _rw_finish_kernel, name="rw_finish",
        grid=(b, t // tm),
        in_specs=[s1, s1, s1, s1, vec, vec],
        out_specs=s1,
        out_shape=jax.ShapeDtypeStruct((b, t, w), BF16),
        compiler_params=_params(("parallel", "arbitrary"), 32 << 20),
    )(y_f, y_b, bonus, g, ln_g.reshape(1, w), ln_b.reshape(1, w))


def _hg_scan_kernel(qf_ref, zf_ref, if_ref, qb_ref, zb_ref, ib_ref, lb_ref, of_ref, ob_ref, s_ref, *, n_heads):
    @pl.when(pl.program_id(1) == 0)
    def _():
        s_ref[...] = jnp.zeros_like(s_ref)

    _hg_scan_dir(0, qf_ref, zf_ref, if_ref, lb_ref, of_ref, s_ref.at[0], n_heads)
    _hg_scan_dir(1, qb_ref, zb_ref, ib_ref, lb_ref, ob_ref, s_ref.at[1], n_heads)


def _hg_scan_dir(d, q_ref, z_ref, i_ref, lb_ref, o_ref, s_ref, n_heads):
    c = CHUNK
    nb = c // SUB
    sgn = 1 - 2 * d
    fwd = d == 0
    incl_bf = jnp.where(_order_masks(d, c)[0], 1.0, 0.0).astype(BF16)
    row = lax.broadcasted_iota(jnp.int32, (c, c), 0)
    col = lax.broadcasted_iota(jnp.int32, (c, c), 1)
    rb, cbk = row // SUB, col // SUB
    lvl_b = (rb - cbk) * sgn > 0
    lvl_1 = jnp.logical_and(rb == cbk, (row - col) * sgn >= 0)
    last_row = jnp.where(d == 0, c - 1, 0)
    rsel = lax.broadcasted_iota(jnp.int32, (c, 1), 0) == last_row

    def heads(x):
        return jnp.stack([x[:, h * LANE:(h + 1) * LANE] for h in range(n_heads)], axis=0)

    q = q_ref[0]
    z = z_ref[0]
    val = i_ref[0]
    lg_l = lb_ref[d, 0:1]
    lg_1ml = lb_ref[d, 1:2]
    one_ml = lb_ref[d, 2:3]
    s0 = s_ref[...]

    ez = jnp.exp(-jnp.abs(z))
    log_sig = jnp.minimum(z, 0.0) - jnp.log(1.0 + ez)
    x2 = lg_1ml + log_sig
    mx = jnp.maximum(lg_l, x2)
    log_f = mx + jnp.log(jnp.exp(lg_l - mx) + jnp.exp(x2 - mx))
    k = one_ml * jnp.where(z >= 0.0, ez, 1.0) / (1.0 + ez)

    cb = _mask_dot(incl_bf, log_f)
    tot = jnp.sum(jnp.where(rsel, cb, 0.0), axis=0, keepdims=True)
    wfull = cb.shape[1]
    cb3 = cb.reshape(nb, SUB, wfull)
    blk_end = cb3[:, SUB - 1:SUB] if fwd else cb3[:, 0:1]

    def q_refd(refs):
        return heads(jnp.concatenate([q * jnp.exp(jnp.minimum(cb - e, 0.0)) for e in refs], axis=0))

    end_own = jnp.broadcast_to(blk_end, (nb, SUB, wfull)).reshape(c, wfull)
    s_b = _pdot_nt(q_refd([blk_end[j] for j in range(nb)]), heads(k * jnp.exp(end_own - cb)))
    s_1 = _pdot_nt(q_refd([jnp.broadcast_to(cb3[:, j:j + 1], (nb, SUB, wfull)).reshape(c, wfull)
                           for j in range(SUB)]), heads(k))
    att = jnp.zeros((n_heads, c, c), F32)
    for j in range(nb):
        att += jnp.where(jnp.logical_and(lvl_b, cbk == j), s_b[:, j * c:(j + 1) * c], 0.0)
    for j in range(SUB):
        att += jnp.where(jnp.logical_and(lvl_1, col % SUB == j), s_1[:, j * c:(j + 1) * c], 0.0)
    v3 = heads(val)
    o = _pdot(att, v3) + _pdot_nt(heads(q * jnp.exp(cb)), s0)
    for h in range(n_heads):
        o_ref[0, :, h * LANE:(h + 1) * LANE] = o[h]
    s_ref[...] = s0 * heads(jnp.exp(tot)) + _pdot_tn(v3, heads(k * jnp.exp(tot - cb)))


def _hg_scan(phg, lbt, n_lat, w):
    b, t, _ = phg.shape
    nc = t // CHUNK
    nlc = n_lat // CHUNK
    nh = w // LANE
    ch = functools.partial(_scan_chunk, n_lat_chunks=nlc, n_chunks=nc)
    col = lambda d, cb_: pl.BlockSpec((1, CHUNK, w), lambda bi, i: (bi, ch(d, i), cb_))
    out = jax.ShapeDtypeStruct((b, t, w), F32)
    return pl.pallas_call(
        functools.partial(_hg_scan_kernel, n_heads=nh), name="hg_scan",
        grid=(b, nc),
        in_specs=[col(0, 0), col(0, 1), col(0, 3), col(1, 0), col(1, 2), col(1, 3),
                  pl.BlockSpec((2, 3, w), lambda bi, i: (0, 0, 0))],
        out_specs=[col(0, 0), col(1, 0)],
        out_shape=[out, out],
        scratch_shapes=[pltpu.VMEM((2, nh, LANE, LANE), F32)],
        compiler_params=_params(("arbitrary", "arbitrary"), 32 << 20),
    )(phg, phg, phg, phg, phg, phg, lbt)


def _hg_finish_kernel(of_ref, ob_ref, g_ref, ng_ref, out_ref, *, n_heads):
    for h in range(n_heads):
        sl = slice(h * LANE, (h + 1) * LANE)
        o = of_ref[0, :, sl] + ob_ref[0, :, sl]
        y = o * lax.rsqrt(jnp.mean(o * o, axis=-1, keepdims=True) + NORM_EPS) * ng_ref[...]
        g = g_ref[0, :, sl]
        out_ref[0, :, sl] = (y * g * _sigmoid(g)).astype(out_ref.dtype)


def _hg_finish(o_f, o_b, phg, norm_g, w):
    b, t, _ = o_f.shape
    tm = _pick(t, (256, 128, 64))
    return pl.pallas_call(
        functools.partial(_hg_finish_kernel, n_heads=w // LANE), name="hg_finish",
        grid=(b, t // tm),
        in_specs=[pl.BlockSpec((1, tm, w), lambda bi, i: (bi, i, 0)),
                  pl.BlockSpec((1, tm, w), lambda bi, i: (bi, i, 0)),
                  pl.BlockSpec((1, tm, w), lambda bi, i: (bi, i, 4)),
                  pl.BlockSpec((1, LANE), lambda bi, i: (0, 0))],
        out_specs=pl.BlockSpec((1, tm, w), lambda bi, i: (bi, i, 0)),
        out_shape=jax.ShapeDtypeStruct((b, t, w), BF16),
        compiler_params=_params(("parallel", "arbitrary"), 32 << 20),
    )(o_f, o_b, phg, norm_g.reshape(1, LANE))


def _hy_pre_kernel(prev_ref, cur_ref, next_ref, cw_ref, cb_ref, x0_ref, z_ref, *, n_lat_blocks, n_blocks, w):
    i = pl.program_id(1)
    tm = cur_ref.shape[1]
    hb = prev_ref.shape[1]
    cur = cur_ref[0].astype(F32)
    first = jnp.logical_or(i == 0, i == n_lat_blocks)
    last = jnp.logical_or(i == n_lat_blocks - 1, i == n_blocks - 1)
    row = lax.broadcasted_iota(jnp.int32, (tm, 1), 0)
    p_last = prev_ref[0, hb - 1:hb].astype(F32) * jnp.where(first, 0.0, 1.0)
    n_first = next_ref[0, 0:1].astype(F32) * jnp.where(last, 0.0, 1.0)
    before = jnp.where(row == 0, p_last, pltpu.roll(cur, 1, 0))
    after = jnp.where(row == tm - 1, n_first, pltpu.roll(cur, tm - 1, 0))
    u = cw_ref[0:1] * before + cw_ref[1:2] * cur + cw_ref[2:3] * after + cb_ref[...]
    x0_ref[0] = u[:, 0:w].astype(x0_ref.dtype)
    z_ref[0] = (u[:, w:2 * w] * u[:, 2 * w:3 * w]).astype(z_ref.dtype)


def _hy_pre(phy, conv_w, conv_b, n_lat):
    b, t, w3 = phy.shape
    w = w3 // 3
    tm = _pick(math.gcd(n_lat, t - n_lat), (256, 128, 64))
    nb = t // tm
    hb = 16
    per = tm // hb
    o = pl.BlockSpec((1, tm, w), lambda bi, i: (bi, i, 0))
    s = jax.ShapeDtypeStruct((b, t, w), BF16)
    return pl.pallas_call(
        functools.partial(_hy_pre_kernel, n_lat_blocks=n_lat // tm, n_blocks=nb, w=w), name="hy_pre",
        grid=(b, nb),
        in_specs=[pl.BlockSpec((1, hb, w3), lambda bi, i: (bi, jnp.maximum(i * per - 1, 0), 0)),
                  pl.BlockSpec((1, tm, w3), lambda bi, i: (bi, i, 0)),
                  pl.BlockSpec((1, hb, w3), lambda bi, i: (bi, jnp.minimum((i + 1) * per, t // hb - 1), 0)),
                  pl.BlockSpec((3, w3), lambda bi, i: (0, 0)),
                  pl.BlockSpec((1, w3), lambda bi, i: (0, 0))],
        out_specs=[o, o],
        out_shape=[s, s],
        compiler_params=_params(("parallel", "arbitrary"), 40 << 20),
    )(phy, phy, phy, conv_w, conv_b.reshape(1, w3))


def _hdot(a, b):
    return jnp.dot(a, b, precision=lax.Precision.HIGHEST, preferred_element_type=F32)


def _hy_taps_kernel(ft_ref, w1_ref, b1_ref, w2_ref, b2_ref, w3f_ref, w3b_ref, fr_ref, dl_ref,
                    tap_ref, sum_ref):
    i = pl.program_id(1)
    ft = ft_ref[...]
    fr = fr_ref[...]
    h = jnp.sin(fr * (_hdot(ft, w1_ref[...]) + b1_ref[...]))
    h = jnp.sin(fr * (_hdot(h, w2_ref[...]) + b2_ref[...]))
    cf = ft[:, HY_EMB:HY_EMB + 1]
    cb = ft[:, HY_EMB + 1:HY_EMB + 2]
    tt = ft[:, 0:1]
    tap = (cf * _hdot(h, w3f_ref[...]) + cb * _hdot(h, w3b_ref[...])) * jnp.exp(-tt * dl_ref[...])
    tap_ref[...] = tap

    @pl.when(i == 0)
    def _():
        sum_ref[...] = jnp.zeros_like(sum_ref)

    sum_ref[...] += jnp.sum(jnp.abs(tap), axis=0, keepdims=True)


def _hy_taps(feats, w1p, b1, w2, b2, w3, freq, deltas):
    rws = feats.shape[0]
    hid = w2.shape[0]
    w = w3.shape[1] // 2
    tr = _pick(rws, (1024, 512, 256, 128))
    ct = w
    nj = w // ct
    c2 = lambda a: pl.BlockSpec(a.shape, lambda j, i: (0, 0))
    return pl.pallas_call(
        _hy_taps_kernel, name="hy_taps",
        grid=(nj, rws // tr),
        in_specs=[pl.BlockSpec((tr, LANE), lambda j, i: (i, 0)),
                  c2(w1p), pl.BlockSpec((1, hid), lambda j, i: (0, 0)),
                  c2(w2), pl.BlockSpec((1, hid), lambda j, i: (0, 0)),
                  pl.BlockSpec((hid, ct), lambda j, i: (0, j)),
                  pl.BlockSpec((hid, ct), lambda j, i: (0, nj + j)),
                  pl.BlockSpec((1, hid), lambda j, i: (0, 0)),
                  pl.BlockSpec((1, ct), lambda j, i: (0, j))],
        out_specs=[pl.BlockSpec((tr, ct), lambda j, i: (i, j)),
                   pl.BlockSpec((1, ct), lambda j, i: (0, j))],
        out_shape=[jax.ShapeDtypeStruct((rws, w), F32), jax.ShapeDtypeStruct((1, w), F32)],
        compiler_params=_params(("parallel", "arbitrary"), 32 << 20),
    )(feats, w1p, b1.reshape(1, hid), w2, b2.reshape(1, hid), w3, w3, freq.reshape(1, hid),
      deltas.reshape(1, w))


def _dft1_kernel(g_ref, x_ref, o_ref):
    o_ref[...] = lax.dot_general(g_ref[...], x_ref[...].astype(BF16), (((2,), (1,)), ((0,), (0,))),
                                 preferred_element_type=F32).astype(o_ref.dtype)


def _dft1(g, x):
    nb, m2, k = g.shape
    c = x.shape[2]
    bt = _pick(nb, (8, 4, 2))
    ct = _pick(c, (256, 128))
    return pl.pallas_call(
        _dft1_kernel, name="dft1",
        grid=(c // ct, nb // bt),
        in_specs=[pl.BlockSpec((bt, m2, k), lambda j, i: (i, 0, 0)),
                  pl.BlockSpec((bt, k, ct), lambda j, i: (i, 0, j))],
        out_specs=pl.BlockSpec((bt, m2, ct), lambda j, i: (i, 0, j)),
        out_shape=jax.ShapeDtypeStruct((nb, m2, c), BF16),
        compiler_params=_params(("parallel", "arbitrary"), 32 << 20),
    )(g, x)


def _spec_kernel(a_ref, f2_ref, sc_ref, o_ref):
    for l in range(a_ref.shape[0]):
        o_ref[l] = jnp.dot(f2_ref[...], a_ref[l].astype(BF16), preferred_element_type=F32) * sc_ref[...]


def _spec(a, f2, scale):
    na, m2, c = a.shape
    kt = _pick(na, (8, 4, 2))
    ct = _pick(c, (256, 128))
    return pl.pallas_call(
        _spec_kernel, name="hy_spec",
        grid=(c // ct, na // kt),
        in_specs=[pl.BlockSpec((kt, m2, ct), lambda j, i: (i, 0, j)),
                  pl.BlockSpec((m2, m2), lambda j, i: (0, 0)),
                  pl.BlockSpec((1, ct), lambda j, i: (0, j))],
        out_specs=pl.BlockSpec((kt, m2, ct), lambda j, i: (i, 0, j)),
        out_shape=jax.ShapeDtypeStruct((na, m2, c), F32),
        compiler_params=_params(("parallel", "arbitrary"), 32 << 20),
    )(a, f2, scale)


def _conv_mid_kernel(a_ref, f2_ref, h_ref, g3_ref, o_ref):
    nbh = a_ref.shape[1] // 2
    for l in range(a_ref.shape[0]):
        x = jnp.dot(f2_ref[...], a_ref[l].astype(BF16), preferred_element_type=F32)
        xr, xi = x[:nbh], x[nbh:]
        hr, hi = h_ref[l, :nbh], h_ref[l, nbh:]
        y = jnp.concatenate([xr * hr - xi * hi, xr * hi + xi * hr], axis=0)
        o_ref[l] = jnp.dot(g3_ref[l], y.astype(BF16), preferred_element_type=F32).astype(o_ref.dtype)


def _conv_mid(a, f2, h, g3):
    na, m2, c = a.shape
    kt = _pick(na, (8, 4, 2))
    ct = _pick(c, (256, 128))
    slab = pl.BlockSpec((kt, m2, ct), lambda j, i: (i, 0, j))
    return pl.pallas_call(
        _conv_mid_kernel, name="conv_mid",
        grid=(c // ct, na // kt),
        in_specs=[slab, pl.BlockSpec((m2, m2), lambda j, i: (0, 0)), slab,
                  pl.BlockSpec((kt, m2, m2), lambda j, i: (i, 0, 0))],
        out_specs=slab,
        out_shape=jax.ShapeDtypeStruct((na, m2, c), BF16),
        compiler_params=_params(("parallel", "arbitrary"), 32 << 20),
    )(a, f2, h, g3)


def _conv_out_kernel(b_ref, f4_ref, o_ref, *, scale):
    for l in range(b_ref.shape[0]):
        o_ref[l] = jnp.dot(f4_ref[...], b_ref[l].astype(BF16), preferred_element_type=F32) * scale


def _conv_out(bm, f4, scale):
    nb, m2, c = bm.shape
    mo = f4.shape[0]
    pt = _pick(nb, (8, 4, 2))
    ct = _pick(c, (256, 128))
    return pl.pallas_call(
        functools.partial(_conv_out_kernel, scale=scale), name="conv_out",
        grid=(c // ct, nb // pt),
        in_specs=[pl.BlockSpec((pt, m2, ct), lambda j, i: (i, 0, j)),
                  pl.BlockSpec((mo, m2), lambda j, i: (0, 0))],
        out_specs=pl.BlockSpec((pt, mo, ct), lambda j, i: (i, 0, j)),
        out_shape=jax.ShapeDtypeStruct((nb, mo, c), F32),
        compiler_params=_params(("parallel", "arbitrary"), 32 << 20),
    )(bm, f4)


def _ctx_conv_kernel(z_ref, ext_ref, sc_ref, o_ref):
    n = z_ref.shape[1]

    def body(s, acc):
        return acc + ext_ref[pl.ds(n - s, n), :] * z_ref[0, pl.ds(s, 1), :]

    acc = lax.fori_loop(0, n, body, jnp.zeros(o_ref.shape[1:], F32))
    o_ref[0] = acc * sc_ref[...]


def _ctx_conv(z, ext, scale):
    b, n, w = z.shape
    ct = LANE
    return pl.pallas_call(
        _ctx_conv_kernel, name="ctx_conv",
        grid=(b, w // ct),
        in_specs=[pl.BlockSpec((1, n, ct), lambda bi, j: (bi, 0, j)),
                  pl.BlockSpec((2 * n, ct), lambda bi, j: (0, j)),
                  pl.BlockSpec((1, ct), lambda bi, j: (0, j))],
        out_specs=pl.BlockSpec((1, n, ct), lambda bi, j: (bi, 0, j)),
        out_shape=jax.ShapeDtypeStruct((b, n, w), F32),
        compiler_params=_params(("parallel", "arbitrary"), 16 << 20),
    )(z, ext, scale)


def _hy_post_kernel(y_ref, z_ref, x0_ref, bias_ref, o_ref):
    o_ref[0] = (x0_ref[0].astype(F32) * (y_ref[0] + bias_ref[...] * z_ref[0].astype(F32))).astype(o_ref.dtype)


def _hy_post(y, z, x0, bias):
    b, t, w = y.shape
    tm = _pick(t, _TM_ROWS)
    s = pl.BlockSpec((1, tm, w), lambda bi, i: (bi, i, 0))
    return pl.pallas_call(
        _hy_post_kernel, name="hy_post",
        grid=(b, t // tm),
        in_specs=[s, s, s, pl.BlockSpec((1, w), lambda bi, i: (0, 0))],
        out_specs=s,
        out_shape=jax.ShapeDtypeStruct((b, t, w), BF16),
        compiler_params=_params(("parallel", "arbitrary"), 32 << 20),
    )(y, z, x0, bias.reshape(1, w))


def _dft_factors(n):
    m = 2 * n
    na = 1 << ((m.bit_length() - 1) // 2)
    return na, m // na


def _cis(num, den, sign):
    ang = (num % den).astype(F32) * (2.0 * math.pi / den)
    return jnp.cos(ang), sign * jnp.sin(ang)


def _cblock(cr, ci):
    return jnp.concatenate([jnp.concatenate([cr, -ci], axis=-1),
                            jnp.concatenate([ci, cr], axis=-1)], axis=-2)


def _dft_tables(n):
    na, nb = _dft_factors(n)
    m = na * nb
    ah = na // 2
    ar = lambda k: jnp.arange(k, dtype=jnp.int32)
    ka, bb = ar(na)[None, :, None], ar(nb)[:, None, None]
    g1c = _cblock(*_cis(ka * (nb * ar(ah)[None, None, :] + bb), m, -1.0))
    g1t = jnp.concatenate(_cis(ka * (nb * ar(na)[None, None, :] + bb), m, -1.0), axis=-2)
    f2 = _cblock(*_cis(ar(nb)[:, None] * ar(nb)[None, :], nb, -1.0))
    g3 = _cblock(*_cis(ar(nb)[None, :, None] * (ar(na)[:, None, None] + na * ar(nb)[None, None, :]), m, 1.0))
    f4 = _cblock(*_cis(ar(ah)[:, None] * ar(na)[None, :], na, 1.0))
    return tuple(t.astype(BF16) for t in (g1c, g1t, f2, g3, f4))


def _filter_feats(n, order):
    bands_n = (HY_EMB - 1) // 2
    t = jnp.linspace(0.0, 1.0, n, dtype=F32)[:, None]
    lag = jnp.arange(n, dtype=F32)[:, None]
    bands = jnp.linspace(1e-4, bands_n - 1, bands_n, dtype=F32)[None, :]
    ang = 2.0 * math.pi * lag * bands / n
    zf = jnp.concatenate([t, jnp.cos(ang), -jnp.sin(ang)], axis=-1)
    if order == "dft":
        na, nb = _dft_factors(n)
        m = (np.arange(na)[None, :] * nb + np.arange(nb)[:, None]).reshape(-1)
    else:
        m = (np.arange(2 * n) - n) % (2 * n)
    lag_of = np.where(m < n, m, np.where(m == n, 0, 2 * n - m))
    cf = np.where(m == 0, 0.5, np.where(m < n, 1.0, 0.0)).astype(np.float32)
    cb = np.where(m == 0, 0.5, np.where(m > n, 1.0, 0.0)).astype(np.float32)
    pad = jnp.zeros((2 * n, LANE - HY_EMB - 2), F32)
    return jnp.concatenate([zf[lag_of], jnp.asarray(cf)[:, None], jnp.asarray(cb)[:, None], pad], axis=-1)


def _hy_deltas(w):
    return jnp.abs(jnp.linspace(math.log(HY_DECAY_TARGET) / HY_SLOW_PCT,
                                math.log(HY_DECAY_TARGET) / HY_FAST_PCT, w, dtype=F32))


def _long_conv_latent(z, taps_perm, inv_l1):
    b, n, w = z.shape
    na, nb = _dft_factors(n)
    ah = na // 2
    g1c, g1t, f2, g3, f4 = _dft_tables(n)
    h = _spec(_dft1(g1t, taps_perm.reshape(nb, na, w)).reshape(nb, 2, na, w).transpose(2, 1, 0, 3)
              .reshape(na, 2 * nb, w), f2, inv_l1)
    outs = []
    for pair in range(b // 2):
        zz = z[2 * pair:2 * pair + 2].astype(BF16).reshape(2, ah, nb, w).transpose(2, 0, 1, 3).reshape(nb, na, w)
        a = _dft1(g1c, zz).reshape(nb, 2, na, w).transpose(2, 1, 0, 3).reshape(na, 2 * nb, w)
        bm = _conv_mid(a, f2, h, g3).reshape(na, 2, nb, w).transpose(2, 1, 0, 3).reshape(nb, 2 * na, w)
        y = _conv_out(bm, f4, 1.0 / (na * nb))
        outs.append(y.reshape(nb, 2, ah, w).transpose(1, 2, 0, 3).reshape(2, n, w))
    return jnp.concatenate(outs, axis=0) if len(outs) > 1 else outs[0]


def _pack_in_proj(w_in, rw_mu, dims):
    depth, d_model, _ = w_in.shape
    w, dl, al, gl = dims
    ld, la, lg = _rup(dl, LANE), _rup(al, LANE), _rup(gl, LANE)
    rw_cols = 3 * w + 2 * dl + 2 * al + gl
    tile = 1024 if d_model >= 2048 else LANE
    na = _rup(3 * w + 2 * ld + 2 * la + lg, tile)
    pieces = [(0, 3 * w, 0), (3 * w, dl, 3 * w), (3 * w + dl, dl, 3 * w + ld),
              (3 * w + 2 * dl, al, 3 * w + 2 * ld), (3 * w + 2 * dl + al, al, 3 * w + 2 * ld + la),
              (3 * w + 2 * dl + 2 * al, gl, 3 * w + 2 * ld + 2 * la)]
    cols, mus = [], []
    orig = np.full((na,), rw_cols - 1, np.int64)
    pos = 0
    for src, width, dst in pieces + [(rw_cols, 0, na)]:
        if dst > pos:
            cols.append(jnp.zeros((depth, d_model, dst - pos), BF16))
            mus.append(jnp.zeros((depth, dst - pos), F32))
        cols.append(w_in[:, :, src:src + width].astype(BF16))
        mus.append(rw_mu[:, src:src + width])
        orig[dst:dst + width] = np.arange(src, src + width)
        pos = dst + width
    offs = []
    src = rw_cols
    for width in (5 * w, 3 * w, w_in.shape[2] - rw_cols - 8 * w):
        offs.append(pos // tile)
        cols.append(w_in[:, :, src:src + width].astype(BF16))
        pad = _rup(width, tile) - width
        if pad:
            cols.append(jnp.zeros((depth, d_model, pad), BF16))
        src += width
        pos += width + pad
    wt = _tile_w(jnp.concatenate(cols, axis=2), tile)
    mu = jnp.concatenate(mus, axis=1).reshape(depth, 1, na)
    grp = np.stack([orig // (rw_cols // 4), orig // (rw_cols // 2)]).astype(np.int32)
    grp = np.minimum(grp, np.array([[3], [1]])).astype(np.int32)
    return wt, tile, offs, mu, jnp.asarray(grp), (w, ld, la, lg), na


def _pad_rows(a, rows):
    return jnp.pad(a, [(0, 0)] * (a.ndim - 2) + [(0, rows - a.shape[-2]), (0, 0)])


def kernel(x, c, ctx, c_ctx, ada_w, ada_b, norm1_g, norm2_g, w_in, rw_mu, rw_w0, rw_w_up, rw_a0, rw_a_up, rw_g_up, rw_k_k, rw_k_a, rw_r_k, rw_ln_g, rw_ln_b, hg_lower_bounds, hg_norm_g, hy_conv_w, hy_conv_b, hy_f_w1, hy_f_b1, hy_f_w2, hy_f_b2, hy_f_w3, hy_freq, hy_bias, w_branch_a, w_branch_b, w_branch_c, w_out, ffn_w_gate, ffn_w_up, ffn_w_down, final_norm_g):
    bsz, n_lat, d = x.shape
    n_ctx = ctx.shape[1]
    depth = w_in.shape[0]
    w = rw_k_k.shape[1]
    dims = (w, rw_w_up.shape[2], rw_a_up.shape[2], rw_g_up.shape[1])
    assert bsz % 2 == 0 and n_lat % GRID_W == 0 and n_ctx % CHUNK == 0 and w % LANE == 0
    rw_cols = 3 * w + 2 * dims[1] + 2 * dims[2] + dims[3]
    hg_end = rw_cols + 5 * w
    hy_end = hg_end + 3 * w

    xs = jnp.concatenate([x, ctx], axis=1)
    cvec = jnp.zeros((_rup(bsz + 1, 8), d), F32).at[:bsz].set(c).at[bsz].set(c_ctx)
    lb_cum = jnp.cumsum(jax.nn.softmax(hg_lower_bounds.astype(F32), axis=0), axis=0)
    deltas = _hy_deltas(w)

    wt_in, tile, offs, mu_all, grp, pdims, na = _pack_in_proj(w_in, rw_mu, dims)
    assert (5 * w) % tile == 0 and (3 * w) % tile == 0 and (3 * d) % tile == 0
    _, ld, la, lg = pdims
    hid = ffn_w_gate.shape[2]
    hp = _rup(hid, 1024) if hid > 1024 else _rup(hid, LANE)
    wt_g = _tile_w(jnp.pad(ffn_w_gate.astype(BF16), ((0, 0), (0, 0), (0, hp - hid))), _tn_ffn(hp))
    wt_u = _tile_w(jnp.pad(ffn_w_up.astype(BF16), ((0, 0), (0, 0), (0, hp - hid))), _tn_ffn(hp))
    wt_d = _tile_w(jnp.pad(ffn_w_down.astype(BF16), ((0, 0), (0, hp - hid), (0, 0))), _tn_wide(d), _tk_of(hp))
    wt_o = _tile_w(w_out, _tn_wide(d), _tk_of(d))
    wt_a, wt_b, wt_c = (_tile_w(t, _tn_wide(d)) for t in (w_branch_a, w_branch_b, w_branch_c))

    for layer in range(depth):
        need_ctx = layer < depth - 1
        mod = _ada(cvec, ada_w, ada_b, layer)[:bsz + 1].reshape(bsz + 1, 1, 6, d)
        mods = [mod[:, :, s, :] for s in range(6)]

        h1 = _modnorm(xs, norm1_g[layer], mods[0], mods[1], n_lat)
        pa = _matmul(h1, wt_in, layer, 0, na // tile, BF16)
        phg = _matmul(h1, wt_in, layer, offs[0], 5 * w // tile, F32)
        rows = None if need_ctx else n_lat
        phy = _matmul(h1, wt_in, layer, offs[1], 3 * w // tile, BF16, rows)
        pgate = _matmul(h1, wt_in, layer, offs[2], 3 * d // tile, BF16, rows)

        r, v, kn, g, bonus, lw, kd, bb = _rw_prep(
            pa, grp, mu_all[layer], rw_w0[layer].reshape(2, 1, w), _pad_rows(rw_w_up[layer], ld),
            rw_a0[layer].reshape(2, 1, w), _pad_rows(rw_a_up[layer], la), _pad_rows(rw_g_up[layer], lg),
            rw_k_k[layer].reshape(1, w), rw_k_a[layer].reshape(1, w), rw_r_k[layer].reshape(1, w),
            n_lat, pdims)
        y_f, y_b = _rw_scan(r, v, kn, lw, kd, bb, n_lat)
        oa = _rw_finish(y_f, y_b, bonus, g, rw_ln_g[layer], rw_ln_b[layer])

        lb = lb_cum[layer] - lb_cum[0]
        lbt = jnp.stack([jnp.log(lb), jnp.log1p(-lb), 1.0 - lb], axis=1)
        o_f, o_b = _hg_scan(phg, lbt, n_lat, w)
        ob = _hg_finish(o_f, o_b, phg, hg_norm_g[layer], w)

        x0, z = _hy_pre(phy, hy_conv_w[layer], hy_conv_b[layer], n_lat)
        w1p = _pad_rows(hy_f_w1[layer], LANE)
        filt = (w1p, hy_f_b1[layer], hy_f_w2[layer], hy_f_b2[layer], hy_f_w3[layer], hy_freq[layer], deltas)
        taps, l1 = _hy_taps(_filter_feats(n_lat, "dft"), *filt)
        y_lat = _long_conv_latent(z[:, :n_lat], taps, 1.0 / l1)
        if need_ctx:
            ext, l1c = _hy_taps(_filter_feats(n_ctx, "lag"), *filt)
            y_ctx = _ctx_conv(z[:, n_lat:].astype(F32), ext, 1.0 / l1c)
            y_lat = jnp.concatenate([y_lat, y_ctx], axis=1)
        oc = _hy_post(y_lat, z, x0, hy_bias[layer])

        ym = _merge(oa, ob, oc, wt_a, wt_b, wt_c, layer, pgate)
        xs = _matmul_resid(ym, wt_o, layer, xs, mods[2], n_lat)

        h2 = _modnorm(xs, norm2_g[layer], mods[3], mods[4], n_lat)
        act = _swiglu_up(h2, wt_g, wt_u, layer)
        xs = _matmul_resid(act, wt_d, layer, xs, mods[5], n_lat)

    return _final_norm(xs, final_norm_g, n_lat)
```

```python
import functools
import math

import numpy as np
import jax
import jax.numpy as jnp
from jax import lax
from jax.experimental import pallas as pl
from jax.experimental.pallas import tpu as pltpu

F32 = jnp.float32
BF16 = jnp.bfloat16

GRID_W = 64
CHUNK = 64
SUB = 8
NORM_EPS = 1e-6
RW_HEAD = 64
RW_GN_EPS = 64e-5
HG_EXPAND = 128
HY_EMB = 33
HY_DECAY_TARGET = 1e-2
HY_FAST_PCT = 0.3
HY_SLOW_PCT = 1.5
LANE = 128
VMEM_CAP = 56 * 1024 * 1024
VMEM_SLACK = 8 * 1024 * 1024


def _params(sem, vmem_bytes):
    return pltpu.CompilerParams(dimension_semantics=sem,
                                vmem_limit_bytes=int(min(max(vmem_bytes + VMEM_SLACK, 16 << 20), VMEM_CAP)))


def _pick(n, cands):
    for c in cands:
        if n % c == 0:
            return c
    return n


def _rup(n, m):
    return -(-n // m) * m


def _bdot(a, b):
    return jnp.dot(a.astype(BF16), b.astype(BF16), preferred_element_type=F32)


def _bdot_nt(a, b):
    return lax.dot_general(a.astype(BF16), b.astype(BF16), (((1,), (1,)), ((), ())),
                           preferred_element_type=F32)


def _bdot_tn(a, b):
    return lax.dot_general(a.astype(BF16), b.astype(BF16), (((0,), (0,)), ((), ())),
                           preferred_element_type=F32)


def _split(x):
    hi = x.astype(BF16)
    lo = (x - hi.astype(F32)).astype(BF16)
    return hi, lo


def _mask_dot(m, x):
    hi, lo = _split(x)
    return (jnp.dot(m, hi, preferred_element_type=F32) + jnp.dot(m, lo, preferred_element_type=F32))


def _x_mask_dot(x, m):
    hi, lo = _split(x)
    return (jnp.dot(hi, m, preferred_element_type=F32) + jnp.dot(lo, m, preferred_element_type=F32))


def _sigmoid(x):
    return 1.0 / (1.0 + jnp.exp(-x))


def _order_masks(d, n, reps=1):
    row = lax.broadcasted_iota(jnp.int32, (n, reps * n), 0)
    col = lax.broadcasted_iota(jnp.int32, (n, reps * n), 1) % n
    diff = (row - col) * (1 - 2 * d)
    return diff >= 0, diff > 0


def _ada_kernel(c_ref, w_ref, b_ref, o_ref):
    c = c_ref[...]
    o_ref[...] = _bdot(c * _sigmoid(c), w_ref[...]) + b_ref[...]


def _ada(cvec, w, b, layer):
    rows, d = cvec.shape
    n = w.shape[2]
    tn = _pick(n, (1024, 512, 256, 128))
    return pl.pallas_call(
        _ada_kernel, name="ada",
        grid=(n // tn,),
        in_specs=[pl.BlockSpec((rows, d), lambda j: (0, 0)),
                  pl.BlockSpec((None, d, tn), lambda j: (layer, 0, j)),
                  pl.BlockSpec((None, 1, tn), lambda j: (layer, 0, j))],
        out_specs=pl.BlockSpec((rows, tn), lambda j: (0, j)),
        out_shape=jax.ShapeDtypeStruct((rows, n), F32),
        compiler_params=_params(("arbitrary",), 3 * d * tn * 4),
    )(cvec, w, b.reshape(b.shape[0], 1, n))


def _modnorm_kernel(x_ref, g_ref, shl_ref, scl_ref, shc_ref, scc_ref, o_ref, *, n_lat):
    tm = x_ref.shape[1]
    x = x_ref[0]
    y = x * lax.rsqrt(jnp.mean(x * x, axis=-1, keepdims=True) + NORM_EPS) * g_ref[...]
    pos = pl.program_id(1) * tm + lax.broadcasted_iota(jnp.int32, (tm, 1), 0)
    is_ctx = pos >= n_lat
    sc = jnp.where(is_ctx, scc_ref[0], scl_ref[0])
    sh = jnp.where(is_ctx, shc_ref[0], shl_ref[0])
    o_ref[0] = (y * (1.0 + sc) + sh).astype(o_ref.dtype)


def _modnorm(x, gain, shift, scale, n_lat):
    b, t, d = x.shape
    tm = _pick(t, (256, 128, 64))
    vec = lambda f: pl.BlockSpec((1, 1, d), f)
    return pl.pallas_call(
        functools.partial(_modnorm_kernel, n_lat=n_lat), name="modnorm",
        grid=(b, t // tm),
        in_specs=[pl.BlockSpec((1, tm, d), lambda bi, i: (bi, i, 0)),
                  pl.BlockSpec((1, d), lambda bi, i: (0, 0)),
                  vec(lambda bi, i: (bi, 0, 0)), vec(lambda bi, i: (bi, 0, 0)),
                  vec(lambda bi, i: (b, 0, 0)), vec(lambda bi, i: (b, 0, 0))],
        out_specs=pl.BlockSpec((1, tm, d), lambda bi, i: (bi, i, 0)),
        out_shape=jax.ShapeDtypeStruct((b, t, d), BF16),
        compiler_params=_params(("parallel", "arbitrary"), 6 * tm * d * 4),
    )(x, gain.reshape(1, d), shift, scale, shift, scale)


def _rmsnorm_kernel(x_ref, g_ref, o_ref):
    x = x_ref[0]
    o_ref[0] = x * lax.rsqrt(jnp.mean(x * x, axis=-1, keepdims=True) + NORM_EPS) * g_ref[...]


def _final_norm(x, gain, n_lat):
    b, _, d = x.shape
    tm = _pick(n_lat, (256, 128, 64))
    return pl.pallas_call(
        _rmsnorm_kernel, name="final_norm",
        grid=(b, n_lat // tm),
        in_specs=[pl.BlockSpec((1, tm, d), lambda bi, i: (bi, i, 0)),
                  pl.BlockSpec((1, d), lambda bi, i: (0, 0))],
        out_specs=pl.BlockSpec((1, tm, d), lambda bi, i: (bi, i, 0)),
        out_shape=jax.ShapeDtypeStruct((b, n_lat, d), F32),
        compiler_params=_params(("parallel", "arbitrary"), 6 * tm * d * 4),
    )(x, gain.reshape(1, d))


def _mm_kernel(x_ref, w_ref, o_ref):
    o_ref[0] = jnp.dot(x_ref[0], w_ref[...], preferred_element_type=F32).astype(o_ref.dtype)


def _tn_wide(n):
    return _pick(n, (1024, 512, 256, 128))


def _tn_ffn(n):
    return _pick(n, (512, 256, 128))


def _tk_of(k):
    return k if k <= 2048 else _pick(k, (2816, 2048, 1024, 512, 256, 128))


class _TiledW:
    def __init__(self, w, tn, tk=None):
        self.w = w.astype(BF16)
        self.tk = w.shape[1] if tk is None else tk
        self.tn = tn
        self.n = w.shape[2]


def _tile_w(w, tn, tk=None):
    return _TiledW(w, tn, tk)


def _cast_pad_kernel(x_ref, o_ref, *, n_in, cols):
    i = pl.program_id(1)

    @pl.when(i < n_in)
    def _():
        o_ref[0, :, :cols] = x_ref[0].astype(o_ref.dtype)
        if cols < o_ref.shape[2]:
            o_ref[0, :, cols:] = jnp.zeros((o_ref.shape[1], o_ref.shape[2] - cols), o_ref.dtype)

    @pl.when(i >= n_in)
    def _():
        o_ref[0] = jnp.zeros(o_ref.shape[1:], o_ref.dtype)


def _cast_pad(w, rows_out, cols_out):
    l, r, c = w.shape
    tr = _pick(math.gcd(r, rows_out), (256, 128, 64, 32, 16))
    n_in = r // tr
    return pl.pallas_call(
        functools.partial(_cast_pad_kernel, n_in=n_in, cols=c), name="cast_pad",
        grid=(l, rows_out // tr),
        in_specs=[pl.BlockSpec((1, tr, c), lambda li, i: (li, jnp.minimum(i, n_in - 1), 0))],
        out_specs=pl.BlockSpec((1, tr, cols_out), lambda li, i: (li, i, 0)),
        out_shape=jax.ShapeDtypeStruct((l, rows_out, cols_out), BF16),
        compiler_params=_params(("parallel", "arbitrary"), 2 * tr * (c * 4 + cols_out * 2)),
    )(w)


def _wspec(tk, tn, f):
    def idx(*a):
        lay, j, l = f(*a)[:3]
        return lay, l, j
    return pl.BlockSpec((None, tk, tn), idx)


_TM_ROWS = (1024, 768, 512, 384, 256, 128, 64)


def _matmul(x, wt, layer, j0, nj, out_dtype, rows=None):
    b, _, k = x.shape
    t = x.shape[1] if rows is None else rows
    tn = wt.tn
    n = nj * tn
    tm = _pick(t, _TM_ROWS)
    osz = jnp.dtype(out_dtype).itemsize
    return pl.pallas_call(
        _mm_kernel, name="in_proj",
        grid=(b, t // tm, n // tn),
        in_specs=[pl.BlockSpec((1, tm, k), lambda bi, i, j: (bi, i, 0)),
                  _wspec(k, tn, lambda bi, i, j: (layer, j0 + j, 0, 0, 0))],
        out_specs=pl.BlockSpec((1, tm, tn), lambda bi, i, j: (bi, i, j)),
        out_shape=jax.ShapeDtypeStruct((b, t, n), out_dtype),
        compiler_params=_params(("parallel", "parallel", "arbitrary"),
                                2 * (tm * k * 2 + k * tn * 2 + tm * tn * osz) + tm * tn * 4),
    )(x, wt.w)


def _swiglu_kernel(x_ref, wg_ref, wu_ref, o_ref):
    x = x_ref[0]
    g = jnp.dot(x, wg_ref[...], preferred_element_type=F32)
    u = jnp.dot(x, wu_ref[...], preferred_element_type=F32)
    o_ref[0] = (g * _sigmoid(g) * u).astype(o_ref.dtype)


def _swiglu_up(x, wg, wu, layer):
    b, t, k = x.shape
    tn = wg.tn
    n = wg.n
    tm = _pick(t, _TM_ROWS)
    wsp = lambda: _wspec(k, tn, lambda bi, i, j: (layer, j, 0, 0, 0))
    return pl.pallas_call(
        _swiglu_kernel, name="swiglu_up",
        grid=(b, t // tm, n // tn),
        in_specs=[pl.BlockSpec((1, tm, k), lambda bi, i, j: (bi, i, 0)), wsp(), wsp()],
        out_specs=pl.BlockSpec((1, tm, tn), lambda bi, i, j: (bi, i, j)),
        out_shape=jax.ShapeDtypeStruct((b, t, n), BF16),
        compiler_params=_params(("parallel", "parallel", "arbitrary"),
                                2 * (tm * k * 2 + 2 * k * tn * 2 + tm * tn * 2) + 3 * tm * tn * 4),
    )(x, wg.w, wu.w)


def _resid_kernel(x_ref, w_ref, r_ref, gl_ref, gc_ref, o_ref, acc_ref, *, n_lat):
    kk = pl.program_id(3)

    @pl.when(kk == 0)
    def _():
        acc_ref[...] = jnp.zeros_like(acc_ref)

    acc_ref[...] += jnp.dot(x_ref[0], w_ref[...], preferred_element_type=F32)

    @pl.when(kk == pl.num_programs(3) - 1)
    def _():
        tm = acc_ref.shape[0]
        pos = pl.program_id(1) * tm + lax.broadcasted_iota(jnp.int32, (tm, 1), 0)
        gate = jnp.where(pos >= n_lat, gc_ref[0], gl_ref[0])
        o_ref[0] = r_ref[0] + gate * acc_ref[...]


def _matmul_resid(x, w, layer, res, gate, n_lat):
    b, t, k = x.shape
    tk, tn = w.tk, w.tn
    n = w.n
    tm = _pick(t, _TM_ROWS)
    return pl.pallas_call(
        functools.partial(_resid_kernel, n_lat=n_lat), name="proj_resid",
        grid=(b, t // tm, n // tn, k // tk),
        in_specs=[pl.BlockSpec((1, tm, tk), lambda bi, i, j, l: (bi, i, l)),
                  _wspec(tk, tn, lambda bi, i, j, l: (layer, j, l, 0, 0)),
                  pl.BlockSpec((1, tm, tn), lambda bi, i, j, l: (bi, i, j)),
                  pl.BlockSpec((1, 1, tn), lambda bi, i, j, l: (bi, 0, j)),
                  pl.BlockSpec((1, 1, tn), lambda bi, i, j, l: (b, 0, j))],
        out_specs=pl.BlockSpec((1, tm, tn), lambda bi, i, j, l: (bi, i, j)),
        out_shape=jax.ShapeDtypeStruct((b, t, n), F32),
        scratch_shapes=[pltpu.VMEM((tm, tn), F32)],
        compiler_params=_params(("parallel", "parallel", "arbitrary", "arbitrary"),
                                2 * (tm * tk * 2 + tk * tn * 2 + 2 * tm * tn * 4) + 2 * tm * tn * 4),
    )(x, w.w, res, gate, gate)


def _merge_kernel(oa_ref, ob_ref, oc_ref, wa_ref, wb_ref, wc_ref, ga_ref, gb_ref, gc_ref, o_ref):
    y = _sigmoid(ga_ref[0].astype(F32)) * jnp.dot(oa_ref[0], wa_ref[...], preferred_element_type=F32)
    y += _sigmoid(gb_ref[0].astype(F32)) * jnp.dot(ob_ref[0], wb_ref[...], preferred_element_type=F32)
    y += _sigmoid(gc_ref[0].astype(F32)) * jnp.dot(oc_ref[0], wc_ref[...], preferred_element_type=F32)
    o_ref[0] = y.astype(o_ref.dtype)


def _merge(oa, ob, oc, wa, wb, wc, layer, pgate):
    b, _, kw = oa.shape
    t = pgate.shape[1]
    tn = wa.tn
    nj = wa.n // tn
    d = nj * tn
    tm = _pick(t, (768, 512, 384, 256, 128, 64))
    br = lambda: pl.BlockSpec((1, tm, kw), lambda bi, i, j: (bi, i, 0))
    wt = lambda: _wspec(kw, tn, lambda bi, i, j: (layer, j, 0, 0, 0))
    gt = lambda s: pl.BlockSpec((1, tm, tn), lambda bi, i, j: (bi, i, s * nj + j))
    gsz = jnp.dtype(pgate.dtype).itemsize
    return pl.pallas_call(
        _merge_kernel, name="merge",
        grid=(b, t // tm, nj),
        in_specs=[br(), br(), br(), wt(), wt(), wt(), gt(0), gt(1), gt(2)],
        out_specs=pl.BlockSpec((1, tm, tn), lambda bi, i, j: (bi, i, j)),
        out_shape=jax.ShapeDtypeStruct((b, t, d), BF16),
        compiler_params=_params(("parallel", "parallel", "arbitrary"),
                                2 * (3 * tm * kw * 2 + 3 * kw * tn * 2 + 3 * tm * tn * gsz + tm * tn * 2)
                                + 4 * tm * tn * 4),
    )(oa, ob, oc, wa.w, wb.w, wc.w, pgate, pgate, pgate)


def _pair_sum_matrix():
    r = lax.broadcasted_iota(jnp.int32, (LANE, LANE), 0) // RW_HEAD
    c = lax.broadcasted_iota(jnp.int32, (LANE, LANE), 1) // RW_HEAD
    return jnp.where(r == c, 1.0, 0.0).astype(BF16)


def _head_sums(x, hs):
    rows, w = x.shape
    nt = w // LANE
    stacked = jnp.concatenate([x[:, j * LANE:(j + 1) * LANE] for j in range(nt)], axis=0)
    s = _x_mask_dot(stacked, hs)
    return jnp.concatenate([s[j * rows:(j + 1) * rows] for j in range(nt)], axis=1)


def _rw_prep_kernel(prev_ref, cur_ref, next_ref, grp_ref, mu_ref, w0_ref, wup_ref, a0_ref, aup_ref,
                    gup_ref, kk_ref, ka_ref, rk_ref,
                    r_ref, v_ref, kn_ref, g_ref, bon_ref, lw_ref, kd_ref, bb_ref,
                    *, n_lat_chunks, n_chunks, w, ld, la, lg):
    i = pl.program_id(1)
    c = CHUNK
    cur = cur_ref[0].astype(F32)
    prev = prev_ref[0].astype(F32)
    nxt = next_ref[0].astype(F32)
    is_ctx = i >= n_lat_chunks
    first = jnp.logical_or(i == 0, i == n_lat_chunks)
    last = jnp.logical_or(i == n_lat_chunks - 1, i == n_chunks - 1)
    row = lax.broadcasted_iota(jnp.int32, (c, 1), 0)
    carry_in = jnp.where(jnp.logical_and(is_ctx, jnp.logical_not(first)), 1.0, 0.0)
    carry_out = jnp.where(jnp.logical_and(is_ctx, jnp.logical_not(last)), 1.0, 0.0)
    tm1 = jnp.where(row == 0, prev[c - 1:c] * carry_in, pltpu.roll(cur, 1, 0))
    tp1 = jnp.where(row == c - 1, nxt[0:1] * carry_out, pltpu.roll(cur, c - 1, 0))
    lat_up = jnp.where(jnp.logical_or(is_ctx, first), 0.0, 1.0)
    lat_dn = jnp.where(jnp.logical_or(is_ctx, last), 0.0, 1.0)
    code = jnp.where(is_ctx, grp_ref[1:2], grp_ref[0:1])
    shifted = jnp.where(code == 0, tm1,
                        jnp.where(code == 1, tp1,
                                  jnp.where(code == 2, prev * lat_up, nxt * lat_dn)))
    m = cur + mu_ref[...] * (shifted - cur)

    r = m[:, 0:w]
    k = m[:, w:2 * w]
    v = m[:, 2 * w:3 * w]
    o = 3 * w
    wd = (m[:, o:o + ld], m[:, o + ld:o + 2 * ld])
    ad = (m[:, o + 2 * ld:o + 2 * ld + la], m[:, o + 2 * ld + la:o + 2 * ld + 2 * la])
    gd = m[:, o + 2 * ld + 2 * la:o + 2 * ld + 2 * la + lg]

    hs = _pair_sum_matrix()
    g_ref[0] = _bdot(_sigmoid(gd), gup_ref[...]).astype(g_ref.dtype)
    kx = k * kk_ref[...]
    kn = kx * lax.rsqrt(jnp.maximum(_head_sums(kx * kx, hs), 1e-24))
    r_ref[0] = r.astype(r_ref.dtype)
    v_ref[0] = v.astype(v_ref.dtype)
    kn_ref[0] = kn.astype(kn_ref.dtype)
    bonus = jnp.zeros_like(r)
    for d in range(2):
        wl = w0_ref[d] + _bdot(jnp.tanh(wd[d]), wup_ref[d])
        sp = jnp.maximum(-wl, 0.0) + jnp.log(1.0 + jnp.exp(-jnp.abs(wl)))
        lw_ref[d, 0] = -jnp.exp(-sp - 0.5)
        a = _sigmoid(a0_ref[d] + _bdot(ad[d], aup_ref[d]))
        kd = k * (1.0 + (a - 1.0) * ka_ref[...])
        kd_ref[d, 0] = kd.astype(kd_ref.dtype)
        bb_ref[d, 0] = (kn * a).astype(bb_ref.dtype)
        bonus += r * kd * rk_ref[...]
    bon_ref[0] = (_head_sums(bonus, hs) * v).astype(bon_ref.dtype)


def _rw_prep(pa, grp, mu, w0, wup, a0, aup, gup, k_k, k_a, r_k, n_lat, dims):
    b, t, na = pa.shape
    w, ld, la, lg = dims
    nc = t // CHUNK
    nlc = n_lat // CHUNK
    blk = lambda f: pl.BlockSpec((1, CHUNK, na), f)
    full = lambda a: pl.BlockSpec(a.shape, lambda bi, i: (0,) * a.ndim)
    o1 = pl.BlockSpec((1, CHUNK, w), lambda bi, i: (bi, i, 0))
    o2 = pl.BlockSpec((2, 1, CHUNK, w), lambda bi, i: (0, bi, i, 0))
    s1 = jax.ShapeDtypeStruct((b, t, w), BF16)
    s2 = jax.ShapeDtypeStruct((2, b, t, w), BF16)
    s2f = jax.ShapeDtypeStruct((2, b, t, w), F32)
    consts = (grp, mu, w0, wup, a0, aup, gup, k_k, k_a, r_k)
    return pl.pallas_call(
        functools.partial(_rw_prep_kernel, n_lat_chunks=nlc, n_chunks=nc, w=w, ld=ld, la=la, lg=lg),
        name="rw_prep",
        grid=(b, nc),
        in_specs=[blk(lambda bi, i: (bi, jnp.maximum(i - 1, 0), 0)),
                  blk(lambda bi, i: (bi, i, 0)),
                  blk(lambda bi, i: (bi, jnp.minimum(i + 1, nc - 1), 0))] + [full(a) for a in consts],
        out_specs=[o1, o1, o1, o1, o1, o2, o2, o2],
        out_shape=[s1, s1, s1, s1, s1, s2f, s2, s2],
        compiler_params=_params(("parallel", "arbitrary"), 32 << 20),
    )(pa, pa, pa, *consts)


def _pdot(a, b):
    return lax.dot_general(a.astype(BF16), b.astype(BF16), (((2,), (1,)), ((0,), (0,))),
                           preferred_element_type=F32)


def _pdot_nt(a, b):
    return lax.dot_general(a.astype(BF16), b.astype(BF16), (((2,), (2,)), ((0,), (0,))),
                           preferred_element_type=F32)


def _pdot_tn(a, b):
    return lax.dot_general(a.astype(BF16), b.astype(BF16), (((1,), (1,)), ((0,), (0,))),
                           preferred_element_type=F32)


def _rw_scan_kernel(*refs, n_pairs):
    s_ref = refs[14]

    @pl.when(pl.program_id(1) == 0)
    def _():
        s_ref[...] = jnp.zeros_like(s_ref)

    _rw_scan_dir(0, *refs[0:6], refs[12], s_ref.at[0], n_pairs)
    _rw_scan_dir(1, *refs[6:12], refs[13], s_ref.at[1], n_pairs)


def _rw_scan_dir(d, r_ref, v_ref, kn_ref, lw_ref, kd_ref, bb_ref, y_ref, s_ref, n_pairs):
    c = CHUNK
    wfull = n_pairs * LANE
    incl_bf = jnp.where(_order_masks(d, c)[0], 1.0, 0.0).astype(BF16)
    incl2, strict2 = _order_masks(d, c, 2)
    lane_w = lax.broadcasted_iota(jnp.int32, (1, wfull), 1) % LANE
    w0 = jnp.where(lane_w < RW_HEAD, 1.0, 0.0)
    w1 = 1.0 - w0
    lane = lax.broadcasted_iota(jnp.int32, (1, 1, LANE), 2)
    m0 = jnp.where(lane < RW_HEAD, 1.0, 0.0)
    m1 = 1.0 - m0
    rowh = lax.broadcasted_iota(jnp.int32, (LANE, LANE), 0) // RW_HEAD
    colh = lax.broadcasted_iota(jnp.int32, (LANE, LANE), 1) // RW_HEAD
    blockdiag = rowh == colh
    last_row = jnp.where(d == 0, c - 1, 0)
    rsel = lax.broadcasted_iota(jnp.int32, (c, 1), 0) == last_row

    def pairs(x):
        return jnp.stack([x[:, p * LANE:(p + 1) * LANE] for p in range(n_pairs)], axis=0)

    def stack(x):
        return jnp.concatenate([x * m0, x * m1], axis=1)

    lw = lw_ref[0, 0]
    r = r_ref[0].astype(F32)
    v = v_ref[0].astype(F32)
    kd = kd_ref[0, 0].astype(F32)
    bb = bb_ref[0, 0].astype(F32)
    cw = _mask_dot(incl_bf, lw)
    tot = jnp.sum(jnp.where(rsel, cw, 0.0), axis=0, keepdims=True)
    w_inv = jnp.exp(-cw)
    w_rem = jnp.exp(tot - cw)
    a_t = -kn_ref[0].astype(F32) * jnp.exp(cw - lw)
    r_t = r * jnp.exp(cw)
    b_t = bb * w_inv
    k_t = kd * w_inv
    ar = pairs(jnp.concatenate([a_t, r_t], axis=0))
    bk = pairs(jnp.concatenate([b_t * w0, b_t * w1, k_t * w0, k_t * w1], axis=0))
    vst = pairs(jnp.concatenate([v * w0, v * w1], axis=0))
    uvr = pairs(jnp.concatenate([bb * w_rem, kd * w_rem], axis=0))
    v3 = pairs(v)
    decay = pairs(jnp.exp(tot))
    s0 = s_ref[...]

    sc = _pdot_nt(ar, bk)
    a_ab = jnp.where(strict2, sc[:, :c, :2 * c], 0.0)
    a_ak = jnp.where(strict2, sc[:, :c, 2 * c:], 0.0)
    r_b = jnp.where(incl2, sc[:, c:, :2 * c], 0.0)
    r_k = jnp.where(incl2, sc[:, c:, 2 * c:], 0.0)

    abd = stack(a_ab)
    q = _pdot(abd, abd)
    nsum = abd
    for _ in range(4):
        both = _pdot(q, jnp.concatenate([nsum, q], axis=2))
        nsum = nsum + q + both[:, :, :LANE]
        q = both[:, :, LANE:]
    nsum = nsum + q + _pdot(q, nsum)
    n_side = nsum[:, :c] + nsum[:, c:]

    art = _pdot_nt(ar, s0)
    rhs = art[:, :c] + _pdot(a_ak, vst)
    u = rhs + _pdot(n_side, stack(rhs))
    y = art[:, c:] + _pdot(jnp.concatenate([r_b, r_k], axis=2), jnp.concatenate([stack(u), vst], axis=1))
    for p in range(n_pairs):
        y_ref[0, :, p * LANE:(p + 1) * LANE] = y[p]
    upd = _pdot_tn(jnp.concatenate([u, v3], axis=1), uvr)
    s_ref[...] = s0 * decay + jnp.where(blockdiag, upd, 0.0)


def _scan_chunk(d, i, n_lat_chunks, n_chunks):
    nctx = n_chunks - n_lat_chunks
    in_ctx = i < nctx
    fwd = jnp.where(in_ctx, n_lat_chunks + i, i - nctx)
    bwd = jnp.where(in_ctx, n_chunks - 1 - i, n_chunks - 1 - i)
    return jnp.where(d == 0, fwd, bwd)


def _rw_scan(r, v, kn, lw, kd, bb, n_lat):
    b, t, w = r.shape
    nc = t // CHUNK
    nlc = n_lat // CHUNK
    npair = w // LANE
    ch = functools.partial(_scan_chunk, n_lat_chunks=nlc, n_chunks=nc)
    s1 = lambda d: pl.BlockSpec((1, CHUNK, w), lambda bi, i: (bi, ch(d, i), 0))
    s2 = lambda d: pl.BlockSpec((1, 1, CHUNK, w), lambda bi, i: (d, bi, ch(d, i), 0))
    per_dir = lambda d: [s1(d), s1(d), s1(d), s2(d), s2(d), s2(d)]
    out = jax.ShapeDtypeStruct((b, t, w), F32)
    return pl.pallas_call(
        functools.partial(_rw_scan_kernel, n_pairs=npair), name="rw_scan",
        grid=(b, nc),
        in_specs=per_dir(0) + per_dir(1),
        out_specs=[s1(0), s1(1)],
        out_shape=[out, out],
        scratch_shapes=[pltpu.VMEM((2, npair, LANE, LANE), F32)],
        compiler_params=_params(("arbitrary", "arbitrary"), 32 << 20),
    )(r, v, kn, lw, kd, bb, r, v, kn, lw, kd, bb)


def _rw_finish_kernel(yf_ref, yb_ref, bon_ref, g_ref, lg_ref, lb_ref, o_ref):
    y = yf_ref[0] + yb_ref[0]
    hs = _pair_sum_matrix()
    mean = _head_sums(y, hs) * (1.0 / RW_HEAD)
    yc = y - mean
    var = _head_sums(yc * yc, hs) * (1.0 / RW_HEAD)
    yn = yc * lax.rsqrt(var + RW_GN_EPS)
    o_ref[0] = ((yn * lg_ref[...] + lb_ref[...] + bon_ref[0].astype(F32)) * g_ref[0].astype(F32)).astype(o_ref.dtype)


def _rw_finish(y_f, y_b, bonus, g, ln_g, ln_b):
    b, t, w = y_f.shape
    tm = _pick(t, (256, 128, 64))
    s1 = pl.BlockSpec((1, tm, w), lambda bi, i: (bi, i, 0))
    vec = pl.BlockSpec((1, w), lambda bi, i: (0, 0))
    return pl.pallas_call(
        _rw_finish_kernel, name="rw_finish",
        grid=(b, t // tm),
        in_specs=[s1, s1, s1, s1, vec, vec],
        out_specs=s1,
        out_shape=jax.ShapeDtypeStruct((b, t, w), BF16),
        compiler_params=_params(("parallel", "arbitrary"), 32 << 20),
    )(y_f, y_b, bonus, g, ln_g.reshape(1, w), ln_b.reshape(1, w))


def _hg_scan_kernel(qf_ref, zf_ref, if_ref, qb_ref, zb_ref, ib_ref, lb_ref, of_ref, ob_ref, s_ref, *, n_heads):
    @pl.when(pl.program_id(1) == 0)
    def _():
        s_ref[...] = jnp.zeros_like(s_ref)

    _hg_scan_dir(0, qf_ref, zf_ref, if_ref, lb_ref, of_ref, s_ref.at[0], n_heads)
    _hg_scan_dir(1, qb_ref, zb_ref, ib_ref, lb_ref, ob_ref, s_ref.at[1], n_heads)


def _hg_scan_dir(d, q_ref, z_ref, i_ref, lb_ref, o_ref, s_ref, n_heads):
    c = CHUNK
    nb = c // SUB
    sgn = 1 - 2 * d
    fwd = d == 0
    incl_bf = jnp.where(_order_masks(d, c)[0], 1.0, 0.0).astype(BF16)
    row = lax.broadcasted_iota(jnp.int32, (c, c), 0)
    col = lax.broadcasted_iota(jnp.int32, (c, c), 1)
    rb, cbk = row // SUB, col // SUB
    lvl_b = (rb - cbk) * sgn > 0
    lvl_1 = jnp.logical_and(rb == cbk, (row - col) * sgn >= 0)
    last_row = jnp.where(d == 0, c - 1, 0)
    rsel = lax.broadcasted_iota(jnp.int32, (c, 1), 0) == last_row

    def heads(x):
        return jnp.stack([x[:, h * LANE:(h + 1) * LANE] for h in range(n_heads)], axis=0)

    q = q_ref[0]
    z = z_ref[0]
    val = i_ref[0]
    lg_l = lb_ref[d, 0:1]
    lg_1ml = lb_ref[d, 1:2]
    one_ml = lb_ref[d, 2:3]
    s0 = s_ref[...]

    ez = jnp.exp(-jnp.abs(z))
    log_sig = jnp.minimum(z, 0.0) - jnp.log(1.0 + ez)
    x2 = lg_1ml + log_sig
    mx = jnp.maximum(lg_l, x2)
    log_f = mx + jnp.log(jnp.exp(lg_l - mx) + jnp.exp(x2 - mx))
    k = one_ml * jnp.where(z >= 0.0, ez, 1.0) / (1.0 + ez)

    cb = _mask_dot(incl_bf, log_f)
    tot = jnp.sum(jnp.where(rsel, cb, 0.0), axis=0, keepdims=True)
    wfull = cb.shape[1]
    cb3 = cb.reshape(nb, SUB, wfull)
    blk_end = cb3[:, SUB - 1:SUB] if fwd else cb3[:, 0:1]

    def q_refd(refs):
        return heads(jnp.concatenate([q * jnp.exp(jnp.minimum(cb - e, 0.0)) for e in refs], axis=0))

    end_own = jnp.broadcast_to(blk_end, (nb, SUB, wfull)).reshape(c, wfull)
    refs = [blk_end[j] for j in range(nb)]
    refs += [jnp.broadcast_to(cb3[:, j:j + 1], (nb, SUB, wfull)).reshape(c, wfull) for j in range(SUB)]
    ks = heads(jnp.concatenate([k * jnp.exp(end_own - cb), k], axis=0))
    sc = _pdot_nt(q_refd(refs), ks)
    att = jnp.zeros((n_heads, c, c), F32)
    for j in range(nb):
        att += jnp.where(jnp.logical_and(lvl_b, cbk == j), sc[:, j * c:(j + 1) * c, :c], 0.0)
    for j in range(SUB):
        att += jnp.where(jnp.logical_and(lvl_1, col % SUB == j), sc[:, (nb + j) * c:(nb + j + 1) * c, c:], 0.0)
    v3 = heads(val)
    o = _pdot(att, v3) + _pdot_nt(heads(q * jnp.exp(cb)), s0)
    for h in range(n_heads):
        o_ref[0, :, h * LANE:(h + 1) * LANE] = o[h]
    s_ref[...] = s0 * heads(jnp.exp(tot)) + _pdot_tn(v3, heads(k * jnp.exp(tot - cb)))


def _hg_scan(phg, lbt, n_lat, w):
    b, t, _ = phg.shape
    nc = t // CHUNK
    nlc = n_lat // CHUNK
    nh = w // LANE
    ch = functools.partial(_scan_chunk, n_lat_chunks=nlc, n_chunks=nc)
    col = lambda d, cb_: pl.BlockSpec((1, CHUNK, w), lambda bi, i: (bi, ch(d, i), cb_))
    out = jax.ShapeDtypeStruct((b, t, w), F32)
    return pl.pallas_call(
        functools.partial(_hg_scan_kernel, n_heads=nh), name="hg_scan",
        grid=(b, nc),
        in_specs=[col(0, 0), col(0, 1), col(0, 3), col(1, 0), col(1, 2), col(1, 3),
                  pl.BlockSpec((2, 3, w), lambda bi, i: (0, 0, 0))],
        out_specs=[col(0, 0), col(1, 0)],
        out_shape=[out, out],
        scratch_shapes=[pltpu.VMEM((2, nh, LANE, LANE), F32)],
        compiler_params=_params(("arbitrary", "arbitrary"), 32 << 20),
    )(phg, phg, phg, phg, phg, phg, lbt)


def _hg_finish_kernel(of_ref, ob_ref, g_ref, ng_ref, out_ref, *, n_heads):
    for h in range(n_heads):
        sl = slice(h * LANE, (h + 1) * LANE)
        o = of_ref[0, :, sl] + ob_ref[0, :, sl]
        y = o * lax.rsqrt(jnp.mean(o * o, axis=-1, keepdims=True) + NORM_EPS) * ng_ref[...]
        g = g_ref[0, :, sl]
        out_ref[0, :, sl] = (y * g * _sigmoid(g)).astype(out_ref.dtype)


def _hg_finish(o_f, o_b, phg, norm_g, w):
    b, t, _ = o_f.shape
    tm = _pick(t, (256, 128, 64))
    return pl.pallas_call(
        functools.partial(_hg_finish_kernel, n_heads=w // LANE), name="hg_finish",
        grid=(b, t // tm),
        in_specs=[pl.BlockSpec((1, tm, w), lambda bi, i: (bi, i, 0)),
                  pl.BlockSpec((1, tm, w), lambda bi, i: (bi, i, 0)),
                  pl.BlockSpec((1, tm, w), lambda bi, i: (bi, i, 4)),
                  pl.BlockSpec((1, LANE), lambda bi, i: (0, 0))],
        out_specs=pl.BlockSpec((1, tm, w), lambda bi, i: (bi, i, 0)),
        out_shape=jax.ShapeDtypeStruct((b, t, w), BF16),
        compiler_params=_params(("parallel", "arbitrary"), 32 << 20),
    )(o_f, o_b, phg, norm_g.reshape(1, LANE))


def _hy_pre_kernel(prev_ref, cur_ref, next_ref, cw_ref, cb_ref, x0_ref, z_ref, *, n_lat_blocks, n_blocks, w):
    i = pl.program_id(1)
    tm = cur_ref.shape[1]
    hb = prev_ref.shape[1]
    cur = cur_ref[0].astype(F32)
    first = jnp.logical_or(i == 0, i == n_lat_blocks)
    last = jnp.logical_or(i == n_lat_blocks - 1, i == n_blocks - 1)
    row = lax.broadcasted_iota(jnp.int32, (tm, 1), 0)
    p_last = prev_ref[0, hb - 1:hb].astype(F32) * jnp.where(first, 0.0, 1.0)
    n_first = next_ref[0, 0:1].astype(F32) * jnp.where(last, 0.0, 1.0)
    before = jnp.where(row == 0, p_last, pltpu.roll(cur, 1, 0))
    after = jnp.where(row == tm - 1, n_first, pltpu.roll(cur, tm - 1, 0))
    u = cw_ref[0:1] * before + cw_ref[1:2] * cur + cw_ref[2:3] * after + cb_ref[...]
    x0_ref[0] = u[:, 0:w].astype(x0_ref.dtype)
    z_ref[0] = (u[:, w:2 * w] * u[:, 2 * w:3 * w]).astype(z_ref.dtype)


def _hy_pre(phy, conv_w, conv_b, n_lat):
    b, t, w3 = phy.shape
    w = w3 // 3
    tm = _pick(math.gcd(n_lat, t - n_lat), (256, 128, 64))
    nb = t // tm
    hb = 16
    per = tm // hb
    o = pl.BlockSpec((1, tm, w), lambda bi, i: (bi, i, 0))
    s = jax.ShapeDtypeStruct((b, t, w), BF16)
    return pl.pallas_call(
        functools.partial(_hy_pre_kernel, n_lat_blocks=n_lat // tm, n_blocks=nb, w=w), name="hy_pre",
        grid=(b, nb),
        in_specs=[pl.BlockSpec((1, hb, w3), lambda bi, i: (bi, jnp.maximum(i * per - 1, 0), 0)),
                  pl.BlockSpec((1, tm, w3), lambda bi, i: (bi, i, 0)),
                  pl.BlockSpec((1, hb, w3), lambda bi, i: (bi, jnp.minimum((i + 1) * per, t // hb - 1), 0)),
                  pl.BlockSpec((3, w3), lambda bi, i: (0, 0)),
                  pl.BlockSpec((1, w3), lambda bi, i: (0, 0))],
        out_specs=[o, o],
        out_shape=[s, s],
        compiler_params=_params(("parallel", "arbitrary"), 40 << 20),
    )(phy, phy, phy, conv_w, conv_b.reshape(1, w3))


def _hdot(a, b):
    return jnp.dot(a, b, precision=lax.Precision.HIGHEST, preferred_element_type=F32)


def _hy_taps_kernel(ft_ref, w1_ref, b1_ref, w2_ref, b2_ref, w3f_ref, w3b_ref, fr_ref, dl_ref,
                    tap_ref, sum_ref):
    i = pl.program_id(1)
    ft = ft_ref[...]
    fr = fr_ref[...]
    h = jnp.sin(fr * (_hdot(ft, w1_ref[...]) + b1_ref[...]))
    h = jnp.sin(fr * (_hdot(h, w2_ref[...]) + b2_ref[...]))
    cf = ft[:, HY_EMB:HY_EMB + 1]
    cb = ft[:, HY_EMB + 1:HY_EMB + 2]
    tt = ft[:, 0:1]
    tap = (cf * _hdot(h, w3f_ref[...]) + cb * _hdot(h, w3b_ref[...])) * jnp.exp(-tt * dl_ref[...])
    tap_ref[...] = tap

    @pl.when(i == 0)
    def _():
        sum_ref[...] = jnp.zeros_like(sum_ref)

    sum_ref[...] += jnp.sum(jnp.abs(tap), axis=0, keepdims=True)


def _hy_taps(feats, w1p, b1, w2, b2, w3, freq, deltas):
    rws = feats.shape[0]
    hid = w2.shape[0]
    w = w3.shape[1] // 2
    tr = _pick(rws, (1024, 512, 256, 128))
    ct = w
    nj = w // ct
    c2 = lambda a: pl.BlockSpec(a.shape, lambda j, i: (0, 0))
    return pl.pallas_call(
        _hy_taps_kernel, name="hy_taps",
        grid=(nj, rws // tr),
        in_specs=[pl.BlockSpec((tr, LANE), lambda j, i: (i, 0)),
                  c2(w1p), pl.BlockSpec((1, hid), lambda j, i: (0, 0)),
                  c2(w2), pl.BlockSpec((1, hid), lambda j, i: (0, 0)),
                  pl.BlockSpec((hid, ct), lambda j, i: (0, j)),
                  pl.BlockSpec((hid, ct), lambda j, i: (0, nj + j)),
                  pl.BlockSpec((1, hid), lambda j, i: (0, 0)),
                  pl.BlockSpec((1, ct), lambda j, i: (0, j))],
        out_specs=[pl.BlockSpec((tr, ct), lambda j, i: (i, j)),
                   pl.BlockSpec((1, ct), lambda j, i: (0, j))],
        out_shape=[jax.ShapeDtypeStruct((rws, w), F32), jax.ShapeDtypeStruct((1, w), F32)],
        compiler_params=_params(("parallel", "arbitrary"), 32 << 20),
    )(feats, w1p, b1.reshape(1, hid), w2, b2.reshape(1, hid), w3, w3, freq.reshape(1, hid),
      deltas.reshape(1, w))


def _dft1_kernel(g_ref, x_ref, o_ref):
    o_ref[...] = lax.dot_general(g_ref[...], x_ref[...].astype(BF16), (((2,), (1,)), ((0,), (0,))),
                                 preferred_element_type=F32).astype(o_ref.dtype)


def _dft1(g, x):
    nb, m2, k = g.shape
    c = x.shape[2]
    bt = _pick(nb, (8, 4, 2))
    ct = _pick(c, (256, 128))
    return pl.pallas_call(
        _dft1_kernel, name="dft1",
        grid=(c // ct, nb // bt),
        in_specs=[pl.BlockSpec((bt, m2, k), lambda j, i: (i, 0, 0)),
                  pl.BlockSpec((bt, k, ct), lambda j, i: (i, 0, j))],
        out_specs=pl.BlockSpec((bt, m2, ct), lambda j, i: (i, 0, j)),
        out_shape=jax.ShapeDtypeStruct((nb, m2, c), BF16),
        compiler_params=_params(("parallel", "arbitrary"), 32 << 20),
    )(g, x)


def _spec_kernel(a_ref, f2_ref, sc_ref, o_ref):
    for l in range(a_ref.shape[0]):
        o_ref[l] = jnp.dot(f2_ref[...], a_ref[l].astype(BF16), preferred_element_type=F32) * sc_ref[...]


def _spec(a, f2, scale):
    na, m2, c = a.shape
    kt = _pick(na, (8, 4, 2))
    ct = _pick(c, (256, 128))
    return pl.pallas_call(
        _spec_kernel, name="hy_spec",
        grid=(c // ct, na // kt),
        in_specs=[pl.BlockSpec((kt, m2, ct), lambda j, i: (i, 0, j)),
                  pl.BlockSpec((m2, m2), lambda j, i: (0, 0)),
                  pl.BlockSpec((1, ct), lambda j, i: (0, j))],
        out_specs=pl.BlockSpec((kt, m2, ct), lambda j, i: (i, 0, j)),
        out_shape=jax.ShapeDtypeStruct((na, m2, c), F32),
        compiler_params=_params(("parallel", "arbitrary"), 32 << 20),
    )(a, f2, scale)


def _conv_mid_kernel(a_ref, f2_ref, h_ref, g3_ref, o_ref):
    nbh = a_ref.shape[1] // 2
    for l in range(a_ref.shape[0]):
        x = jnp.dot(f2_ref[...], a_ref[l].astype(BF16), preferred_element_type=F32)
        xr, xi = x[:nbh], x[nbh:]
        hr, hi = h_ref[l, :nbh], h_ref[l, nbh:]
        y = jnp.concatenate([xr * hr - xi * hi, xr * hi + xi * hr], axis=0)
        o_ref[l] = jnp.dot(g3_ref[l], y.astype(BF16), preferred_element_type=F32).astype(o_ref.dtype)


def _conv_mid(a, f2, h, g3):
    na, m2, c = a.shape
    kt = _pick(na, (8, 4, 2))
    ct = _pick(c, (256, 128))
    slab = pl.BlockSpec((kt, m2, ct), lambda j, i: (i, 0, j))
    return pl.pallas_call(
        _conv_mid_kernel, name="conv_mid",
        grid=(c // ct, na // kt),
        in_specs=[slab, pl.BlockSpec((m2, m2), lambda j, i: (0, 0)), slab,
                  pl.BlockSpec((kt, m2, m2), lambda j, i: (i, 0, 0))],
        out_specs=slab,
        out_shape=jax.ShapeDtypeStruct((na, m2, c), BF16),
        compiler_params=_params(("parallel", "arbitrary"), 32 << 20),
    )(a, f2, h, g3)


def _conv_out_kernel(b_ref, f4_ref, o_ref, *, scale):
    for l in range(b_ref.shape[0]):
        o_ref[l] = jnp.dot(f4_ref[...], b_ref[l].astype(BF16), preferred_element_type=F32) * scale


def _conv_out(bm, f4, scale):
    nb, m2, c = bm.shape
    mo = f4.shape[0]
    pt = _pick(nb, (8, 4, 2))
    ct = _pick(c, (256, 128))
    return pl.pallas_call(
        functools.partial(_conv_out_kernel, scale=scale), name="conv_out",
        grid=(c // ct, nb // pt),
        in_specs=[pl.BlockSpec((pt, m2, ct), lambda j, i: (i, 0, j)),
                  pl.BlockSpec((mo, m2), lambda j, i: (0, 0))],
        out_specs=pl.BlockSpec((pt, mo, ct), lambda j, i: (i, 0, j)),
        out_shape=jax.ShapeDtypeStruct((nb, mo, c), F32),
        compiler_params=_params(("parallel", "arbitrary"), 32 << 20),
    )(bm, f4)


def _ctx_conv_kernel(z_ref, ext_ref, sc_ref, o_ref):
    n = z_ref.shape[1]

    def body(s, acc):
        return acc + ext_ref[pl.ds(n - s, n), :] * z_ref[0, pl.ds(s, 1), :]

    acc = lax.fori_loop(0, n, body, jnp.zeros(o_ref.shape[1:], F32))
    o_ref[0] = acc * sc_ref[...]


def _ctx_conv(z, ext, scale):
    b, n, w = z.shape
    ct = LANE
    return pl.pallas_call(
        _ctx_conv_kernel, name="ctx_conv",
        grid=(b, w // ct),
        in_specs=[pl.BlockSpec((1, n, ct), lambda bi, j: (bi, 0, j)),
                  pl.BlockSpec((2 * n, ct), lambda bi, j: (0, j)),
                  pl.BlockSpec((1, ct), lambda bi, j: (0, j))],
        out_specs=pl.BlockSpec((1, n, ct), lambda bi, j: (bi, 0, j)),
        out_shape=jax.ShapeDtypeStruct((b, n, w), F32),
        compiler_params=_params(("parallel", "arbitrary"), 16 << 20),
    )(z, ext, scale)


def _hy_post_kernel(y_ref, z_ref, x0_ref, bias_ref, o_ref):
    o_ref[0] = (x0_ref[0].astype(F32) * (y_ref[0] + bias_ref[...] * z_ref[0].astype(F32))).astype(o_ref.dtype)


def _hy_post(y, z, x0, bias):
    b, t, w = y.shape
    tm = _pick(t, _TM_ROWS)
    s = pl.BlockSpec((1, tm, w), lambda bi, i: (bi, i, 0))
    return pl.pallas_call(
        _hy_post_kernel, name="hy_post",
        grid=(b, t // tm),
        in_specs=[s, s, s, pl.BlockSpec((1, w), lambda bi, i: (0, 0))],
        out_specs=s,
        out_shape=jax.ShapeDtypeStruct((b, t, w), BF16),
        compiler_params=_params(("parallel", "arbitrary"), 32 << 20),
    )(y, z, x0, bias.reshape(1, w))


def _dft_factors(n):
    m = 2 * n
    na = 1 << ((m.bit_length() - 1) // 2)
    return na, m // na


def _cis(num, den, sign):
    ang = (num % den).astype(F32) * (2.0 * math.pi / den)
    return jnp.cos(ang), sign * jnp.sin(ang)


def _cblock(cr, ci):
    return jnp.concatenate([jnp.concatenate([cr, -ci], axis=-1),
                            jnp.concatenate([ci, cr], axis=-1)], axis=-2)


def _dft_tables(n):
    na, nb = _dft_factors(n)
    m = na * nb
    ah = na // 2
    ar = lambda k: jnp.arange(k, dtype=jnp.int32)
    ka, bb = ar(na)[None, :, None], ar(nb)[:, None, None]
    g1c = _cblock(*_cis(ka * (nb * ar(ah)[None, None, :] + bb), m, -1.0))
    g1t = jnp.concatenate(_cis(ka * (nb * ar(na)[None, None, :] + bb), m, -1.0), axis=-2)
    f2 = _cblock(*_cis(ar(nb)[:, None] * ar(nb)[None, :], nb, -1.0))
    g3 = _cblock(*_cis(ar(nb)[None, :, None] * (ar(na)[:, None, None] + na * ar(nb)[None, None, :]), m, 1.0))
    f4 = _cblock(*_cis(ar(ah)[:, None] * ar(na)[None, :], na, 1.0))
    return tuple(t.astype(BF16) for t in (g1c, g1t, f2, g3, f4))


def _filter_feats(n, order):
    bands_n = (HY_EMB - 1) // 2
    t = jnp.linspace(0.0, 1.0, n, dtype=F32)[:, None]
    lag = jnp.arange(n, dtype=F32)[:, None]
    bands = jnp.linspace(1e-4, bands_n - 1, bands_n, dtype=F32)[None, :]
    ang = 2.0 * math.pi * lag * bands / n
    zf = jnp.concatenate([t, jnp.cos(ang), -jnp.sin(ang)], axis=-1)
    if order == "dft":
        na, nb = _dft_factors(n)
        m = (np.arange(na)[None, :] * nb + np.arange(nb)[:, None]).reshape(-1)
    else:
        m = (np.arange(2 * n) - n) % (2 * n)
    lag_of = np.where(m < n, m, np.where(m == n, 0, 2 * n - m))
    cf = np.where(m == 0, 0.5, np.where(m < n, 1.0, 0.0)).astype(np.float32)
    cb = np.where(m == 0, 0.5, np.where(m > n, 1.0, 0.0)).astype(np.float32)
    pad = jnp.zeros((2 * n, LANE - HY_EMB - 2), F32)
    return jnp.concatenate([zf[lag_of], jnp.asarray(cf)[:, None], jnp.asarray(cb)[:, None], pad], axis=-1)


def _hy_deltas(w):
    return jnp.abs(jnp.linspace(math.log(HY_DECAY_TARGET) / HY_SLOW_PCT,
                                math.log(HY_DECAY_TARGET) / HY_FAST_PCT, w, dtype=F32))


def _long_conv_latent(z, taps_perm, inv_l1):
    b, n, w = z.shape
    na, nb = _dft_factors(n)
    ah = na // 2
    g1c, g1t, f2, g3, f4 = _dft_tables(n)
    h = _spec(_dft1(g1t, taps_perm.reshape(nb, na, w)).reshape(nb, 2, na, w).transpose(2, 1, 0, 3)
              .reshape(na, 2 * nb, w), f2, inv_l1)
    outs = []
    for pair in range(b // 2):
        zz = z[2 * pair:2 * pair + 2].astype(BF16).reshape(2, ah, nb, w).transpose(2, 0, 1, 3).reshape(nb, na, w)
        a = _dft1(g1c, zz).reshape(nb, 2, na, w).transpose(2, 1, 0, 3).reshape(na, 2 * nb, w)
        bm = _conv_mid(a, f2, h, g3).reshape(na, 2, nb, w).transpose(2, 1, 0, 3).reshape(nb, 2 * na, w)
        y = _conv_out(bm, f4, 1.0 / (na * nb))
        outs.append(y.reshape(nb, 2, ah, w).transpose(1, 2, 0, 3).reshape(2, n, w))
    return jnp.concatenate(outs, axis=0) if len(outs) > 1 else outs[0]


def _pack_in_proj(w_in, rw_mu, dims):
    depth, d_model, _ = w_in.shape
    w, dl, al, gl = dims
    ld, la, lg = _rup(dl, LANE), _rup(al, LANE), _rup(gl, LANE)
    rw_cols = 3 * w + 2 * dl + 2 * al + gl
    tile = 1024 if d_model >= 2048 else LANE
    na = _rup(3 * w + 2 * ld + 2 * la + lg, tile)
    pieces = [(0, 3 * w, 0), (3 * w, dl, 3 * w), (3 * w + dl, dl, 3 * w + ld),
              (3 * w + 2 * dl, al, 3 * w + 2 * ld), (3 * w + 2 * dl + al, al, 3 * w + 2 * ld + la),
              (3 * w + 2 * dl + 2 * al, gl, 3 * w + 2 * ld + 2 * la)]
    cols, mus = [], []
    orig = np.full((na,), rw_cols - 1, np.int64)
    pos = 0
    for src, width, dst in pieces + [(rw_cols, 0, na)]:
        if dst > pos:
            cols.append(jnp.zeros((depth, d_model, dst - pos), BF16))
            mus.append(jnp.zeros((depth, dst - pos), F32))
        cols.append(w_in[:, :, src:src + width].astype(BF16))
        mus.append(rw_mu[:, src:src + width])
        orig[dst:dst + width] = np.arange(src, src + width)
        pos = dst + width
    wa = _tile_w(jnp.concatenate(cols, axis=2), tile)
    shift = _rup(rw_cols, tile) - rw_cols
    total = shift + w_in.shape[2]
    wrest = _tile_w(jnp.pad(w_in, ((0, 0), (0, 0), (shift, _rup(total, tile) - total))), tile)
    starts = (rw_cols, rw_cols + 5 * w, rw_cols + 8 * w)
    offs = [(shift + s) // tile for s in starts]
    mu = jnp.concatenate(mus, axis=1).reshape(depth, 1, na)
    grp = np.stack([orig // (rw_cols // 4), orig // (rw_cols // 2)]).astype(np.int32)
    grp = np.minimum(grp, np.array([[3], [1]])).astype(np.int32)
    return wa, wrest, tile, offs, mu, jnp.asarray(grp), (w, ld, la, lg), na


def _pad_rows(a, rows):
    return jnp.pad(a, [(0, 0)] * (a.ndim - 2) + [(0, rows - a.shape[-2]), (0, 0)])


def kernel(x, c, ctx, c_ctx, ada_w, ada_b, norm1_g, norm2_g, w_in, rw_mu, rw_w0, rw_w_up, rw_a0, rw_a_up, rw_g_up, rw_k_k, rw_k_a, rw_r_k, rw_ln_g, rw_ln_b, hg_lower_bounds, hg_norm_g, hy_conv_w, hy_conv_b, hy_f_w1, hy_f_b1, hy_f_w2, hy_f_b2, hy_f_w3, hy_freq, hy_bias, w_branch_a, w_branch_b, w_branch_c, w_out, ffn_w_gate, ffn_w_up, ffn_w_down, final_norm_g):
    bsz, n_lat, d = x.shape
    n_ctx = ctx.shape[1]
    depth = w_in.shape[0]
    w = rw_k_k.shape[1]
    dims = (w, rw_w_up.shape[2], rw_a_up.shape[2], rw_g_up.shape[1])
    assert bsz % 2 == 0 and n_lat % GRID_W == 0 and n_ctx % CHUNK == 0 and w % LANE == 0
    rw_cols = 3 * w + 2 * dims[1] + 2 * dims[2] + dims[3]
    hg_end = rw_cols + 5 * w
    hy_end = hg_end + 3 * w

    xs = jnp.concatenate([x, ctx], axis=1)
    cvec = jnp.zeros((_rup(bsz + 1, 8), d), F32).at[:bsz].set(c).at[bsz].set(c_ctx)
    lb_cum = jnp.cumsum(jax.nn.softmax(hg_lower_bounds.astype(F32), axis=0), axis=0)
    deltas = _hy_deltas(w)

    wt_rw, wt_in, tile, offs, mu_all, grp, pdims, na = _pack_in_proj(w_in, rw_mu, dims)
    assert (5 * w) % tile == 0 and (3 * w) % tile == 0 and (3 * d) % tile == 0
    _, ld, la, lg = pdims
    hid = ffn_w_gate.shape[2]
    hp = _rup(hid, 1024) if hid > 1024 else _rup(hid, LANE)
    wt_g = _tile_w(_cast_pad(ffn_w_gate, d, hp), _tn_ffn(hp))
    wt_u = _tile_w(_cast_pad(ffn_w_up, d, hp), _tn_ffn(hp))
    wt_d = _tile_w(_cast_pad(ffn_w_down, hp, d), _tn_wide(d), _tk_of(hp))
    wt_o = _tile_w(w_out, _tn_wide(d), _tk_of(d))
    wt_a, wt_b, wt_c = (_tile_w(t, _tn_wide(d)) for t in (w_branch_a, w_branch_b, w_branch_c))

    for layer in range(depth):
        need_ctx = layer < depth - 1
        mod = _ada(cvec, ada_w, ada_b, layer)[:bsz + 1].reshape(bsz + 1, 1, 6, d)
        mods = [mod[:, :, s, :] for s in range(6)]

        h1 = _modnorm(xs, norm1_g[layer], mods[0], mods[1], n_lat)
        pa = _matmul(h1, wt_rw, layer, 0, na // tile, BF16)
        phg = _matmul(h1, wt_in, layer, offs[0], 5 * w // tile, F32)
        rows = None if need_ctx else n_lat
        phy = _matmul(h1, wt_in, layer, offs[1], 3 * w // tile, BF16, rows)
        pgate = _matmul(h1, wt_in, layer, offs[2], 3 * d // tile, BF16, rows)

        r, v, kn, g, bonus, lw, kd, bb = _rw_prep(
            pa, grp, mu_all[layer], rw_w0[layer].reshape(2, 1, w), _pad_rows(rw_w_up[layer], ld),
            rw_a0[layer].reshape(2, 1, w), _pad_rows(rw_a_up[layer], la), _pad_rows(rw_g_up[layer], lg),
            rw_k_k[layer].reshape(1, w), rw_k_a[layer].reshape(1, w), rw_r_k[layer].reshape(1, w),
            n_lat, pdims)
        y_f, y_b = _rw_scan(r, v, kn, lw, kd, bb, n_lat)
        oa = _rw_finish(y_f, y_b, bonus, g, rw_ln_g[layer], rw_ln_b[layer])

        lb = lb_cum[layer] - lb_cum[0]
        lbt = jnp.stack([jnp.log(lb), jnp.log1p(-lb), 1.0 - lb], axis=1)
        o_f, o_b = _hg_scan(phg, lbt, n_lat, w)
        ob = _hg_finish(o_f, o_b, phg, hg_norm_g[layer], w)

        x0, z = _hy_pre(phy, hy_conv_w[layer], hy_conv_b[layer], n_lat)
        w1p = _pad_rows(hy_f_w1[layer], LANE)
        filt = (w1p, hy_f_b1[layer], hy_f_w2[layer], hy_f_b2[layer], hy_f_w3[layer], hy_freq[layer], deltas)
        taps, l1 = _hy_taps(_filter_feats(n_lat, "dft"), *filt)
        y_lat = _long_conv_latent(z[:, :n_lat], taps, 1.0 / l1)
        if need_ctx:
            ext, l1c = _hy_taps(_filter_feats(n_ctx, "lag"), *filt)
            y_ctx = _ctx_conv(z[:, n_lat:].astype(F32), ext, 1.0 / l1c)
            y_lat = jnp.concatenate([y_lat, y_ctx], axis=1)
        oc = _hy_post(y_lat, z, x0, hy_bias[layer])

        ym = _merge(oa, ob, oc, wt_a, wt_b, wt_c, layer, pgate)
        xs = _matmul_resid(ym, wt_o, layer, xs, mods[2], n_lat)

        h2 = _modnorm(xs, norm2_g[layer], mods[3], mods[4], n_lat)
        act = _swiglu_up(h2, wt_g, wt_u, layer)
        xs = _matmul_resid(act, wt_d, layer, xs, mods[5], n_lat)

    return _final_norm(xs, final_norm_g, n_lat)
```

```python
import functools
import math

import numpy as np
import jax
import jax.numpy as jnp
from jax import lax
from jax.experimental import pallas as pl
from jax.experimental.pallas import tpu as pltpu

F32 = jnp.float32
BF16 = jnp.bfloat16

GRID_W = 64
CHUNK = 64
SUB = 8
NORM_EPS = 1e-6
RW_HEAD = 64
RW_GN_EPS = 64e-5
HG_EXPAND = 128
HY_EMB = 33
HY_DECAY_TARGET = 1e-2
HY_FAST_PCT = 0.3
HY_SLOW_PCT = 1.5
LANE = 128
VMEM_CAP = 56 * 1024 * 1024
VMEM_SLACK = 8 * 1024 * 1024


def _params(sem, vmem_bytes):
    return pltpu.CompilerParams(dimension_semantics=sem,
                                vmem_limit_bytes=int(min(max(vmem_bytes + VMEM_SLACK, 16 << 20), VMEM_CAP)))


def _pick(n, cands):
    for c in cands:
        if n % c == 0:
            return c
    return n


def _rup(n, m):
    return -(-n // m) * m


def _bdot(a, b):
    return jnp.dot(a.astype(BF16), b.astype(BF16), preferred_element_type=F32)


def _bdot_nt(a, b):
    return lax.dot_general(a.astype(BF16), b.astype(BF16), (((1,), (1,)), ((), ())),
                           preferred_element_type=F32)


def _bdot_tn(a, b):
    return lax.dot_general(a.astype(BF16), b.astype(BF16), (((0,), (0,)), ((), ())),
                           preferred_element_type=F32)


def _split(x):
    hi = x.astype(BF16)
    lo = (x - hi.astype(F32)).astype(BF16)
    return hi, lo


def _mask_dot(m, x):
    hi, lo = _split(x)
    return (jnp.dot(m, hi, preferred_element_type=F32) + jnp.dot(m, lo, preferred_element_type=F32))


def _x_mask_dot(x, m):
    hi, lo = _split(x)
    return (jnp.dot(hi, m, preferred_element_type=F32) + jnp.dot(lo, m, preferred_element_type=F32))


def _sigmoid(x):
    return 1.0 / (1.0 + jnp.exp(-x))


def _order_masks(d, n, reps=1):
    row = lax.broadcasted_iota(jnp.int32, (n, reps * n), 0)
    col = lax.broadcasted_iota(jnp.int32, (n, reps * n), 1) % n
    diff = (row - col) * (1 - 2 * d)
    return diff >= 0, diff > 0


def _ada_kernel(c_ref, w_ref, b_ref, o_ref):
    c = c_ref[...]
    o_ref[...] = _bdot(c * _sigmoid(c), w_ref[...]) + b_ref[...]


def _ada(cvec, w, b, layer):
    rows, d = cvec.shape
    n = w.shape[2]
    tn = _pick(n, (1024, 512, 256, 128))
    return pl.pallas_call(
        _ada_kernel, name="ada",
        grid=(n // tn,),
        in_specs=[pl.BlockSpec((rows, d), lambda j: (0, 0)),
                  pl.BlockSpec((None, d, tn), lambda j: (layer, 0, j)),
                  pl.BlockSpec((None, 1, tn), lambda j: (layer, 0, j))],
        out_specs=pl.BlockSpec((rows, tn), lambda j: (0, j)),
        out_shape=jax.ShapeDtypeStruct((rows, n), F32),
        compiler_params=_params(("arbitrary",), 3 * d * tn * 4),
    )(cvec, w, b.reshape(b.shape[0], 1, n))


def _modnorm_kernel(x_ref, g_ref, shl_ref, scl_ref, shc_ref, scc_ref, o_ref, *, n_lat):
    tm = x_ref.shape[1]
    x = x_ref[0]
    y = x * lax.rsqrt(jnp.mean(x * x, axis=-1, keepdims=True) + NORM_EPS) * g_ref[...]
    pos = pl.program_id(1) * tm + lax.broadcasted_iota(jnp.int32, (tm, 1), 0)
    is_ctx = pos >= n_lat
    sc = jnp.where(is_ctx, scc_ref[0], scl_ref[0])
    sh = jnp.where(is_ctx, shc_ref[0], shl_ref[0])
    o_ref[0] = (y * (1.0 + sc) + sh).astype(o_ref.dtype)


def _modnorm(x, gain, shift, scale, n_lat):
    b, t, d = x.shape
    tm = _pick(t, (512, 384, 256, 128, 64))
    vec = lambda f: pl.BlockSpec((1, 1, d), f)
    return pl.pallas_call(
        functools.partial(_modnorm_kernel, n_lat=n_lat), name="modnorm",
        grid=(b, t // tm),
        in_specs=[pl.BlockSpec((1, tm, d), lambda bi, i: (bi, i, 0)),
                  pl.BlockSpec((1, d), lambda bi, i: (0, 0)),
                  vec(lambda bi, i: (bi, 0, 0)), vec(lambda bi, i: (bi, 0, 0)),
                  vec(lambda bi, i: (b, 0, 0)), vec(lambda bi, i: (b, 0, 0))],
        out_specs=pl.BlockSpec((1, tm, d), lambda bi, i: (bi, i, 0)),
        out_shape=jax.ShapeDtypeStruct((b, t, d), BF16),
        compiler_params=_params(("parallel", "arbitrary"), 6 * tm * d * 4),
    )(x, gain.reshape(1, d), shift, scale, shift, scale)


def _rmsnorm_kernel(x_ref, g_ref, o_ref):
    x = x_ref[0]
    o_ref[0] = x * lax.rsqrt(jnp.mean(x * x, axis=-1, keepdims=True) + NORM_EPS) * g_ref[...]


def _final_norm(x, gain, n_lat):
    b, _, d = x.shape
    tm = _pick(n_lat, (512, 384, 256, 128, 64))
    return pl.pallas_call(
        _rmsnorm_kernel, name="final_norm",
        grid=(b, n_lat // tm),
        in_specs=[pl.BlockSpec((1, tm, d), lambda bi, i: (bi, i, 0)),
                  pl.BlockSpec((1, d), lambda bi, i: (0, 0))],
        out_specs=pl.BlockSpec((1, tm, d), lambda bi, i: (bi, i, 0)),
        out_shape=jax.ShapeDtypeStruct((b, n_lat, d), F32),
        compiler_params=_params(("parallel", "arbitrary"), 6 * tm * d * 4),
    )(x, gain.reshape(1, d))


def _mm_kernel(x_ref, w_ref, o_ref):
    o_ref[0] = jnp.dot(x_ref[0], w_ref[...], preferred_element_type=F32).astype(o_ref.dtype)


def _tn_wide(n):
    return _pick(n, (1024, 512, 256, 128))


def _tn_ffn(n):
    return _pick(n, (512, 256, 128))


def _tk_of(k):
    return k if k <= 2048 else _pick(k, (2816, 2048, 1024, 512, 256, 128))


class _TiledW:
    def __init__(self, w, tn, tk=None):
        self.w = w.astype(BF16)
        self.tk = w.shape[1] if tk is None else tk
        self.tn = tn
        self.n = w.shape[2]


def _tile_w(w, tn, tk=None):
    return _TiledW(w, tn, tk)


def _cast_pad_kernel(x_ref, o_ref, *, n_in, cols):
    i = pl.program_id(1)

    @pl.when(i < n_in)
    def _():
        o_ref[0, :, :cols] = x_ref[0].astype(o_ref.dtype)
        if cols < o_ref.shape[2]:
            o_ref[0, :, cols:] = jnp.zeros((o_ref.shape[1], o_ref.shape[2] - cols), o_ref.dtype)

    @pl.when(i >= n_in)
    def _():
        o_ref[0] = jnp.zeros(o_ref.shape[1:], o_ref.dtype)


def _cast_pad(w, rows_out, cols_out):
    l, r, c = w.shape
    tr = _pick(math.gcd(r, rows_out), (256, 128, 64, 32, 16))
    n_in = r // tr
    return pl.pallas_call(
        functools.partial(_cast_pad_kernel, n_in=n_in, cols=c), name="cast_pad",
        grid=(l, rows_out // tr),
        in_specs=[pl.BlockSpec((1, tr, c), lambda li, i: (li, jnp.minimum(i, n_in - 1), 0))],
        out_specs=pl.BlockSpec((1, tr, cols_out), lambda li, i: (li, i, 0)),
        out_shape=jax.ShapeDtypeStruct((l, rows_out, cols_out), BF16),
        compiler_params=_params(("parallel", "arbitrary"), 2 * tr * (c * 4 + cols_out * 2)),
    )(w)


def _wspec(tk, tn, f):
    def idx(*a):
        lay, j, l = f(*a)[:3]
        return lay, l, j
    return pl.BlockSpec((None, tk, tn), idx)


_TM_ROWS = (1024, 768, 512, 384, 256, 128, 64)


def _matmul(x, wt, layer, j0, nj, out_dtype, rows=None):
    b, _, k = x.shape
    t = x.shape[1] if rows is None else rows
    tn = wt.tn
    n = nj * tn
    tm = _pick(t, _TM_ROWS)
    osz = jnp.dtype(out_dtype).itemsize
    return pl.pallas_call(
        _mm_kernel, name="in_proj",
        grid=(b, t // tm, n // tn),
        in_specs=[pl.BlockSpec((1, tm, k), lambda bi, i, j: (bi, i, 0)),
                  _wspec(k, tn, lambda bi, i, j: (layer, j0 + j, 0, 0, 0))],
        out_specs=pl.BlockSpec((1, tm, tn), lambda bi, i, j: (bi, i, j)),
        out_shape=jax.ShapeDtypeStruct((b, t, n), out_dtype),
        compiler_params=_params(("parallel", "parallel", "arbitrary"),
                                2 * (tm * k * 2 + k * tn * 2 + tm * tn * osz) + tm * tn * 4),
    )(x, wt.w)


def _swiglu_kernel(x_ref, wg_ref, wu_ref, o_ref):
    x = x_ref[0]
    g = jnp.dot(x, wg_ref[...], preferred_element_type=F32)
    u = jnp.dot(x, wu_ref[...], preferred_element_type=F32)
    o_ref[0] = (g * _sigmoid(g) * u).astype(o_ref.dtype)


def _swiglu_up(x, wg, wu, layer):
    b, t, k = x.shape
    tn = wg.tn
    n = wg.n
    tm = _pick(t, _TM_ROWS)
    wsp = lambda: _wspec(k, tn, lambda bi, i, j: (layer, j, 0, 0, 0))
    return pl.pallas_call(
        _swiglu_kernel, name="swiglu_up",
        grid=(b, t // tm, n // tn),
        in_specs=[pl.BlockSpec((1, tm, k), lambda bi, i, j: (bi, i, 0)), wsp(), wsp()],
        out_specs=pl.BlockSpec((1, tm, tn), lambda bi, i, j: (bi, i, j)),
        out_shape=jax.ShapeDtypeStruct((b, t, n), BF16),
        compiler_params=_params(("parallel", "parallel", "arbitrary"),
                                2 * (tm * k * 2 + 2 * k * tn * 2 + tm * tn * 2) + 3 * tm * tn * 4),
    )(x, wg.w, wu.w)


def _resid_kernel(x_ref, w_ref, r_ref, gl_ref, gc_ref, o_ref, acc_ref, *, n_lat):
    kk = pl.program_id(3)

    @pl.when(kk == 0)
    def _():
        acc_ref[...] = jnp.zeros_like(acc_ref)

    acc_ref[...] += jnp.dot(x_ref[0], w_ref[...], preferred_element_type=F32)

    @pl.when(kk == pl.num_programs(3) - 1)
    def _():
        tm = acc_ref.shape[0]
        pos = pl.program_id(1) * tm + lax.broadcasted_iota(jnp.int32, (tm, 1), 0)
        gate = jnp.where(pos >= n_lat, gc_ref[0], gl_ref[0])
        o_ref[0] = r_ref[0] + gate * acc_ref[...]


def _matmul_resid(x, w, layer, res, gate, n_lat):
    b, t, k = x.shape
    tk, tn = w.tk, w.tn
    n = w.n
    tm = _pick(t, _TM_ROWS)
    return pl.pallas_call(
        functools.partial(_resid_kernel, n_lat=n_lat), name="proj_resid",
        grid=(b, t // tm, n // tn, k // tk),
        in_specs=[pl.BlockSpec((1, tm, tk), lambda bi, i, j, l: (bi, i, l)),
                  _wspec(tk, tn, lambda bi, i, j, l: (layer, j, l, 0, 0)),
                  pl.BlockSpec((1, tm, tn), lambda bi, i, j, l: (bi, i, j)),
                  pl.BlockSpec((1, 1, tn), lambda bi, i, j, l: (bi, 0, j)),
                  pl.BlockSpec((1, 1, tn), lambda bi, i, j, l: (b, 0, j))],
        out_specs=pl.BlockSpec((1, tm, tn), lambda bi, i, j, l: (bi, i, j)),
        out_shape=jax.ShapeDtypeStruct((b, t, n), F32),
        scratch_shapes=[pltpu.VMEM((tm, tn), F32)],
        compiler_params=_params(("parallel", "parallel", "arbitrary", "arbitrary"),
                                2 * (tm * tk * 2 + tk * tn * 2 + 2 * tm * tn * 4) + 2 * tm * tn * 4),
    )(x, w.w, res, gate, gate)


def _merge_kernel(oa_ref, ob_ref, oc_ref, wa_ref, wb_ref, wc_ref, ga_ref, gb_ref, gc_ref, o_ref):
    y = _sigmoid(ga_ref[0].astype(F32)) * jnp.dot(oa_ref[0], wa_ref[...], preferred_element_type=F32)
    y += _sigmoid(gb_ref[0].astype(F32)) * jnp.dot(ob_ref[0], wb_ref[...], preferred_element_type=F32)
    y += _sigmoid(gc_ref[0].astype(F32)) * jnp.dot(oc_ref[0], wc_ref[...], preferred_element_type=F32)
    o_ref[0] = y.astype(o_ref.dtype)


def _merge(oa, ob, oc, wa, wb, wc, layer, pgate):
    b, _, kw = oa.shape
    t = pgate.shape[1]
    tn = wa.tn
    nj = wa.n // tn
    d = nj * tn
    tm = _pick(t, (768, 512, 384, 256, 128, 64))
    br = lambda: pl.BlockSpec((1, tm, kw), lambda bi, i, j: (bi, i, 0))
    wt = lambda: _wspec(kw, tn, lambda bi, i, j: (layer, j, 0, 0, 0))
    gt = lambda s: pl.BlockSpec((1, tm, tn), lambda bi, i, j: (bi, i, s * nj + j))
    gsz = jnp.dtype(pgate.dtype).itemsize
    return pl.pallas_call(
        _merge_kernel, name="merge",
        grid=(b, t // tm, nj),
        in_specs=[br(), br(), br(), wt(), wt(), wt(), gt(0), gt(1), gt(2)],
        out_specs=pl.BlockSpec((1, tm, tn), lambda bi, i, j: (bi, i, j)),
        out_shape=jax.ShapeDtypeStruct((b, t, d), BF16),
        compiler_params=_params(("parallel", "parallel", "arbitrary"),
                                2 * (3 * tm * kw * 2 + 3 * kw * tn * 2 + 3 * tm * tn * gsz + tm * tn * 2)
                                + 4 * tm * tn * 4),
    )(oa, ob, oc, wa.w, wb.w, wc.w, pgate, pgate, pgate)


def _pair_sum_matrix():
    r = lax.broadcasted_iota(jnp.int32, (LANE, LANE), 0) // RW_HEAD
    c = lax.broadcasted_iota(jnp.int32, (LANE, LANE), 1) // RW_HEAD
    return jnp.where(r == c, 1.0, 0.0).astype(BF16)


def _head_sums(x, hs):
    rows, w = x.shape
    nt = w // LANE
    stacked = jnp.concatenate([x[:, j * LANE:(j + 1) * LANE] for j in range(nt)], axis=0)
    s = _x_mask_dot(stacked, hs)
    return jnp.concatenate([s[j * rows:(j + 1) * rows] for j in range(nt)], axis=1)


def _rw_prep_kernel(prev_ref, cur_ref, next_ref, grp_ref, mu_ref, w0_ref, wup_ref, a0_ref, aup_ref,
                    gup_ref, kk_ref, ka_ref, rk_ref,
                    r_ref, v_ref, kn_ref, g_ref, bon_ref, lw_ref, kd_ref, bb_ref,
                    *, n_lat_chunks, n_chunks, w, ld, la, lg):
    i = pl.program_id(1)
    c = CHUNK
    cur = cur_ref[0].astype(F32)
    prev = prev_ref[0].astype(F32)
    nxt = next_ref[0].astype(F32)
    is_ctx = i >= n_lat_chunks
    first = jnp.logical_or(i == 0, i == n_lat_chunks)
    last = jnp.logical_or(i == n_lat_chunks - 1, i == n_chunks - 1)
    row = lax.broadcasted_iota(jnp.int32, (c, 1), 0)
    carry_in = jnp.where(jnp.logical_and(is_ctx, jnp.logical_not(first)), 1.0, 0.0)
    carry_out = jnp.where(jnp.logical_and(is_ctx, jnp.logical_not(last)), 1.0, 0.0)
    tm1 = jnp.where(row == 0, prev[c - 1:c] * carry_in, pltpu.roll(cur, 1, 0))
    tp1 = jnp.where(row == c - 1, nxt[0:1] * carry_out, pltpu.roll(cur, c - 1, 0))
    lat_up = jnp.where(jnp.logical_or(is_ctx, first), 0.0, 1.0)
    lat_dn = jnp.where(jnp.logical_or(is_ctx, last), 0.0, 1.0)
    code = jnp.where(is_ctx, grp_ref[1:2], grp_ref[0:1])
    shifted = jnp.where(code == 0, tm1,
                        jnp.where(code == 1, tp1,
                                  jnp.where(code == 2, prev * lat_up, nxt * lat_dn)))
    m = cur + mu_ref[...] * (shifted - cur)

    r = m[:, 0:w]
    k = m[:, w:2 * w]
    v = m[:, 2 * w:3 * w]
    o = 3 * w
    wd = (m[:, o:o + ld], m[:, o + ld:o + 2 * ld])
    ad = (m[:, o + 2 * ld:o + 2 * ld + la], m[:, o + 2 * ld + la:o + 2 * ld + 2 * la])
    gd = m[:, o + 2 * ld + 2 * la:o + 2 * ld + 2 * la + lg]

    hs = _pair_sum_matrix()
    g_ref[0] = _bdot(_sigmoid(gd), gup_ref[...]).astype(g_ref.dtype)
    kx = k * kk_ref[...]
    kn = kx * lax.rsqrt(jnp.maximum(_head_sums(kx * kx, hs), 1e-24))
    r_ref[0] = r.astype(r_ref.dtype)
    v_ref[0] = v.astype(v_ref.dtype)
    kn_ref[0] = kn.astype(kn_ref.dtype)
    bonus = jnp.zeros_like(r)
    for d in range(2):
        wl = w0_ref[d] + _bdot(jnp.tanh(wd[d]), wup_ref[d])
        lw_ref[d, 0] = -math.exp(-0.5) * _sigmoid(wl)
        a = _sigmoid(a0_ref[d] + _bdot(ad[d], aup_ref[d]))
        kd = k * (1.0 + (a - 1.0) * ka_ref[...])
        kd_ref[d, 0] = kd.astype(kd_ref.dtype)
        bb_ref[d, 0] = (kn * a).astype(bb_ref.dtype)
        bonus += r * kd * rk_ref[...]
    bon_ref[0] = (_head_sums(bonus, hs) * v).astype(bon_ref.dtype)


def _rw_prep(pa, grp, mu, w0, wup, a0, aup, gup, k_k, k_a, r_k, n_lat, dims):
    b, t, na = pa.shape
    w, ld, la, lg = dims
    nc = t // CHUNK
    nlc = n_lat // CHUNK
    blk = lambda f: pl.BlockSpec((1, CHUNK, na), f)
    full = lambda a: pl.BlockSpec(a.shape, lambda bi, i: (0,) * a.ndim)
    o1 = pl.BlockSpec((1, CHUNK, w), lambda bi, i: (bi, i, 0))
    o2 = pl.BlockSpec((2, 1, CHUNK, w), lambda bi, i: (0, bi, i, 0))
    s1 = jax.ShapeDtypeStruct((b, t, w), BF16)
    s2 = jax.ShapeDtypeStruct((2, b, t, w), BF16)
    s2f = jax.ShapeDtypeStruct((2, b, t, w), F32)
    consts = (grp, mu, w0, wup, a0, aup, gup, k_k, k_a, r_k)
    return pl.pallas_call(
        functools.partial(_rw_prep_kernel, n_lat_chunks=nlc, n_chunks=nc, w=w, ld=ld, la=la, lg=lg),
        name="rw_prep",
        grid=(b, nc),
        in_specs=[blk(lambda bi, i: (bi, jnp.maximum(i - 1, 0), 0)),
                  blk(lambda bi, i: (bi, i, 0)),
                  blk(lambda bi, i: (bi, jnp.minimum(i + 1, nc - 1), 0))] + [full(a) for a in consts],
        out_specs=[o1, o1, o1, o1, o1, o2, o2, o2],
        out_shape=[s1, s1, s1, s1, s1, s2f, s2, s2],
        compiler_params=_params(("parallel", "arbitrary"), 32 << 20),
    )(pa, pa, pa, *consts)


def _pdot(a, b):
    return lax.dot_general(a.astype(BF16), b.astype(BF16), (((2,), (1,)), ((0,), (0,))),
                           preferred_element_type=F32)


def _pdot_nt(a, b):
    return lax.dot_general(a.astype(BF16), b.astype(BF16), (((2,), (2,)), ((0,), (0,))),
                           preferred_element_type=F32)


def _pdot_tn(a, b):
    return lax.dot_general(a.astype(BF16), b.astype(BF16), (((1,), (1,)), ((0,), (0,))),
                           preferred_element_type=F32)


def _rw_scan_dir(d, r_ref, v_ref, kn_ref, lw_ref, kd_ref, bb_ref, y_ref, s_ref, n_pairs):
    c = CHUNK
    wfull = n_pairs * LANE
    incl_bf = jnp.where(_order_masks(d, c)[0], 1.0, 0.0).astype(BF16)
    incl2, strict2 = _order_masks(d, c, 2)
    lane_w = lax.broadcasted_iota(jnp.int32, (1, wfull), 1) % LANE
    w0 = jnp.where(lane_w < RW_HEAD, 1.0, 0.0)
    w1 = 1.0 - w0
    lane = lax.broadcasted_iota(jnp.int32, (1, 1, LANE), 2)
    m0 = jnp.where(lane < RW_HEAD, 1.0, 0.0)
    m1 = 1.0 - m0
    rowh = lax.broadcasted_iota(jnp.int32, (LANE, LANE), 0) // RW_HEAD
    colh = lax.broadcasted_iota(jnp.int32, (LANE, LANE), 1) // RW_HEAD
    blockdiag = rowh == colh
    last_row = jnp.where(d == 0, c - 1, 0)
    rsel = lax.broadcasted_iota(jnp.int32, (c, 1), 0) == last_row

    def pairs(x):
        return jnp.stack([x[:, p * LANE:(p + 1) * LANE] for p in range(n_pairs)], axis=0)

    def stack(x):
        return jnp.concatenate([x * m0, x * m1], axis=1)

    lw = lw_ref[0, 0]
    r = r_ref[0].astype(F32)
    v = v_ref[0].astype(F32)
    kd = kd_ref[0, 0].astype(F32)
    bb = bb_ref[0, 0].astype(F32)
    cw = _mask_dot(incl_bf, lw)
    tot = jnp.sum(jnp.where(rsel, cw, 0.0), axis=0, keepdims=True)
    w_inv = jnp.exp(-cw)
    w_rem = jnp.exp(tot - cw)
    a_t = -kn_ref[0].astype(F32) * jnp.exp(cw - lw)
    r_t = r * jnp.exp(cw)
    b_t = bb * w_inv
    k_t = kd * w_inv
    ar = pairs(jnp.concatenate([a_t, r_t], axis=0))
    bk = pairs(jnp.concatenate([b_t * w0, b_t * w1, k_t * w0, k_t * w1], axis=0))
    vst = pairs(jnp.concatenate([v * w0, v * w1], axis=0))
    uvr = pairs(jnp.concatenate([bb * w_rem, kd * w_rem], axis=0))
    v3 = pairs(v)
    decay = pairs(jnp.exp(tot))
    s0 = s_ref[...]
    yield

    sc = _pdot_nt(ar, bk)
    a_ab = jnp.where(strict2, sc[:, :c, :2 * c], 0.0)
    a_ak = jnp.where(strict2, sc[:, :c, 2 * c:], 0.0)
    r_b = jnp.where(incl2, sc[:, c:, :2 * c], 0.0)
    r_k = jnp.where(incl2, sc[:, c:, 2 * c:], 0.0)

    abd = stack(a_ab)
    yield
    q = _pdot(abd, abd)
    nsum = abd
    for _ in range(4):
        yield
        both = _pdot(q, jnp.concatenate([nsum, q], axis=2))
        nsum = nsum + q + both[:, :, :LANE]
        q = both[:, :, LANE:]
    yield
    nsum = nsum + q + _pdot(q, nsum)
    n_side = nsum[:, :c] + nsum[:, c:]

    yield
    art = _pdot_nt(ar, s0)
    rhs = art[:, :c] + _pdot(a_ak, vst)
    yield
    u = rhs + _pdot(n_side, stack(rhs))
    yield
    y = art[:, c:] + _pdot(jnp.concatenate([r_b, r_k], axis=2), jnp.concatenate([stack(u), vst], axis=1))
    for p in range(n_pairs):
        y_ref[0, :, p * LANE:(p + 1) * LANE] = y[p]
    yield
    upd = _pdot_tn(jnp.concatenate([u, v3], axis=1), uvr)
    s_ref[...] = s0 * decay + jnp.where(blockdiag, upd, 0.0)


def _scan_chunk(d, i, n_lat_chunks, n_chunks):
    nctx = n_chunks - n_lat_chunks
    in_ctx = i < nctx
    fwd = jnp.where(in_ctx, n_lat_chunks + i, i - nctx)
    bwd = jnp.where(in_ctx, n_chunks - 1 - i, n_chunks - 1 - i)
    return jnp.where(d == 0, fwd, bwd)


def _rw_finish_kernel(yf_ref, yb_ref, bon_ref, g_ref, lg_ref, lb_ref, o_ref):
    y = yf_ref[0] + yb_ref[0]
    hs = _pair_sum_matrix()
    mean = _head_sums(y, hs) * (1.0 / RW_HEAD)
    yc = y - mean
    var = _head_sums(yc * yc, hs) * (1.0 / RW_HEAD)
    yn = yc * lax.rsqrt(var + RW_GN_EPS)
    o_ref[0] = ((yn * lg_ref[...] + lb_ref[...] + bon_ref[0].astype(F32)) * g_ref[0].astype(F32)).astype(o_ref.dtype)


def _rw_finish(y_f, y_b, bonus, g, ln_g, ln_b):
    b, t, w = y_f.shape
    tm = _pick(t, _TM_ROWS)
    s1 = pl.BlockSpec((1, tm, w), lambda bi, i: (bi, i, 0))
    vec = pl.BlockSpec((1, w), lambda bi, i: (0, 0))
    return pl.pallas_call(
        _rw_finish_kernel, name="rw_finish",
        grid=(b, t // tm),
        in_specs=[s1, s1, s1, s1, vec, vec],
        out_specs=s1,
        out_shape=jax.ShapeDtypeStruct((b, t, w), BF16),
        compiler_params=_params(("parallel", "arbitrary"), 32 << 20),
    )(y_f, y_b, bonus, g, ln_g.reshape(1, w), ln_b.reshape(1, w))


def _hg_scan_dir(d, q_ref, z_ref, i_ref, lb_ref, o_ref, s_ref, n_heads):
    c = CHUNK
    nb = c // SUB
    sgn = 1 - 2 * d
    fwd = d == 0
    incl_bf = jnp.where(_order_masks(d, c)[0], 1.0, 0.0).astype(BF16)
    row = lax.broadcasted_iota(jnp.int32, (c, c), 0)
    col = lax.broadcasted_iota(jnp.int32, (c, c), 1)
    rb, cbk = row // SUB, col // SUB
    lvl_b = (rb - cbk) * sgn > 0
    lvl_1 = jnp.logical_and(rb == cbk, (row - col) * sgn >= 0)
    last_row = jnp.where(d == 0, c - 1, 0)
    rsel = lax.broadcasted_iota(jnp.int32, (c, 1), 0) == last_row

    def heads(x):
        return jnp.stack([x[:, h * LANE:(h + 1) * LANE] for h in range(n_heads)], axis=0)

    q = q_ref[0]
    z = z_ref[0]
    val = i_ref[0]
    lg_l = lb_ref[d, 0:1]
    lg_1ml = lb_ref[d, 1:2]
    one_ml = lb_ref[d, 2:3]
    s0 = s_ref[...]

    ez = jnp.exp(-jnp.abs(z))
    log_sig = jnp.minimum(z, 0.0) - jnp.log(1.0 + ez)
    x2 = lg_1ml + log_sig
    mx = jnp.maximum(lg_l, x2)
    log_f = mx + jnp.log(jnp.exp(lg_l - mx) + jnp.exp(x2 - mx))
    k = one_ml * jnp.where(z >= 0.0, ez, 1.0) / (1.0 + ez)
    yield

    cb = _mask_dot(incl_bf, log_f)
    tot = jnp.sum(jnp.where(rsel, cb, 0.0), axis=0, keepdims=True)
    wfull = cb.shape[1]
    cb3 = cb.reshape(nb, SUB, wfull)
    blk_end = cb3[:, SUB - 1:SUB] if fwd else cb3[:, 0:1]

    end_own = jnp.broadcast_to(blk_end, (nb, SUB, wfull)).reshape(c, wfull)
    refs = [(blk_end[j], slice(0, c)) for j in range(nb)]
    refs += [(jnp.broadcast_to(cb3[:, j:j + 1], (nb, SUB, wfull)).reshape(c, wfull), slice(c, 2 * c))
             for j in range(SUB)]
    which = jnp.where(lvl_b, cbk, jnp.where(lvl_1, nb + col % SUB, -1))
    ks = heads(jnp.concatenate([k * jnp.exp(end_own - cb), k], axis=0))
    q_bf = q.astype(BF16)
    att = jnp.zeros((n_heads, c, c), F32)
    group = 4
    for g0 in range(0, len(refs), group):
        yield
        part = refs[g0:g0 + group]
        qs = heads(jnp.concatenate([q_bf * jnp.exp(jnp.minimum(cb - e, 0.0)).astype(BF16) for e, _ in part],
                                   axis=0))
        sc = _pdot_nt(qs, ks)
        for n, (_, cols) in enumerate(part):
            att = jnp.where(which == g0 + n, sc[:, n * c:(n + 1) * c, cols], att)
    yield
    v3 = heads(val)
    o = _pdot(att, v3) + _pdot_nt(heads(q * jnp.exp(cb)), s0)
    for h in range(n_heads):
        o_ref[0, :, h * LANE:(h + 1) * LANE] = o[h]
    yield
    s_ref[...] = s0 * heads(jnp.exp(tot)) + _pdot_tn(v3, heads(k * jnp.exp(tot - cb)))


def _scans_kernel(*refs, n_pairs, n_heads):
    rw_in, hg_in = refs[0:12], refs[12:19]
    y_f, y_b, o_f, o_b, s_rw, s_hg = refs[19:25]

    @pl.when(pl.program_id(1) == 0)
    def _():
        s_rw[...] = jnp.zeros_like(s_rw)
        s_hg[...] = jnp.zeros_like(s_hg)

    live = [_rw_scan_dir(0, *rw_in[0:6], y_f, s_rw.at[0], n_pairs),
            _hg_scan_dir(0, *hg_in[0:3], hg_in[6], o_f, s_hg.at[0], n_heads),
            _rw_scan_dir(1, *rw_in[6:12], y_b, s_rw.at[1], n_pairs),
            _hg_scan_dir(1, *hg_in[3:6], hg_in[6], o_b, s_hg.at[1], n_heads)]
    while live:
        for gen in list(live):
            if next(gen, StopIteration) is StopIteration:
                live.remove(gen)


def _scans(r, v, kn, lw, kd, bb, phg, lbt, n_lat):
    b, t, w = r.shape
    nc = t // CHUNK
    nlc = n_lat // CHUNK
    ch = functools.partial(_scan_chunk, n_lat_chunks=nlc, n_chunks=nc)
    col = lambda d, cb_: pl.BlockSpec((1, CHUNK, w), lambda bi, i: (bi, ch(d, i), cb_))
    s2 = lambda d: pl.BlockSpec((1, 1, CHUNK, w), lambda bi, i: (d, bi, ch(d, i), 0))
    rw_dir = lambda d: [col(d, 0), col(d, 0), col(d, 0), s2(d), s2(d), s2(d)]
    out = jax.ShapeDtypeStruct((b, t, w), F32)
    return pl.pallas_call(
        functools.partial(_scans_kernel, n_pairs=w // LANE, n_heads=w // LANE), name="scans",
        grid=(b, nc),
        in_specs=rw_dir(0) + rw_dir(1)
        + [col(0, 0), col(0, 1), col(0, 3), col(1, 0), col(1, 2), col(1, 3),
           pl.BlockSpec((2, 3, w), lambda bi, i: (0, 0, 0))],
        out_specs=[col(0, 0), col(1, 0), col(0, 0), col(1, 0)],
        out_shape=[out, out, out, out],
        scratch_shapes=[pltpu.VMEM((2, w // LANE, LANE, LANE), F32), pltpu.VMEM((2, w // LANE, LANE, LANE), F32)],
        compiler_params=_params(("arbitrary", "arbitrary"), 40 << 20),
    )(r, v, kn, lw, kd, bb, r, v, kn, lw, kd, bb, phg, phg, phg, phg, phg, phg, lbt)


def _hg_finish_kernel(of_ref, ob_ref, g_ref, ng_ref, out_ref, *, n_heads):
    for h in range(n_heads):
        sl = slice(h * LANE, (h + 1) * LANE)
        o = of_ref[0, :, sl] + ob_ref[0, :, sl]
        y = o * lax.rsqrt(jnp.mean(o * o, axis=-1, keepdims=True) + NORM_EPS) * ng_ref[...]
        g = g_ref[0, :, sl]
        out_ref[0, :, sl] = (y * g * _sigmoid(g)).astype(out_ref.dtype)


def _hg_finish(o_f, o_b, phg, norm_g, w):
    b, t, _ = o_f.shape
    tm = _pick(t, _TM_ROWS)
    return pl.pallas_call(
        functools.partial(_hg_finish_kernel, n_heads=w // LANE), name="hg_finish",
        grid=(b, t // tm),
        in_specs=[pl.BlockSpec((1, tm, w), lambda bi, i: (bi, i, 0)),
                  pl.BlockSpec((1, tm, w), lambda bi, i: (bi, i, 0)),
                  pl.BlockSpec((1, tm, w), lambda bi, i: (bi, i, 4)),
                  pl.BlockSpec((1, LANE), lambda bi, i: (0, 0))],
        out_specs=pl.BlockSpec((1, tm, w), lambda bi, i: (bi, i, 0)),
        out_shape=jax.ShapeDtypeStruct((b, t, w), BF16),
        compiler_params=_params(("parallel", "arbitrary"), 32 << 20),
    )(o_f, o_b, phg, norm_g.reshape(1, LANE))


def _hy_pre_kernel(prev_ref, cur_ref, next_ref, cw_ref, cb_ref, x0_ref, z_ref, *, n_lat_blocks, n_blocks, w):
    i = pl.program_id(1)
    tm = cur_ref.shape[1]
    hb = prev_ref.shape[1]
    cur = cur_ref[0].astype(F32)
    first = jnp.logical_or(i == 0, i == n_lat_blocks)
    last = jnp.logical_or(i == n_lat_blocks - 1, i == n_blocks - 1)
    row = lax.broadcasted_iota(jnp.int32, (tm, 1), 0)
    p_last = prev_ref[0, hb - 1:hb].astype(F32) * jnp.where(first, 0.0, 1.0)
    n_first = next_ref[0, 0:1].astype(F32) * jnp.where(last, 0.0, 1.0)
    before = jnp.where(row == 0, p_last, pltpu.roll(cur, 1, 0))
    after = jnp.where(row == tm - 1, n_first, pltpu.roll(cur, tm - 1, 0))
    u = cw_ref[0:1] * before + cw_ref[1:2] * cur + cw_ref[2:3] * after + cb_ref[...]
    x0_ref[0] = u[:, 0:w].astype(x0_ref.dtype)
    z_ref[0] = (u[:, w:2 * w] * u[:, 2 * w:3 * w]).astype(z_ref.dtype)


def _hy_pre(phy, conv_w, conv_b, n_lat):
    b, t, w3 = phy.shape
    w = w3 // 3
    tm = _pick(math.gcd(n_lat, t - n_lat), (256, 128, 64))
    nb = t // tm
    hb = 16
    per = tm // hb
    o = pl.BlockSpec((1, tm, w), lambda bi, i: (bi, i, 0))
    s = jax.ShapeDtypeStruct((b, t, w), BF16)
    return pl.pallas_call(
        functools.partial(_hy_pre_kernel, n_lat_blocks=n_lat // tm, n_blocks=nb, w=w), name="hy_pre",
        grid=(b, nb),
        in_specs=[pl.BlockSpec((1, hb, w3), lambda bi, i: (bi, jnp.maximum(i * per - 1, 0), 0)),
                  pl.BlockSpec((1, tm, w3), lambda bi, i: (bi, i, 0)),
                  pl.BlockSpec((1, hb, w3), lambda bi, i: (bi, jnp.minimum((i + 1) * per, t // hb - 1), 0)),
                  pl.BlockSpec((3, w3), lambda bi, i: (0, 0)),
                  pl.BlockSpec((1, w3), lambda bi, i: (0, 0))],
        out_specs=[o, o],
        out_shape=[s, s],
        compiler_params=_params(("parallel", "arbitrary"), 40 << 20),
    )(phy, phy, phy, conv_w, conv_b.reshape(1, w3))


def _hdot(a, b):
    return jnp.dot(a, b, precision=lax.Precision.HIGHEST, preferred_element_type=F32)


def _hy_taps_kernel(ft_ref, w1_ref, b1_ref, w2_ref, b2_ref, w3f_ref, w3b_ref, fr_ref, dl_ref,
                    tap_ref, sum_ref):
    i = pl.program_id(1)
    ft = ft_ref[...]
    fr = fr_ref[...]
    h = jnp.sin(fr * (_hdot(ft, w1_ref[...]) + b1_ref[...]))
    h = jnp.sin(fr * (_hdot(h, w2_ref[...]) + b2_ref[...]))
    cf = ft[:, HY_EMB:HY_EMB + 1]
    cb = ft[:, HY_EMB + 1:HY_EMB + 2]
    tt = ft[:, 0:1]
    tap = (cf * _hdot(h, w3f_ref[...]) + cb * _hdot(h, w3b_ref[...])) * jnp.exp(-tt * dl_ref[...])
    tap_ref[...] = tap

    @pl.when(i == 0)
    def _():
        sum_ref[...] = jnp.zeros_like(sum_ref)

    sum_ref[...] += jnp.sum(jnp.abs(tap), axis=0, keepdims=True)


def _hy_taps(feats, w1p, b1, w2, b2, w3, freq, deltas):
    rws = feats.shape[0]
    hid = w2.shape[0]
    w = w3.shape[1] // 2
    tr = _pick(rws, (1024, 512, 256, 128))
    ct = w
    nj = w // ct
    c2 = lambda a: pl.BlockSpec(a.shape, lambda j, i: (0, 0))
    return pl.pallas_call(
        _hy_taps_kernel, name="hy_taps",
        grid=(nj, rws // tr),
        in_specs=[pl.BlockSpec((tr, LANE), lambda j, i: (i, 0)),
                  c2(w1p), pl.BlockSpec((1, hid), lambda j, i: (0, 0)),
                  c2(w2), pl.BlockSpec((1, hid), lambda j, i: (0, 0)),
                  pl.BlockSpec((hid, ct), lambda j, i: (0, j)),
                  pl.BlockSpec((hid, ct), lambda j, i: (0, nj + j)),
                  pl.BlockSpec((1, hid), lambda j, i: (0, 0)),
                  pl.BlockSpec((1, ct), lambda j, i: (0, j))],
        out_specs=[pl.BlockSpec((tr, ct), lambda j, i: (i, j)),
                   pl.BlockSpec((1, ct), lambda j, i: (0, j))],
        out_shape=[jax.ShapeDtypeStruct((rws, w), F32), jax.ShapeDtypeStruct((1, w), F32)],
        compiler_params=_params(("parallel", "arbitrary"), 32 << 20),
    )(feats, w1p, b1.reshape(1, hid), w2, b2.reshape(1, hid), w3, w3, freq.reshape(1, hid),
      deltas.reshape(1, w))


def _dft1_kernel(g_ref, x_ref, o_ref):
    o_ref[...] = lax.dot_general(g_ref[...], x_ref[...].astype(BF16), (((2,), (1,)), ((0,), (0,))),
                                 preferred_element_type=F32).astype(o_ref.dtype)


def _dft1(g, x):
    nb, m2, k = g.shape
    c = x.shape[2]
    bt = _pick(nb, (8, 4, 2))
    ct = _pick(c, (256, 128))
    return pl.pallas_call(
        _dft1_kernel, name="dft1",
        grid=(c // ct, nb // bt),
        in_specs=[pl.BlockSpec((bt, m2, k), lambda j, i: (i, 0, 0)),
                  pl.BlockSpec((bt, k, ct), lambda j, i: (i, 0, j))],
        out_specs=pl.BlockSpec((bt, m2, ct), lambda j, i: (i, 0, j)),
        out_shape=jax.ShapeDtypeStruct((nb, m2, c), BF16),
        compiler_params=_params(("parallel", "arbitrary"), 32 << 20),
    )(g, x)


def _spec_kernel(a_ref, f2_ref, sc_ref, o_ref):
    for l in range(a_ref.shape[0]):
        o_ref[l] = jnp.dot(f2_ref[...], a_ref[l].astype(BF16), preferred_element_type=F32) * sc_ref[...]


def _spec(a, f2, scale):
    na, m2, c = a.shape
    kt = _pick(na, (8, 4, 2))
    ct = _pick(c, (256, 128))
    return pl.pallas_call(
        _spec_kernel, name="hy_spec",
        grid=(c // ct, na // kt),
        in_specs=[pl.BlockSpec((kt, m2, ct), lambda j, i: (i, 0, j)),
                  pl.BlockSpec((m2, m2), lambda j, i: (0, 0)),
                  pl.BlockSpec((1, ct), lambda j, i: (0, j))],
        out_specs=pl.BlockSpec((kt, m2, ct), lambda j, i: (i, 0, j)),
        out_shape=jax.ShapeDtypeStruct((na, m2, c), F32),
        compiler_params=_params(("parallel", "arbitrary"), 32 << 20),
    )(a, f2, scale)


def _conv_mid_kernel(a_ref, f2_ref, h_ref, g3_ref, o_ref):
    nbh = a_ref.shape[1] // 2
    for l in range(a_ref.shape[0]):
        x = jnp.dot(f2_ref[...], a_ref[l].astype(BF16), preferred_element_type=F32)
        xr, xi = x[:nbh], x[nbh:]
        hr, hi = h_ref[l, :nbh], h_ref[l, nbh:]
        y = jnp.concatenate([xr * hr - xi * hi, xr * hi + xi * hr], axis=0)
        o_ref[l] = jnp.dot(g3_ref[l], y.astype(BF16), preferred_element_type=F32).astype(o_ref.dtype)


def _conv_mid(a, f2, h, g3):
    na, m2, c = a.shape
    kt = _pick(na, (8, 4, 2))
    ct = _pick(c, (256, 128))
    slab = pl.BlockSpec((kt, m2, ct), lambda j, i: (i, 0, j))
    return pl.pallas_call(
        _conv_mid_kernel, name="conv_mid",
        grid=(c // ct, na // kt),
        in_specs=[slab, pl.BlockSpec((m2, m2), lambda j, i: (0, 0)), slab,
                  pl.BlockSpec((kt, m2, m2), lambda j, i: (i, 0, 0))],
        out_specs=slab,
        out_shape=jax.ShapeDtypeStruct((na, m2, c), BF16),
        compiler_params=_params(("parallel", "arbitrary"), 32 << 20),
    )(a, f2, h, g3)


def _conv_out_kernel(b_ref, f4_ref, o_ref, *, scale):
    for l in range(b_ref.shape[0]):
        o_ref[l] = jnp.dot(f4_ref[...], b_ref[l].astype(BF16), preferred_element_type=F32) * scale


def _conv_out(bm, f4, scale):
    nb, m2, c = bm.shape
    mo = f4.shape[0]
    pt = _pick(nb, (8, 4, 2))
    ct = _pick(c, (256, 128))
    return pl.pallas_call(
        functools.partial(_conv_out_kernel, scale=scale), name="conv_out",
        grid=(c // ct, nb // pt),
        in_specs=[pl.BlockSpec((pt, m2, ct), lambda j, i: (i, 0, j)),
                  pl.BlockSpec((mo, m2), lambda j, i: (0, 0))],
        out_specs=pl.BlockSpec((pt, mo, ct), lambda j, i: (i, 0, j)),
        out_shape=jax.ShapeDtypeStruct((nb, mo, c), F32),
        compiler_params=_params(("parallel", "arbitrary"), 32 << 20),
    )(bm, f4)


def _ctx_conv_kernel(z_ref, ext_ref, sc_ref, o_ref):
    n = z_ref.shape[1]

    def body(s, acc):
        return acc + ext_ref[pl.ds(n - s, n), :] * z_ref[0, pl.ds(s, 1), :]

    acc = lax.fori_loop(0, n, body, jnp.zeros(o_ref.shape[1:], F32))
    o_ref[0] = acc * sc_ref[...]


def _ctx_conv(z, ext, scale):
    b, n, w = z.shape
    ct = LANE
    return pl.pallas_call(
        _ctx_conv_kernel, name="ctx_conv",
        grid=(b, w // ct),
        in_specs=[pl.BlockSpec((1, n, ct), lambda bi, j: (bi, 0, j)),
                  pl.BlockSpec((2 * n, ct), lambda bi, j: (0, j)),
                  pl.BlockSpec((1, ct), lambda bi, j: (0, j))],
        out_specs=pl.BlockSpec((1, n, ct), lambda bi, j: (bi, 0, j)),
        out_shape=jax.ShapeDtypeStruct((b, n, w), F32),
        compiler_params=_params(("parallel", "arbitrary"), 16 << 20),
    )(z, ext, scale)


def _hy_post_kernel(y_ref, z_ref, x0_ref, bias_ref, o_ref):
    o_ref[0] = (x0_ref[0].astype(F32) * (y_ref[0] + bias_ref[...] * z_ref[0].astype(F32))).astype(o_ref.dtype)


def _hy_post(y, z, x0, bias):
    b, t, w = y.shape
    tm = _pick(t, _TM_ROWS)
    s = pl.BlockSpec((1, tm, w), lambda bi, i: (bi, i, 0))
    return pl.pallas_call(
        _hy_post_kernel, name="hy_post",
        grid=(b, t // tm),
        in_specs=[s, s, s, pl.BlockSpec((1, w), lambda bi, i: (0, 0))],
        out_specs=s,
        out_shape=jax.ShapeDtypeStruct((b, t, w), BF16),
        compiler_params=_params(("parallel", "arbitrary"), 32 << 20),
    )(y, z, x0, bias.reshape(1, w))


def _dft_factors(n):
    m = 2 * n
    na = 1 << ((m.bit_length() - 1) // 2)
    return na, m // na


def _cis(num, den, sign):
    ang = (num % den).astype(F32) * (2.0 * math.pi / den)
    return jnp.cos(ang), sign * jnp.sin(ang)


def _cblock(cr, ci):
    return jnp.concatenate([jnp.concatenate([cr, -ci], axis=-1),
                            jnp.concatenate([ci, cr], axis=-1)], axis=-2)


def _dft_tables(n):
    na, nb = _dft_factors(n)
    m = na * nb
    ah = na // 2
    ar = lambda k: jnp.arange(k, dtype=jnp.int32)
    ka, bb = ar(na)[None, :, None], ar(nb)[:, None, None]
    g1c = _cblock(*_cis(ka * (nb * ar(ah)[None, None, :] + bb), m, -1.0))
    g1t = jnp.concatenate(_cis(ka * (nb * ar(na)[None, None, :] + bb), m, -1.0), axis=-2)
    f2 = _cblock(*_cis(ar(nb)[:, None] * ar(nb)[None, :], nb, -1.0))
    g3 = _cblock(*_cis(ar(nb)[None, :, None] * (ar(na)[:, None, None] + na * ar(nb)[None, None, :]), m, 1.0))
    f4 = _cblock(*_cis(ar(ah)[:, None] * ar(na)[None, :], na, 1.0))
    return tuple(t.astype(BF16) for t in (g1c, g1t, f2, g3, f4))


def _filter_feats(n, order):
    bands_n = (HY_EMB - 1) // 2
    t = jnp.linspace(0.0, 1.0, n, dtype=F32)[:, None]
    lag = jnp.arange(n, dtype=F32)[:, None]
    bands = jnp.linspace(1e-4, bands_n - 1, bands_n, dtype=F32)[None, :]
    ang = 2.0 * math.pi * lag * bands / n
    zf = jnp.concatenate([t, jnp.cos(ang), -jnp.sin(ang)], axis=-1)
    if order == "dft":
        na, nb = _dft_factors(n)
        m = (np.arange(na)[None, :] * nb + np.arange(nb)[:, None]).reshape(-1)
    else:
        m = (np.arange(2 * n) - n) % (2 * n)
    lag_of = np.where(m < n, m, np.where(m == n, 0, 2 * n - m))
    cf = np.where(m == 0, 0.5, np.where(m < n, 1.0, 0.0)).astype(np.float32)
    cb = np.where(m == 0, 0.5, np.where(m > n, 1.0, 0.0)).astype(np.float32)
    pad = jnp.zeros((2 * n, LANE - HY_EMB - 2), F32)
    return jnp.concatenate([zf[lag_of], jnp.asarray(cf)[:, None], jnp.asarray(cb)[:, None], pad], axis=-1)


def _hy_deltas(w):
    return jnp.abs(jnp.linspace(math.log(HY_DECAY_TARGET) / HY_SLOW_PCT,
                                math.log(HY_DECAY_TARGET) / HY_FAST_PCT, w, dtype=F32))


def _long_conv_latent(z, taps_perm, inv_l1):
    b, n, w = z.shape
    na, nb = _dft_factors(n)
    ah = na // 2
    g1c, g1t, f2, g3, f4 = _dft_tables(n)
    h = _spec(_dft1(g1t, taps_perm.reshape(nb, na, w)).reshape(nb, 2, na, w).transpose(2, 1, 0, 3)
              .reshape(na, 2 * nb, w), f2, inv_l1)
    outs = []
    for pair in range(b // 2):
        zz = z[2 * pair:2 * pair + 2].astype(BF16).reshape(2, ah, nb, w).transpose(2, 0, 1, 3).reshape(nb, na, w)
        a = _dft1(g1c, zz).reshape(nb, 2, na, w).transpose(2, 1, 0, 3).reshape(na, 2 * nb, w)
        bm = _conv_mid(a, f2, h, g3).reshape(na, 2, nb, w).transpose(2, 1, 0, 3).reshape(nb, 2 * na, w)
        y = _conv_out(bm, f4, 1.0 / (na * nb))
        outs.append(y.reshape(nb, 2, ah, w).transpose(1, 2, 0, 3).reshape(2, n, w))
    return jnp.concatenate(outs, axis=0) if len(outs) > 1 else outs[0]


def _pack_in_proj(w_in, rw_mu, dims):
    depth, d_model, _ = w_in.shape
    w, dl, al, gl = dims
    ld, la, lg = _rup(dl, LANE), _rup(al, LANE), _rup(gl, LANE)
    rw_cols = 3 * w + 2 * dl + 2 * al + gl
    tile = 1024 if d_model >= 2048 else LANE
    na = _rup(3 * w + 2 * ld + 2 * la + lg, tile)
    pieces = [(0, 3 * w, 0), (3 * w, dl, 3 * w), (3 * w + dl, dl, 3 * w + ld),
              (3 * w + 2 * dl, al, 3 * w + 2 * ld), (3 * w + 2 * dl + al, al, 3 * w + 2 * ld + la),
              (3 * w + 2 * dl + 2 * al, gl, 3 * w + 2 * ld + 2 * la)]
    cols, mus = [], []
    orig = np.full((na,), rw_cols - 1, np.int64)
    pos = 0
    for src, width, dst in pieces + [(rw_cols, 0, na)]:
        if dst > pos:
            cols.append(jnp.zeros((depth, d_model, dst - pos), BF16))
            mus.append(jnp.zeros((depth, dst - pos), F32))
        cols.append(w_in[:, :, src:src + width].astype(BF16))
        mus.append(rw_mu[:, src:src + width])
        orig[dst:dst + width] = np.arange(src, src + width)
        pos = dst + width
    wa = _tile_w(jnp.concatenate(cols, axis=2), tile)
    shift = _rup(rw_cols, tile) - rw_cols
    total = shift + w_in.shape[2]
    wrest = _tile_w(jnp.pad(w_in, ((0, 0), (0, 0), (shift, _rup(total, tile) - total))), tile)
    starts = (rw_cols, rw_cols + 5 * w, rw_cols + 8 * w)
    offs = [(shift + s) // tile for s in starts]
    mu = jnp.concatenate(mus, axis=1).reshape(depth, 1, na)
    grp = np.stack([orig // (rw_cols // 4), orig // (rw_cols // 2)]).astype(np.int32)
    grp = np.minimum(grp, np.array([[3], [1]])).astype(np.int32)
    return wa, wrest, tile, offs, mu, jnp.asarray(grp), (w, ld, la, lg), na


def _pad_rows(a, rows):
    return jnp.pad(a, [(0, 0)] * (a.ndim - 2) + [(0, rows - a.shape[-2]), (0, 0)])


def kernel(x, c, ctx, c_ctx, ada_w, ada_b, norm1_g, norm2_g, w_in, rw_mu, rw_w0, rw_w_up, rw_a0, rw_a_up, rw_g_up, rw_k_k, rw_k_a, rw_r_k, rw_ln_g, rw_ln_b, hg_lower_bounds, hg_norm_g, hy_conv_w, hy_conv_b, hy_f_w1, hy_f_b1, hy_f_w2, hy_f_b2, hy_f_w3, hy_freq, hy_bias, w_branch_a, w_branch_b, w_branch_c, w_out, ffn_w_gate, ffn_w_up, ffn_w_down, final_norm_g):
    bsz, n_lat, d = x.shape
    n_ctx = ctx.shape[1]
    depth = w_in.shape[0]
    w = rw_k_k.shape[1]
    dims = (w, rw_w_up.shape[2], rw_a_up.shape[2], rw_g_up.shape[1])
    assert bsz % 2 == 0 and n_lat % GRID_W == 0 and n_ctx % CHUNK == 0 and w % LANE == 0
    rw_cols = 3 * w + 2 * dims[1] + 2 * dims[2] + dims[3]
    hg_end = rw_cols + 5 * w
    hy_end = hg_end + 3 * w

    xs = jnp.concatenate([x, ctx], axis=1)
    cvec = jnp.zeros((_rup(bsz + 1, 8), d), F32).at[:bsz].set(c).at[bsz].set(c_ctx)
    lb_cum = jnp.cumsum(jax.nn.softmax(hg_lower_bounds.astype(F32), axis=0), axis=0)
    deltas = _hy_deltas(w)

    wt_rw, wt_in, tile, offs, mu_all, grp, pdims, na = _pack_in_proj(w_in, rw_mu, dims)
    assert (5 * w) % tile == 0 and (3 * w) % tile == 0 and (3 * d) % tile == 0
    _, ld, la, lg = pdims
    hid = ffn_w_gate.shape[2]
    hp = _rup(hid, 1024) if hid > 1024 else _rup(hid, LANE)
    wt_g = _tile_w(_cast_pad(ffn_w_gate, d, hp), _tn_ffn(hp))
    wt_u = _tile_w(_cast_pad(ffn_w_up, d, hp), _tn_ffn(hp))
    wt_d = _tile_w(_cast_pad(ffn_w_down, hp, d), _tn_wide(d), _tk_of(hp))
    wt_o = _tile_w(w_out, _tn_wide(d), _tk_of(d))
    wt_a, wt_b, wt_c = (_tile_w(t, _tn_wide(d)) for t in (w_branch_a, w_branch_b, w_branch_c))

    for layer in range(depth):
        need_ctx = layer < depth - 1
        mod = _ada(cvec, ada_w, ada_b, layer)[:bsz + 1].reshape(bsz + 1, 1, 6, d)
        mods = [mod[:, :, s, :] for s in range(6)]

        h1 = _modnorm(xs, norm1_g[layer], mods[0], mods[1], n_lat)
        pa = _matmul(h1, wt_rw, layer, 0, na // tile, BF16)
        phg = _matmul(h1, wt_in, layer, offs[0], 5 * w // tile, F32)
        rows = None if need_ctx else n_lat
        phy = _matmul(h1, wt_in, layer, offs[1], 3 * w // tile, BF16, rows)
        pgate = _matmul(h1, wt_in, layer, offs[2], 3 * d // tile, BF16, rows)

        r, v, kn, g, bonus, lw, kd, bb = _rw_prep(
            pa, grp, mu_all[layer], rw_w0[layer].reshape(2, 1, w), _pad_rows(rw_w_up[layer], ld),
            rw_a0[layer].reshape(2, 1, w), _pad_rows(rw_a_up[layer], la), _pad_rows(rw_g_up[layer], lg),
            rw_k_k[layer].reshape(1, w), rw_k_a[layer].reshape(1, w), rw_r_k[layer].reshape(1, w),
            n_lat, pdims)
        lb = lb_cum[layer] - lb_cum[0]
        lbt = jnp.stack([jnp.log(lb), jnp.log1p(-lb), 1.0 - lb], axis=1)
        y_f, y_b, o_f, o_b = _scans(r, v, kn, lw, kd, bb, phg, lbt, n_lat)
        oa = _rw_finish(y_f, y_b, bonus, g, rw_ln_g[layer], rw_ln_b[layer])
        ob = _hg_finish(o_f, o_b, phg, hg_norm_g[layer], w)

        x0, z = _hy_pre(phy, hy_conv_w[layer], hy_conv_b[layer], n_lat)
        w1p = _pad_rows(hy_f_w1[layer], LANE)
        filt = (w1p, hy_f_b1[layer], hy_f_w2[layer], hy_f_b2[layer], hy_f_w3[layer], hy_freq[layer], deltas)
        taps, l1 = _hy_taps(_filter_feats(n_lat, "dft"), *filt)
        y_lat = _long_conv_latent(z[:, :n_lat], taps, 1.0 / l1)
        if need_ctx:
            ext, l1c = _hy_taps(_filter_feats(n_ctx, "lag"), *filt)
            y_ctx = _ctx_conv(z[:, n_lat:].astype(F32), ext, 1.0 / l1c)
            y_lat = jnp.concatenate([y_lat, y_ctx], axis=1)
        oc = _hy_post(y_lat, z, x0, hy_bias[layer])

        ym = _merge(oa, ob, oc, wt_a, wt_b, wt_c, layer, pgate)
        xs = _matmul_resid(ym, wt_o, layer, xs, mods[2], n_lat)

        h2 = _modnorm(xs, norm2_g[layer], mods[3], mods[4], n_lat)
        act = _swiglu_up(h2, wt_g, wt_u, layer)
        xs = _matmul_resid(act, wt_d, layer, xs, mods[5], n_lat)

    return _final_norm(xs, final_norm_g, n_lat)
```

```python
import functools
import math

import numpy as np
import jax
import jax.numpy as jnp
from jax import lax
from jax.experimental import pallas as pl
from jax.experimental.pallas import tpu as pltpu

F32 = jnp.float32
BF16 = jnp.bfloat16

GRID_W = 64
CHUNK = 64
SUB = 8
NORM_EPS = 1e-6
RW_HEAD = 64
RW_GN_EPS = 64e-5
HY_EMB = 33
HY_DECAY_TARGET = 1e-2
HY_FAST_PCT = 0.3
HY_SLOW_PCT = 1.5
LANE = 128
VMEM_CAP = 56 * 1024 * 1024
VMEM_SLACK = 8 * 1024 * 1024


def _params(sem, vmem_bytes):
    return pltpu.CompilerParams(dimension_semantics=sem,
                                vmem_limit_bytes=int(min(max(vmem_bytes + VMEM_SLACK, 16 << 20), VMEM_CAP)))


def _pick(n, cands):
    for c in cands:
        if n % c == 0:
            return c
    return n


def _rup(n, m):
    return -(-n // m) * m


def _bdot(a, b):
    return jnp.dot(a.astype(BF16), b.astype(BF16), preferred_element_type=F32)


def _split(x):
    hi = x.astype(BF16)
    lo = (x - hi.astype(F32)).astype(BF16)
    return hi, lo


def _mask_dot(m, x):
    hi, lo = _split(x)
    return (jnp.dot(m, hi, preferred_element_type=F32) + jnp.dot(m, lo, preferred_element_type=F32))


def _x_mask_dot(x, m):
    hi, lo = _split(x)
    return (jnp.dot(hi, m, preferred_element_type=F32) + jnp.dot(lo, m, preferred_element_type=F32))


def _sigmoid(x):
    return 1.0 / (1.0 + jnp.exp(-x))


def _order_masks(d, n, reps=1):
    row = lax.broadcasted_iota(jnp.int32, (n, reps * n), 0)
    col = lax.broadcasted_iota(jnp.int32, (n, reps * n), 1) % n
    diff = (row - col) * (1 - 2 * d)
    return diff >= 0, diff > 0


def _ada_kernel(c_ref, w_ref, b_ref, o_ref):
    c = c_ref[...]
    o_ref[...] = _bdot(c * _sigmoid(c), w_ref[...]) + b_ref[...]


def _ada(cvec, w, b, layer):
    rows, d = cvec.shape
    n = w.shape[2]
    tn = _pick(n, (1024, 512, 256, 128))
    return pl.pallas_call(
        _ada_kernel, name="ada",
        grid=(n // tn,),
        in_specs=[pl.BlockSpec((rows, d), lambda j: (0, 0)),
                  pl.BlockSpec((None, d, tn), lambda j: (layer, 0, j)),
                  pl.BlockSpec((None, 1, tn), lambda j: (layer, 0, j))],
        out_specs=pl.BlockSpec((rows, tn), lambda j: (0, j)),
        out_shape=jax.ShapeDtypeStruct((rows, n), F32),
        compiler_params=_params(("arbitrary",), 3 * d * tn * 4),
    )(cvec, w, b.reshape(b.shape[0], 1, n))


def _modnorm_kernel(x_ref, g_ref, shl_ref, scl_ref, shc_ref, scc_ref, o_ref, *, n_lat):
    tm = x_ref.shape[1]
    x = x_ref[0]
    y = x * lax.rsqrt(jnp.mean(x * x, axis=-1, keepdims=True) + NORM_EPS) * g_ref[...]
    pos = pl.program_id(1) * tm + lax.broadcasted_iota(jnp.int32, (tm, 1), 0)
    is_ctx = pos >= n_lat
    sc = jnp.where(is_ctx, scc_ref[0], scl_ref[0])
    sh = jnp.where(is_ctx, shc_ref[0], shl_ref[0])
    o_ref[0] = (y * (1.0 + sc) + sh).astype(o_ref.dtype)


def _modnorm(x, gain, shift, scale, n_lat):
    b, t, d = x.shape
    tm = _pick(t, (512, 384, 256, 128, 64))
    vec = lambda f: pl.BlockSpec((1, 1, d), f)
    return pl.pallas_call(
        functools.partial(_modnorm_kernel, n_lat=n_lat), name="modnorm",
        grid=(b, t // tm),
        in_specs=[pl.BlockSpec((1, tm, d), lambda bi, i: (bi, i, 0)),
                  pl.BlockSpec((1, d), lambda bi, i: (0, 0)),
                  vec(lambda bi, i: (bi, 0, 0)), vec(lambda bi, i: (bi, 0, 0)),
                  vec(lambda bi, i: (b, 0, 0)), vec(lambda bi, i: (b, 0, 0))],
        out_specs=pl.BlockSpec((1, tm, d), lambda bi, i: (bi, i, 0)),
        out_shape=jax.ShapeDtypeStruct((b, t, d), BF16),
        compiler_params=_params(("parallel", "arbitrary"), 6 * tm * d * 4),
    )(x, gain.reshape(1, d), shift, scale, shift, scale)


def _rmsnorm_kernel(x_ref, g_ref, o_ref):
    x = x_ref[0]
    o_ref[0] = x * lax.rsqrt(jnp.mean(x * x, axis=-1, keepdims=True) + NORM_EPS) * g_ref[...]


def _final_norm(x, gain, n_lat):
    b, _, d = x.shape
    tm = _pick(n_lat, (512, 384, 256, 128, 64))
    return pl.pallas_call(
        _rmsnorm_kernel, name="final_norm",
        grid=(b, n_lat // tm),
        in_specs=[pl.BlockSpec((1, tm, d), lambda bi, i: (bi, i, 0)),
                  pl.BlockSpec((1, d), lambda bi, i: (0, 0))],
        out_specs=pl.BlockSpec((1, tm, d), lambda bi, i: (bi, i, 0)),
        out_shape=jax.ShapeDtypeStruct((b, n_lat, d), F32),
        compiler_params=_params(("parallel", "arbitrary"), 6 * tm * d * 4),
    )(x, gain.reshape(1, d))


def _mm_kernel(x_ref, w_ref, o_ref):
    o_ref[0] = jnp.dot(x_ref[0], w_ref[...], preferred_element_type=F32).astype(o_ref.dtype)


def _tn_wide(n):
    return _pick(n, (1024, 512, 256, 128))


def _tn_ffn(n):
    return _pick(n, (512, 256, 128))


def _tk_of(k):
    return k if k <= 2048 else _pick(k, (2816, 2048, 1024, 512, 256, 128))


class _TiledW:
    def __init__(self, w, tn, tk=None):
        self.w = w.astype(BF16)
        self.tk = w.shape[1] if tk is None else tk
        self.tn = tn
        self.n = w.shape[2]


def _tile_w(w, tn, tk=None):
    return _TiledW(w, tn, tk)


def _cast_pad_kernel(x_ref, o_ref, *, n_in, cols):
    i = pl.program_id(1)

    @pl.when(i < n_in)
    def _():
        o_ref[0, :, :cols] = x_ref[0].astype(o_ref.dtype)
        if cols < o_ref.shape[2]:
            o_ref[0, :, cols:] = jnp.zeros((o_ref.shape[1], o_ref.shape[2] - cols), o_ref.dtype)

    @pl.when(i >= n_in)
    def _():
        o_ref[0] = jnp.zeros(o_ref.shape[1:], o_ref.dtype)


def _cast_pad(w, rows_out, cols_out):
    l, r, c = w.shape
    tr = _pick(math.gcd(r, rows_out), (256, 128, 64, 32, 16))
    n_in = r // tr
    return pl.pallas_call(
        functools.partial(_cast_pad_kernel, n_in=n_in, cols=c), name="cast_pad",
        grid=(l, rows_out // tr),
        in_specs=[pl.BlockSpec((1, tr, c), lambda li, i: (li, jnp.minimum(i, n_in - 1), 0))],
        out_specs=pl.BlockSpec((1, tr, cols_out), lambda li, i: (li, i, 0)),
        out_shape=jax.ShapeDtypeStruct((l, rows_out, cols_out), BF16),
        compiler_params=_params(("parallel", "arbitrary"), 2 * tr * (c * 4 + cols_out * 2)),
    )(w)


def _cast_shift_kernel(x_ref, o_ref, *, shift):
    c = x_ref.shape[2]
    rows, cols = o_ref.shape[1:]
    if shift:
        o_ref[0, :, :shift] = jnp.zeros((rows, shift), o_ref.dtype)
    o_ref[0, :, shift:shift + c] = x_ref[0].astype(o_ref.dtype)
    if shift + c < cols:
        o_ref[0, :, shift + c:] = jnp.zeros((rows, cols - shift - c), o_ref.dtype)


def _cast_shift(w, shift, cols_out):
    l, r, c = w.shape
    tr = _pick(r, (128, 64, 32, 16))
    return pl.pallas_call(
        functools.partial(_cast_shift_kernel, shift=shift), name="cast_shift",
        grid=(l, r // tr),
        in_specs=[pl.BlockSpec((1, tr, c), lambda li, i: (li, i, 0))],
        out_specs=pl.BlockSpec((1, tr, cols_out), lambda li, i: (li, i, 0)),
        out_shape=jax.ShapeDtypeStruct((l, r, cols_out), BF16),
        compiler_params=_params(("parallel", "arbitrary"), 2 * tr * (c * 4 + cols_out * 2) + tr * c * 4),
    )(w)


def _wspec(tk, tn, f):
    def idx(*a):
        lay, j, l = f(*a)[:3]
        return lay, l, j
    return pl.BlockSpec((None, tk, tn), idx)


_TM_ROWS = (1024, 768, 512, 384, 256, 128, 64)


def _matmul(x, wt, layer, j0, nj, out_dtype, rows=None):
    b, _, k = x.shape
    t = x.shape[1] if rows is None else rows
    tn = wt.tn
    n = nj * tn
    tm = _pick(t, _TM_ROWS)
    osz = jnp.dtype(out_dtype).itemsize
    return pl.pallas_call(
        _mm_kernel, name="in_proj",
        grid=(b, t // tm, n // tn),
        in_specs=[pl.BlockSpec((1, tm, k), lambda bi, i, j: (bi, i, 0)),
                  _wspec(k, tn, lambda bi, i, j: (layer, j0 + j, 0, 0, 0))],
        out_specs=pl.BlockSpec((1, tm, tn), lambda bi, i, j: (bi, i, j)),
        out_shape=jax.ShapeDtypeStruct((b, t, n), out_dtype),
        compiler_params=_params(("parallel", "parallel", "arbitrary"),
                                2 * (tm * k * 2 + k * tn * 2 + tm * tn * osz) + tm * tn * 4),
    )(x, wt.w)


def _swiglu_kernel(x_ref, wg_ref, wu_ref, o_ref):
    x = x_ref[0]
    g = jnp.dot(x, wg_ref[...], preferred_element_type=F32)
    u = jnp.dot(x, wu_ref[...], preferred_element_type=F32)
    o_ref[0] = (g * _sigmoid(g) * u).astype(o_ref.dtype)


def _swiglu_up(x, wg, wu, layer):
    b, t, k = x.shape
    tn = wg.tn
    n = wg.n
    tm = _pick(t, _TM_ROWS)
    wsp = lambda: _wspec(k, tn, lambda bi, i, j: (layer, j, 0, 0, 0))
    return pl.pallas_call(
        _swiglu_kernel, name="swiglu_up",
        grid=(b, t // tm, n // tn),
        in_specs=[pl.BlockSpec((1, tm, k), lambda bi, i, j: (bi, i, 0)), wsp(), wsp()],
        out_specs=pl.BlockSpec((1, tm, tn), lambda bi, i, j: (bi, i, j)),
        out_shape=jax.ShapeDtypeStruct((b, t, n), BF16),
        compiler_params=_params(("parallel", "parallel", "arbitrary"),
                                2 * (tm * k * 2 + 2 * k * tn * 2 + tm * tn * 2) + 3 * tm * tn * 4),
    )(x, wg.w, wu.w)


def _resid_kernel(x_ref, w_ref, r_ref, gl_ref, gc_ref, o_ref, acc_ref, *, n_lat):
    kk = pl.program_id(3)

    @pl.when(kk == 0)
    def _():
        acc_ref[...] = jnp.zeros_like(acc_ref)

    acc_ref[...] += jnp.dot(x_ref[0], w_ref[...], preferred_element_type=F32)

    @pl.when(kk == pl.num_programs(3) - 1)
    def _():
        tm = acc_ref.shape[0]
        pos = pl.program_id(1) * tm + lax.broadcasted_iota(jnp.int32, (tm, 1), 0)
        gate = jnp.where(pos >= n_lat, gc_ref[0], gl_ref[0])
        o_ref[0] = r_ref[0] + gate * acc_ref[...]


def _matmul_resid(x, w, layer, res, gate, n_lat):
    b, t, k = x.shape
    tk, tn = w.tk, w.tn
    n = w.n
    tm = _pick(t, _TM_ROWS)
    return pl.pallas_call(
        functools.partial(_resid_kernel, n_lat=n_lat), name="proj_resid",
        grid=(b, t // tm, n // tn, k // tk),
        in_specs=[pl.BlockSpec((1, tm, tk), lambda bi, i, j, l: (bi, i, l)),
                  _wspec(tk, tn, lambda bi, i, j, l: (layer, j, l, 0, 0)),
                  pl.BlockSpec((1, tm, tn), lambda bi, i, j, l: (bi, i, j)),
                  pl.BlockSpec((1, 1, tn), lambda bi, i, j, l: (bi, 0, j)),
                  pl.BlockSpec((1, 1, tn), lambda bi, i, j, l: (b, 0, j))],
        out_specs=pl.BlockSpec((1, tm, tn), lambda bi, i, j, l: (bi, i, j)),
        out_shape=jax.ShapeDtypeStruct((b, t, n), F32),
        scratch_shapes=[pltpu.VMEM((tm, tn), F32)],
        compiler_params=_params(("parallel", "parallel", "arbitrary", "arbitrary"),
                                2 * (tm * tk * 2 + tk * tn * 2 + 2 * tm * tn * 4) + 2 * tm * tn * 4),
    )(x, w.w, res, gate, gate)


def _merge_kernel(oa_ref, ob_ref, oc_ref, wa_ref, wb_ref, wc_ref, ga_ref, gb_ref, gc_ref, o_ref):
    y = _sigmoid(ga_ref[0].astype(F32)) * jnp.dot(oa_ref[0], wa_ref[...], preferred_element_type=F32)
    y += _sigmoid(gb_ref[0].astype(F32)) * jnp.dot(ob_ref[0], wb_ref[...], preferred_element_type=F32)
    y += _sigmoid(gc_ref[0].astype(F32)) * jnp.dot(oc_ref[0], wc_ref[...], preferred_element_type=F32)
    o_ref[0] = y.astype(o_ref.dtype)


def _merge(oa, ob, oc, wa, wb, wc, layer, pgate):
    b, _, kw = oa.shape
    t = pgate.shape[1]
    tn = wa.tn
    nj = wa.n // tn
    d = nj * tn
    tm = _pick(t, (768, 512, 384, 256, 128, 64))
    br = lambda: pl.BlockSpec((1, tm, kw), lambda bi, i, j: (bi, i, 0))
    wt = lambda: _wspec(kw, tn, lambda bi, i, j: (layer, j, 0, 0, 0))
    gt = lambda s: pl.BlockSpec((1, tm, tn), lambda bi, i, j: (bi, i, s * nj + j))
    gsz = jnp.dtype(pgate.dtype).itemsize
    return pl.pallas_call(
        _merge_kernel, name="merge",
        grid=(b, t // tm, nj),
        in_specs=[br(), br(), br(), wt(), wt(), wt(), gt(0), gt(1), gt(2)],
        out_specs=pl.BlockSpec((1, tm, tn), lambda bi, i, j: (bi, i, j)),
        out_shape=jax.ShapeDtypeStruct((b, t, d), BF16),
        compiler_params=_params(("parallel", "parallel", "arbitrary"),
                                2 * (3 * tm * kw * 2 + 3 * kw * tn * 2 + 3 * tm * tn * gsz + tm * tn * 2)
                                + 4 * tm * tn * 4),
    )(oa, ob, oc, wa.w, wb.w, wc.w, pgate, pgate, pgate)


def _pair_sum_matrix():
    r = lax.broadcasted_iota(jnp.int32, (LANE, LANE), 0) // RW_HEAD
    c = lax.broadcasted_iota(jnp.int32, (LANE, LANE), 1) // RW_HEAD
    return jnp.where(r == c, 1.0, 0.0).astype(BF16)


def _head_sums(x, hs):
    rows, w = x.shape
    nt = w // LANE
    stacked = jnp.concatenate([x[:, j * LANE:(j + 1) * LANE] for j in range(nt)], axis=0)
    s = _x_mask_dot(stacked, hs)
    return jnp.concatenate([s[j * rows:(j + 1) * rows] for j in range(nt)], axis=1)


def _rw_prep_kernel(prev_ref, cur_ref, next_ref, grp_ref, mu_ref, w0_ref, wup_ref, a0_ref, aup_ref,
                    gup_ref, kk_ref, ka_ref, rk_ref,
                    r_ref, v_ref, kn_ref, g_ref, bon_ref, lw_ref, kd_ref, bb_ref,
                    *, n_lat_chunks, n_chunks, w, ld, la, lg):
    i = pl.program_id(1)
    c = CHUNK
    cur = cur_ref[0].astype(F32)
    prev = prev_ref[0].astype(F32)
    nxt = next_ref[0].astype(F32)
    is_ctx = i >= n_lat_chunks
    first = jnp.logical_or(i == 0, i == n_lat_chunks)
    last = jnp.logical_or(i == n_lat_chunks - 1, i == n_chunks - 1)
    row = lax.broadcasted_iota(jnp.int32, (c, 1), 0)
    carry_in = jnp.where(jnp.logical_and(is_ctx, jnp.logical_not(first)), 1.0, 0.0)
    carry_out = jnp.where(jnp.logical_and(is_ctx, jnp.logical_not(last)), 1.0, 0.0)
    tm1 = jnp.where(row == 0, prev[c - 1:c] * carry_in, pltpu.roll(cur, 1, 0))
    tp1 = jnp.where(row == c - 1, nxt[0:1] * carry_out, pltpu.roll(cur, c - 1, 0))
    lat_up = jnp.where(jnp.logical_or(is_ctx, first), 0.0, 1.0)
    lat_dn = jnp.where(jnp.logical_or(is_ctx, last), 0.0, 1.0)
    code = jnp.where(is_ctx, grp_ref[1:2], grp_ref[0:1])
    shifted = jnp.where(code == 0, tm1,
                        jnp.where(code == 1, tp1,
                                  jnp.where(code == 2, prev * lat_up, nxt * lat_dn)))
    m = cur + mu_ref[...] * (shifted - cur)

    r = m[:, 0:w]
    k = m[:, w:2 * w]
    v = m[:, 2 * w:3 * w]
    o = 3 * w
    wd = (m[:, o:o + ld], m[:, o + ld:o + 2 * ld])
    ad = (m[:, o + 2 * ld:o + 2 * ld + la], m[:, o + 2 * ld + la:o + 2 * ld + 2 * la])
    gd = m[:, o + 2 * ld + 2 * la:o + 2 * ld + 2 * la + lg]

    hs = _pair_sum_matrix()
    g_ref[0] = _bdot(_sigmoid(gd), gup_ref[...]).astype(g_ref.dtype)
    kx = k * kk_ref[...]
    kn = kx * lax.rsqrt(jnp.maximum(_head_sums(kx * kx, hs), 1e-24))
    r_ref[0] = r.astype(r_ref.dtype)
    v_ref[0] = v.astype(v_ref.dtype)
    kn_ref[0] = kn.astype(kn_ref.dtype)
    bonus = jnp.zeros_like(r)
    for d in range(2):
        wl = w0_ref[d] + _bdot(jnp.tanh(wd[d]), wup_ref[d])
        lw_ref[d, 0] = -math.exp(-0.5) * _sigmoid(wl)
        a = _sigmoid(a0_ref[d] + _bdot(ad[d], aup_ref[d]))
        kd = k * (1.0 + (a - 1.0) * ka_ref[...])
        kd_ref[d, 0] = kd.astype(kd_ref.dtype)
        bb_ref[d, 0] = (kn * a).astype(bb_ref.dtype)
        bonus += r * kd * rk_ref[...]
    bon_ref[0] = (_head_sums(bonus, hs) * v).astype(bon_ref.dtype)


def _rw_prep(pa, grp, mu, w0, wup, a0, aup, gup, k_k, k_a, r_k, n_lat, dims):
    b, t, na = pa.shape
    w, ld, la, lg = dims
    nc = t // CHUNK
    nlc = n_lat // CHUNK
    blk = lambda f: pl.BlockSpec((1, CHUNK, na), f)
    full = lambda a: pl.BlockSpec(a.shape, lambda bi, i: (0,) * a.ndim)
    o1 = pl.BlockSpec((1, CHUNK, w), lambda bi, i: (bi, i, 0))
    o2 = pl.BlockSpec((2, 1, CHUNK, w), lambda bi, i: (0, bi, i, 0))
    s1 = jax.ShapeDtypeStruct((b, t, w), BF16)
    s2 = jax.ShapeDtypeStruct((2, b, t, w), BF16)
    s2f = jax.ShapeDtypeStruct((2, b, t, w), F32)
    consts = (grp, mu, w0, wup, a0, aup, gup, k_k, k_a, r_k)
    return pl.pallas_call(
        functools.partial(_rw_prep_kernel, n_lat_chunks=nlc, n_chunks=nc, w=w, ld=ld, la=la, lg=lg),
        name="rw_prep",
        grid=(b, nc),
        in_specs=[blk(lambda bi, i: (bi, jnp.maximum(i - 1, 0), 0)),
                  blk(lambda bi, i: (bi, i, 0)),
                  blk(lambda bi, i: (bi, jnp.minimum(i + 1, nc - 1), 0))] + [full(a) for a in consts],
        out_specs=[o1, o1, o1, o1, o1, o2, o2, o2],
        out_shape=[s1, s1, s1, s1, s1, s2f, s2, s2],
        compiler_params=_params(("parallel", "arbitrary"), 32 << 20),
    )(pa, pa, pa, *consts)


def _pdot(a, b):
    return lax.dot_general(a.astype(BF16), b.astype(BF16), (((2,), (1,)), ((0,), (0,))),
                           preferred_element_type=F32)


def _pdot_nt(a, b):
    return lax.dot_general(a.astype(BF16), b.astype(BF16), (((2,), (2,)), ((0,), (0,))),
                           preferred_element_type=F32)


def _pdot_tn(a, b):
    return lax.dot_general(a.astype(BF16), b.astype(BF16), (((1,), (1,)), ((0,), (0,))),
                           preferred_element_type=F32)


def _rw_scan_dir(d, r_ref, v_ref, kn_ref, lw_ref, kd_ref, bb_ref, y_ref, s_ref, n_pairs):
    c = CHUNK
    wfull = n_pairs * LANE
    incl_bf = jnp.where(_order_masks(d, c)[0], 1.0, 0.0).astype(BF16)
    incl2, strict2 = _order_masks(d, c, 2)
    lane_w = lax.broadcasted_iota(jnp.int32, (1, wfull), 1) % LANE
    w0 = jnp.where(lane_w < RW_HEAD, 1.0, 0.0)
    w1 = 1.0 - w0
    lane = lax.broadcasted_iota(jnp.int32, (1, 1, LANE), 2)
    m0 = jnp.where(lane < RW_HEAD, 1.0, 0.0)
    m1 = 1.0 - m0
    rowh = lax.broadcasted_iota(jnp.int32, (LANE, LANE), 0) // RW_HEAD
    colh = lax.broadcasted_iota(jnp.int32, (LANE, LANE), 1) // RW_HEAD
    blockdiag = rowh == colh
    last_row = jnp.where(d == 0, c - 1, 0)
    rsel = lax.broadcasted_iota(jnp.int32, (c, 1), 0) == last_row

    def pairs(x):
        return jnp.stack([x[:, p * LANE:(p + 1) * LANE] for p in range(n_pairs)], axis=0)

    def stack(x):
        return jnp.concatenate([x * m0, x * m1], axis=1)

    lw = lw_ref[0, 0]
    r = r_ref[0].astype(F32)
    v = v_ref[0].astype(F32)
    kd = kd_ref[0, 0].astype(F32)
    bb = bb_ref[0, 0].astype(F32)
    cw = _mask_dot(incl_bf, lw)
    tot = jnp.sum(jnp.where(rsel, cw, 0.0), axis=0, keepdims=True)
    w_inv = jnp.exp(-cw)
    w_rem = jnp.exp(tot - cw)
    a_t = -kn_ref[0].astype(F32) * jnp.exp(cw - lw)
    r_t = r * jnp.exp(cw)
    b_t = bb * w_inv
    k_t = kd * w_inv
    ar = pairs(jnp.concatenate([a_t, r_t], axis=0))
    bk = pairs(jnp.concatenate([b_t * w0, b_t * w1, k_t * w0, k_t * w1], axis=0))
    vst = pairs(jnp.concatenate([v * w0, v * w1], axis=0))
    uvr = pairs(jnp.concatenate([bb * w_rem, kd * w_rem], axis=0))
    v3 = pairs(v)
    decay = pairs(jnp.exp(tot))
    s0 = s_ref[...]
    yield

    sc = _pdot_nt(ar, bk)
    a_ab = jnp.where(strict2, sc[:, :c, :2 * c], 0.0)
    a_ak = jnp.where(strict2, sc[:, :c, 2 * c:], 0.0)
    r_b = jnp.where(incl2, sc[:, c:, :2 * c], 0.0)
    r_k = jnp.where(incl2, sc[:, c:, 2 * c:], 0.0)

    abd = stack(a_ab)
    yield
    q = _pdot(abd, abd)
    nsum = abd
    for _ in range(4):
        yield
        both = _pdot(q, jnp.concatenate([nsum, q], axis=2))
        nsum = nsum + q + both[:, :, :LANE]
        q = both[:, :, LANE:]
    yield
    nsum = nsum + q + _pdot(q, nsum)
    n_side = nsum[:, :c] + nsum[:, c:]

    yield
    art = _pdot_nt(ar, s0)
    rhs = art[:, :c] + _pdot(a_ak, vst)
    yield
    u = rhs + _pdot(n_side, stack(rhs))
    yield
    y = art[:, c:] + _pdot(jnp.concatenate([r_b, r_k], axis=2), jnp.concatenate([stack(u), vst], axis=1))
    for p in range(n_pairs):
        y_ref[0, :, p * LANE:(p + 1) * LANE] = y[p]
    yield
    upd = _pdot_tn(jnp.concatenate([u, v3], axis=1), uvr)
    s_ref[...] = s0 * decay + jnp.where(blockdiag, upd, 0.0)


def _scan_chunk(d, i, n_lat_chunks, n_chunks):
    nctx = n_chunks - n_lat_chunks
    in_ctx = i < nctx
    fwd = jnp.where(in_ctx, n_lat_chunks + i, i - nctx)
    bwd = jnp.where(in_ctx, n_chunks - 1 - i, n_chunks - 1 - i)
    return jnp.where(d == 0, fwd, bwd)


def _rw_finish_kernel(yf_ref, yb_ref, bon_ref, g_ref, lg_ref, lb_ref, o_ref):
    y = yf_ref[0] + yb_ref[0]
    hs = _pair_sum_matrix()
    mean = _head_sums(y, hs) * (1.0 / RW_HEAD)
    yc = y - mean
    var = _head_sums(yc * yc, hs) * (1.0 / RW_HEAD)
    yn = yc * lax.rsqrt(var + RW_GN_EPS)
    o_ref[0] = ((yn * lg_ref[...] + lb_ref[...] + bon_ref[0].astype(F32)) * g_ref[0].astype(F32)).astype(o_ref.dtype)


def _rw_finish(y_f, y_b, bonus, g, ln_g, ln_b):
    b, t, w = y_f.shape
    tm = _pick(t, _TM_ROWS)
    s1 = pl.BlockSpec((1, tm, w), lambda bi, i: (bi, i, 0))
    vec = pl.BlockSpec((1, w), lambda bi, i: (0, 0))
    return pl.pallas_call(
        _rw_finish_kernel, name="rw_finish",
        grid=(b, t // tm),
        in_specs=[s1, s1, s1, s1, vec, vec],
        out_specs=s1,
        out_shape=jax.ShapeDtypeStruct((b, t, w), BF16),
        compiler_params=_params(("parallel", "arbitrary"), 32 << 20),
    )(y_f, y_b, bonus, g, ln_g.reshape(1, w), ln_b.reshape(1, w))


def _hg_scan_dir(d, q_ref, z_ref, i_ref, lb_ref, o_ref, s_ref, n_heads):
    c = CHUNK
    nb = c // SUB
    sgn = 1 - 2 * d
    fwd = d == 0
    incl_bf = jnp.where(_order_masks(d, c)[0], 1.0, 0.0).astype(BF16)
    row = lax.broadcasted_iota(jnp.int32, (c, c), 0)
    col = lax.broadcasted_iota(jnp.int32, (c, c), 1)
    rb, cbk = row // SUB, col // SUB
    lvl_b = (rb - cbk) * sgn > 0
    lvl_1 = jnp.logical_and(rb == cbk, (row - col) * sgn >= 0)
    last_row = jnp.where(d == 0, c - 1, 0)
    rsel = lax.broadcasted_iota(jnp.int32, (c, 1), 0) == last_row

    def heads(x):
        return jnp.stack([x[:, h * LANE:(h + 1) * LANE] for h in range(n_heads)], axis=0)

    q = q_ref[0]
    z = z_ref[0]
    val = i_ref[0]
    lg_l = lb_ref[d, 0:1]
    lg_1ml = lb_ref[d, 1:2]
    one_ml = lb_ref[d, 2:3]
    s0 = s_ref[...]

    ez = jnp.exp(-jnp.abs(z))
    log_sig = jnp.minimum(z, 0.0) - jnp.log(1.0 + ez)
    x2 = lg_1ml + log_sig
    mx = jnp.maximum(lg_l, x2)
    log_f = mx + jnp.log(jnp.exp(lg_l - mx) + jnp.exp(x2 - mx))
    k = one_ml * jnp.where(z >= 0.0, ez, 1.0) / (1.0 + ez)
    yield

    cb = _mask_dot(incl_bf, log_f)
    tot = jnp.sum(jnp.where(rsel, cb, 0.0), axis=0, keepdims=True)
    wfull = cb.shape[1]
    cb3 = cb.reshape(nb, SUB, wfull)
    blk_end = cb3[:, SUB - 1:SUB] if fwd else cb3[:, 0:1]

    end_own = jnp.broadcast_to(blk_end, (nb, SUB, wfull)).reshape(c, wfull)
    refs = [(blk_end[j], slice(0, c)) for j in range(nb)]
    refs += [(jnp.broadcast_to(cb3[:, j:j + 1], (nb, SUB, wfull)).reshape(c, wfull), slice(c, 2 * c))
             for j in range(SUB)]
    which = jnp.where(lvl_b, cbk, jnp.where(lvl_1, nb + col % SUB, -1))
    ks = heads(jnp.concatenate([k * jnp.exp(end_own - cb), k], axis=0))
    q_bf = q.astype(BF16)
    att = jnp.zeros((n_heads, c, c), F32)
    group = 4
    for g0 in range(0, len(refs), group):
        yield
        part = refs[g0:g0 + group]
        qs = heads(jnp.concatenate([q_bf * jnp.exp(jnp.minimum(cb - e, 0.0)).astype(BF16) for e, _ in part],
                                   axis=0))
        sc = _pdot_nt(qs, ks)
        for n, (_, cols) in enumerate(part):
            att = jnp.where(which == g0 + n, sc[:, n * c:(n + 1) * c, cols], att)
    yield
    v3 = heads(val)
    o = _pdot(att, v3) + _pdot_nt(heads(q * jnp.exp(cb)), s0)
    for h in range(n_heads):
        o_ref[0, :, h * LANE:(h + 1) * LANE] = o[h]
    yield
    s_ref[...] = s0 * heads(jnp.exp(tot)) + _pdot_tn(v3, heads(k * jnp.exp(tot - cb)))


def _scans_kernel(*refs, n_pairs, n_heads):
    rw_in, hg_in = refs[0:12], refs[12:19]
    y_f, y_b, o_f, o_b, s_rw, s_hg = refs[19:25]

    @pl.when(pl.program_id(1) == 0)
    def _():
        s_rw[...] = jnp.zeros_like(s_rw)
        s_hg[...] = jnp.zeros_like(s_hg)

    live = [_rw_scan_dir(0, *rw_in[0:6], y_f, s_rw.at[0], n_pairs),
            _hg_scan_dir(0, *hg_in[0:3], hg_in[6], o_f, s_hg.at[0], n_heads),
            _rw_scan_dir(1, *rw_in[6:12], y_b, s_rw.at[1], n_pairs),
            _hg_scan_dir(1, *hg_in[3:6], hg_in[6], o_b, s_hg.at[1], n_heads)]
    while live:
        for gen in list(live):
            if next(gen, StopIteration) is StopIteration:
                live.remove(gen)


def _scans(r, v, kn, lw, kd, bb, phg, lbt, n_lat):
    b, t, w = r.shape
    nc = t // CHUNK
    nlc = n_lat // CHUNK
    ch = functools.partial(_scan_chunk, n_lat_chunks=nlc, n_chunks=nc)
    col = lambda d, cb_: pl.BlockSpec((1, CHUNK, w), lambda bi, i: (bi, ch(d, i), cb_))
    s2 = lambda d: pl.BlockSpec((1, 1, CHUNK, w), lambda bi, i: (d, bi, ch(d, i), 0))
    rw_dir = lambda d: [col(d, 0), col(d, 0), col(d, 0), s2(d), s2(d), s2(d)]
    out = jax.ShapeDtypeStruct((b, t, w), F32)
    return pl.pallas_call(
        functools.partial(_scans_kernel, n_pairs=w // LANE, n_heads=w // LANE), name="scans",
        grid=(b, nc),
        in_specs=rw_dir(0) + rw_dir(1)
        + [col(0, 0), col(0, 1), col(0, 3), col(1, 0), col(1, 2), col(1, 3),
           pl.BlockSpec((2, 3, w), lambda bi, i: (0, 0, 0))],
        out_specs=[col(0, 0), col(1, 0), col(0, 0), col(1, 0)],
        out_shape=[out, out, out, out],
        scratch_shapes=[pltpu.VMEM((2, w // LANE, LANE, LANE), F32), pltpu.VMEM((2, w // LANE, LANE, LANE), F32)],
        compiler_params=_params(("arbitrary", "arbitrary"), 40 << 20),
    )(r, v, kn, lw, kd, bb, r, v, kn, lw, kd, bb, phg, phg, phg, phg, phg, phg, lbt)


def _hg_finish_kernel(of_ref, ob_ref, g_ref, ng_ref, out_ref, *, n_heads):
    for h in range(n_heads):
        sl = slice(h * LANE, (h + 1) * LANE)
        o = of_ref[0, :, sl] + ob_ref[0, :, sl]
        y = o * lax.rsqrt(jnp.mean(o * o, axis=-1, keepdims=True) + NORM_EPS) * ng_ref[...]
        g = g_ref[0, :, sl]
        out_ref[0, :, sl] = (y * g * _sigmoid(g)).astype(out_ref.dtype)


def _hg_finish(o_f, o_b, phg, norm_g, w):
    b, t, _ = o_f.shape
    tm = _pick(t, _TM_ROWS)
    return pl.pallas_call(
        functools.partial(_hg_finish_kernel, n_heads=w // LANE), name="hg_finish",
        grid=(b, t // tm),
        in_specs=[pl.BlockSpec((1, tm, w), lambda bi, i: (bi, i, 0)),
                  pl.BlockSpec((1, tm, w), lambda bi, i: (bi, i, 0)),
                  pl.BlockSpec((1, tm, w), lambda bi, i: (bi, i, 4)),
                  pl.BlockSpec((1, LANE), lambda bi, i: (0, 0))],
        out_specs=pl.BlockSpec((1, tm, w), lambda bi, i: (bi, i, 0)),
        out_shape=jax.ShapeDtypeStruct((b, t, w), BF16),
        compiler_params=_params(("parallel", "arbitrary"), 32 << 20),
    )(o_f, o_b, phg, norm_g.reshape(1, LANE))


def _hy_pre_kernel(prev_ref, cur_ref, next_ref, cw_ref, cb_ref, x0_ref, z_ref, *, n_lat_blocks, n_blocks, w):
    i = pl.program_id(1)
    tm = cur_ref.shape[1]
    hb = prev_ref.shape[1]
    cur = cur_ref[0].astype(F32)
    first = jnp.logical_or(i == 0, i == n_lat_blocks)
    last = jnp.logical_or(i == n_lat_blocks - 1, i == n_blocks - 1)
    row = lax.broadcasted_iota(jnp.int32, (tm, 1), 0)
    p_last = prev_ref[0, hb - 1:hb].astype(F32) * jnp.where(first, 0.0, 1.0)
    n_first = next_ref[0, 0:1].astype(F32) * jnp.where(last, 0.0, 1.0)
    before = jnp.where(row == 0, p_last, pltpu.roll(cur, 1, 0))
    after = jnp.where(row == tm - 1, n_first, pltpu.roll(cur, tm - 1, 0))
    u = cw_ref[0:1] * before + cw_ref[1:2] * cur + cw_ref[2:3] * after + cb_ref[...]
    x0_ref[0] = u[:, 0:w].astype(x0_ref.dtype)
    z_ref[0] = (u[:, w:2 * w] * u[:, 2 * w:3 * w]).astype(z_ref.dtype)


def _hy_pre(phy, conv_w, conv_b, n_lat):
    b, t, w3 = phy.shape
    w = w3 // 3
    tm = _pick(math.gcd(n_lat, t - n_lat), (256, 128, 64))
    nb = t // tm
    hb = 16
    per = tm // hb
    o = pl.BlockSpec((1, tm, w), lambda bi, i: (bi, i, 0))
    s = jax.ShapeDtypeStruct((b, t, w), BF16)
    return pl.pallas_call(
        functools.partial(_hy_pre_kernel, n_lat_blocks=n_lat // tm, n_blocks=nb, w=w), name="hy_pre",
        grid=(b, nb),
        in_specs=[pl.BlockSpec((1, hb, w3), lambda bi, i: (bi, jnp.maximum(i * per - 1, 0), 0)),
                  pl.BlockSpec((1, tm, w3), lambda bi, i: (bi, i, 0)),
                  pl.BlockSpec((1, hb, w3), lambda bi, i: (bi, jnp.minimum((i + 1) * per, t // hb - 1), 0)),
                  pl.BlockSpec((3, w3), lambda bi, i: (0, 0)),
                  pl.BlockSpec((1, w3), lambda bi, i: (0, 0))],
        out_specs=[o, o],
        out_shape=[s, s],
        compiler_params=_params(("parallel", "arbitrary"), 40 << 20),
    )(phy, phy, phy, conv_w, conv_b.reshape(1, w3))


def _hdot(a, b):
    return jnp.dot(a, b, precision=lax.Precision.HIGHEST, preferred_element_type=F32)


def _hy_taps_kernel(ft_ref, w1_ref, b1_ref, w2_ref, b2_ref, w3f_ref, w3b_ref, fr_ref, dl_ref,
                    tap_ref, sum_ref):
    i = pl.program_id(1)
    ft = ft_ref[...]
    fr = fr_ref[...]
    h = jnp.sin(fr * (_hdot(ft, w1_ref[...]) + b1_ref[...]))
    h = jnp.sin(fr * (_hdot(h, w2_ref[...]) + b2_ref[...]))
    cf = ft[:, HY_EMB:HY_EMB + 1]
    cb = ft[:, HY_EMB + 1:HY_EMB + 2]
    tt = ft[:, 0:1]
    tap = (cf * _hdot(h, w3f_ref[...]) + cb * _hdot(h, w3b_ref[...])) * jnp.exp(-tt * dl_ref[...])
    tap_ref[...] = tap

    @pl.when(i == 0)
    def _():
        sum_ref[...] = jnp.zeros_like(sum_ref)

    sum_ref[...] += jnp.sum(jnp.abs(tap), axis=0, keepdims=True)


def _hy_taps(feats, w1p, b1, w2, b2, w3, freq, deltas):
    rws = feats.shape[0]
    hid = w2.shape[0]
    w = w3.shape[1] // 2
    tr = _pick(rws, (1024, 512, 256, 128))
    ct = w
    nj = w // ct
    c2 = lambda a: pl.BlockSpec(a.shape, lambda j, i: (0, 0))
    return pl.pallas_call(
        _hy_taps_kernel, name="hy_taps",
        grid=(nj, rws // tr),
        in_specs=[pl.BlockSpec((tr, LANE), lambda j, i: (i, 0)),
                  c2(w1p), pl.BlockSpec((1, hid), lambda j, i: (0, 0)),
                  c2(w2), pl.BlockSpec((1, hid), lambda j, i: (0, 0)),
                  pl.BlockSpec((hid, ct), lambda j, i: (0, j)),
                  pl.BlockSpec((hid, ct), lambda j, i: (0, nj + j)),
                  pl.BlockSpec((1, hid), lambda j, i: (0, 0)),
                  pl.BlockSpec((1, ct), lambda j, i: (0, j))],
        out_specs=[pl.BlockSpec((tr, ct), lambda j, i: (i, j)),
                   pl.BlockSpec((1, ct), lambda j, i: (0, j))],
        out_shape=[jax.ShapeDtypeStruct((rws, w), F32), jax.ShapeDtypeStruct((1, w), F32)],
        compiler_params=_params(("parallel", "arbitrary"), 32 << 20),
    )(feats, w1p, b1.reshape(1, hid), w2, b2.reshape(1, hid), w3, w3, freq.reshape(1, hid),
      deltas.reshape(1, w))


def _dft1_kernel(g_ref, x_ref, o_ref):
    o_ref[...] = lax.dot_general(g_ref[...], x_ref[...].astype(BF16), (((2,), (1,)), ((0,), (0,))),
                                 preferred_element_type=F32).astype(o_ref.dtype)


def _dft1(g, x):
    nb, m2, k = g.shape
    c = x.shape[2]
    bt = _pick(nb, (8, 4, 2))
    ct = _pick(c, (256, 128))
    return pl.pallas_call(
        _dft1_kernel, name="dft1",
        grid=(c // ct, nb // bt),
        in_specs=[pl.BlockSpec((bt, m2, k), lambda j, i: (i, 0, 0)),
                  pl.BlockSpec((bt, k, ct), lambda j, i: (i, 0, j))],
        out_specs=pl.BlockSpec((bt, m2, ct), lambda j, i: (i, 0, j)),
        out_shape=jax.ShapeDtypeStruct((nb, m2, c), BF16),
        compiler_params=_params(("parallel", "arbitrary"), 32 << 20),
    )(g, x)


def _spec_kernel(a_ref, f2_ref, sc_ref, o_ref):
    for l in range(a_ref.shape[0]):
        o_ref[l] = jnp.dot(f2_ref[...], a_ref[l].astype(BF16), preferred_element_type=F32) * sc_ref[...]


def _spec(a, f2, scale):
    na, m2, c = a.shape
    kt = _pick(na, (8, 4, 2))
    ct = _pick(c, (256, 128))
    return pl.pallas_call(
        _spec_kernel, name="hy_spec",
        grid=(c // ct, na // kt),
        in_specs=[pl.BlockSpec((kt, m2, ct), lambda j, i: (i, 0, j)),
                  pl.BlockSpec((m2, m2), lambda j, i: (0, 0)),
                  pl.BlockSpec((1, ct), lambda j, i: (0, j))],
        out_specs=pl.BlockSpec((kt, m2, ct), lambda j, i: (i, 0, j)),
        out_shape=jax.ShapeDtypeStruct((na, m2, c), F32),
        compiler_params=_params(("parallel", "arbitrary"), 32 << 20),
    )(a, f2, scale)


def _conv_mid_kernel(a_ref, f2_ref, h_ref, g3_ref, o_ref):
    nbh = a_ref.shape[1] // 2
    for l in range(a_ref.shape[0]):
        x = jnp.dot(f2_ref[...], a_ref[l].astype(BF16), preferred_element_type=F32)
        xr, xi = x[:nbh], x[nbh:]
        hr, hi = h_ref[l, :nbh], h_ref[l, nbh:]
        y = jnp.concatenate([xr * hr - xi * hi, xr * hi + xi * hr], axis=0)
        o_ref[l] = jnp.dot(g3_ref[l], y.astype(BF16), preferred_element_type=F32).astype(o_ref.dtype)


def _conv_mid(a, f2, h, g3):
    na, m2, c = a.shape
    kt = _pick(na, (8, 4, 2))
    ct = _pick(c, (256, 128))
    slab = pl.BlockSpec((kt, m2, ct), lambda j, i: (i, 0, j))
    return pl.pallas_call(
        _conv_mid_kernel, name="conv_mid",
        grid=(c // ct, na // kt),
        in_specs=[slab, pl.BlockSpec((m2, m2), lambda j, i: (0, 0)), slab,
                  pl.BlockSpec((kt, m2, m2), lambda j, i: (i, 0, 0))],
        out_specs=slab,
        out_shape=jax.ShapeDtypeStruct((na, m2, c), BF16),
        compiler_params=_params(("parallel", "arbitrary"), 32 << 20),
    )(a, f2, h, g3)


def _conv_out_kernel(b_ref, f4_ref, o_ref, *, scale):
    for l in range(b_ref.shape[0]):
        o_ref[l] = jnp.dot(f4_ref[...], b_ref[l].astype(BF16), preferred_element_type=F32) * scale


def _conv_out(bm, f4, scale):
    nb, m2, c = bm.shape
    mo = f4.shape[0]
    pt = _pick(nb, (8, 4, 2))
    ct = _pick(c, (256, 128))
    return pl.pallas_call(
        functools.partial(_conv_out_kernel, scale=scale), name="conv_out",
        grid=(c // ct, nb // pt),
        in_specs=[pl.BlockSpec((pt, m2, ct), lambda j, i: (i, 0, j)),
                  pl.BlockSpec((mo, m2), lambda j, i: (0, 0))],
        out_specs=pl.BlockSpec((pt, mo, ct), lambda j, i: (i, 0, j)),
        out_shape=jax.ShapeDtypeStruct((nb, mo, c), F32),
        compiler_params=_params(("parallel", "arbitrary"), 32 << 20),
    )(bm, f4)


def _ctx_conv_kernel(z_ref, ext_ref, sc_ref, o_ref):
    n = z_ref.shape[1]

    def body(s, acc):
        return acc + ext_ref[pl.ds(n - s, n), :] * z_ref[0, pl.ds(s, 1), :]

    acc = lax.fori_loop(0, n, body, jnp.zeros(o_ref.shape[1:], F32))
    o_ref[0] = acc * sc_ref[...]


def _ctx_conv(z, ext, scale):
    b, n, w = z.shape
    ct = LANE
    return pl.pallas_call(
        _ctx_conv_kernel, name="ctx_conv",
        grid=(b, w // ct),
        in_specs=[pl.BlockSpec((1, n, ct), lambda bi, j: (bi, 0, j)),
                  pl.BlockSpec((2 * n, ct), lambda bi, j: (0, j)),
                  pl.BlockSpec((1, ct), lambda bi, j: (0, j))],
        out_specs=pl.BlockSpec((1, n, ct), lambda bi, j: (bi, 0, j)),
        out_shape=jax.ShapeDtypeStruct((b, n, w), F32),
        compiler_params=_params(("parallel", "arbitrary"), 16 << 20),
    )(z, ext, scale)


def _hy_post_kernel(y_ref, z_ref, x0_ref, bias_ref, o_ref):
    o_ref[0] = (x0_ref[0].astype(F32) * (y_ref[0] + bias_ref[...] * z_ref[0].astype(F32))).astype(o_ref.dtype)


def _hy_post(y, z, x0, bias):
    b, t, w = y.shape
    tm = _pick(t, _TM_ROWS)
    s = pl.BlockSpec((1, tm, w), lambda bi, i: (bi, i, 0))
    return pl.pallas_call(
        _hy_post_kernel, name="hy_post",
        grid=(b, t // tm),
        in_specs=[s, s, s, pl.BlockSpec((1, w), lambda bi, i: (0, 0))],
        out_specs=s,
        out_shape=jax.ShapeDtypeStruct((b, t, w), BF16),
        compiler_params=_params(("parallel", "arbitrary"), 32 << 20),
    )(y, z, x0, bias.reshape(1, w))


def _dft_factors(n):
    m = 2 * n
    na = 1 << ((m.bit_length() - 1) // 2)
    return na, m // na


def _cis(num, den, sign):
    ang = (num % den).astype(F32) * (2.0 * math.pi / den)
    return jnp.cos(ang), sign * jnp.sin(ang)


def _cblock(cr, ci):
    return jnp.concatenate([jnp.concatenate([cr, -ci], axis=-1),
                            jnp.concatenate([ci, cr], axis=-1)], axis=-2)


def _dft_tables(n):
    na, nb = _dft_factors(n)
    m = na * nb
    ah = na // 2
    ar = lambda k: jnp.arange(k, dtype=jnp.int32)
    ka, bb = ar(na)[None, :, None], ar(nb)[:, None, None]
    g1c = _cblock(*_cis(ka * (nb * ar(ah)[None, None, :] + bb), m, -1.0))
    g1t = jnp.concatenate(_cis(ka * (nb * ar(na)[None, None, :] + bb), m, -1.0), axis=-2)
    f2 = _cblock(*_cis(ar(nb)[:, None] * ar(nb)[None, :], nb, -1.0))
    g3 = _cblock(*_cis(ar(nb)[None, :, None] * (ar(na)[:, None, None] + na * ar(nb)[None, None, :]), m, 1.0))
    f4 = _cblock(*_cis(ar(ah)[:, None] * ar(na)[None, :], na, 1.0))
    return tuple(t.astype(BF16) for t in (g1c, g1t, f2, g3, f4))


def _filter_feats(n, order):
    bands_n = (HY_EMB - 1) // 2
    t = jnp.linspace(0.0, 1.0, n, dtype=F32)[:, None]
    lag = jnp.arange(n, dtype=F32)[:, None]
    bands = jnp.linspace(1e-4, bands_n - 1, bands_n, dtype=F32)[None, :]
    ang = 2.0 * math.pi * lag * bands / n
    zf = jnp.concatenate([t, jnp.cos(ang), -jnp.sin(ang)], axis=-1)
    if order == "dft":
        na, nb = _dft_factors(n)
        m = (np.arange(na)[None, :] * nb + np.arange(nb)[:, None]).reshape(-1)
    else:
        m = (np.arange(2 * n) - n) % (2 * n)
    lag_of = np.where(m < n, m, np.where(m == n, 0, 2 * n - m))
    cf = np.where(m == 0, 0.5, np.where(m < n, 1.0, 0.0)).astype(np.float32)
    cb = np.where(m == 0, 0.5, np.where(m > n, 1.0, 0.0)).astype(np.float32)
    pad = jnp.zeros((2 * n, LANE - HY_EMB - 2), F32)
    return jnp.concatenate([zf[lag_of], jnp.asarray(cf)[:, None], jnp.asarray(cb)[:, None], pad], axis=-1)


def _hy_deltas(w):
    return jnp.abs(jnp.linspace(math.log(HY_DECAY_TARGET) / HY_SLOW_PCT,
                                math.log(HY_DECAY_TARGET) / HY_FAST_PCT, w, dtype=F32))


def _long_conv_latent(z, taps_perm, inv_l1):
    b, n, w = z.shape
    na, nb = _dft_factors(n)
    ah = na // 2
    g1c, g1t, f2, g3, f4 = _dft_tables(n)
    h = _spec(_dft1(g1t, taps_perm.reshape(nb, na, w)).reshape(nb, 2, na, w).transpose(2, 1, 0, 3)
              .reshape(na, 2 * nb, w), f2, inv_l1)
    outs = []
    for pair in range(b // 2):
        zz = z[2 * pair:2 * pair + 2].astype(BF16).reshape(2, ah, nb, w).transpose(2, 0, 1, 3).reshape(nb, na, w)
        a = _dft1(g1c, zz).reshape(nb, 2, na, w).transpose(2, 1, 0, 3).reshape(na, 2 * nb, w)
        bm = _conv_mid(a, f2, h, g3).reshape(na, 2, nb, w).transpose(2, 1, 0, 3).reshape(nb, 2 * na, w)
        y = _conv_out(bm, f4, 1.0 / (na * nb))
        outs.append(y.reshape(nb, 2, ah, w).transpose(1, 2, 0, 3).reshape(2, n, w))
    return jnp.concatenate(outs, axis=0) if len(outs) > 1 else outs[0]


def _pack_in_proj(w_in, rw_mu, dims):
    depth, d_model, _ = w_in.shape
    w, dl, al, gl = dims
    ld, la, lg = _rup(dl, LANE), _rup(al, LANE), _rup(gl, LANE)
    rw_cols = 3 * w + 2 * dl + 2 * al + gl
    tile = 1024 if d_model >= 2048 else LANE
    na = _rup(3 * w + 2 * ld + 2 * la + lg, tile)
    pieces = [(0, 3 * w, 0), (3 * w, dl, 3 * w), (3 * w + dl, dl, 3 * w + ld),
              (3 * w + 2 * dl, al, 3 * w + 2 * ld), (3 * w + 2 * dl + al, al, 3 * w + 2 * ld + la),
              (3 * w + 2 * dl + 2 * al, gl, 3 * w + 2 * ld + 2 * la)]
    cols, mus = [], []
    orig = np.full((na,), rw_cols - 1, np.int64)
    pos = 0
    for src, width, dst in pieces + [(rw_cols, 0, na)]:
        if dst > pos:
            cols.append(jnp.zeros((depth, d_model, dst - pos), BF16))
            mus.append(jnp.zeros((depth, dst - pos), F32))
        cols.append(w_in[:, :, src:src + width].astype(BF16))
        mus.append(rw_mu[:, src:src + width])
        orig[dst:dst + width] = np.arange(src, src + width)
        pos = dst + width
    wa = _tile_w(jnp.concatenate(cols, axis=2), tile)
    shift = _rup(rw_cols, tile) - rw_cols
    total = shift + w_in.shape[2]
    wrest = _tile_w(_cast_shift(w_in, shift, _rup(total, tile)), tile)
    starts = (rw_cols, rw_cols + 5 * w, rw_cols + 8 * w)
    offs = [(shift + s) // tile for s in starts]
    mu = jnp.concatenate(mus, axis=1).reshape(depth, 1, na)
    grp = np.stack([orig // (rw_cols // 4), orig // (rw_cols // 2)]).astype(np.int32)
    grp = np.minimum(grp, np.array([[3], [1]])).astype(np.int32)
    return wa, wrest, tile, offs, mu, jnp.asarray(grp), (w, ld, la, lg), na


def _pad_rows(a, rows):
    return jnp.pad(a, [(0, 0)] * (a.ndim - 2) + [(0, rows - a.shape[-2]), (0, 0)])


def kernel(x, c, ctx, c_ctx, ada_w, ada_b, norm1_g, norm2_g, w_in, rw_mu, rw_w0, rw_w_up, rw_a0, rw_a_up, rw_g_up, rw_k_k, rw_k_a, rw_r_k, rw_ln_g, rw_ln_b, hg_lower_bounds, hg_norm_g, hy_conv_w, hy_conv_b, hy_f_w1, hy_f_b1, hy_f_w2, hy_f_b2, hy_f_w3, hy_freq, hy_bias, w_branch_a, w_branch_b, w_branch_c, w_out, ffn_w_gate, ffn_w_up, ffn_w_down, final_norm_g):
    bsz, n_lat, d = x.shape
    n_ctx = ctx.shape[1]
    depth = w_in.shape[0]
    w = rw_k_k.shape[1]
    dims = (w, rw_w_up.shape[2], rw_a_up.shape[2], rw_g_up.shape[1])
    assert bsz % 2 == 0 and n_lat % GRID_W == 0 and n_ctx % CHUNK == 0 and w % LANE == 0

    xs = jnp.concatenate([x, ctx], axis=1)
    cvec = jnp.zeros((_rup(bsz + 1, 8), d), F32).at[:bsz].set(c).at[bsz].set(c_ctx)
    lb_cum = jnp.cumsum(jax.nn.softmax(hg_lower_bounds.astype(F32), axis=0), axis=0)
    deltas = _hy_deltas(w)

    wt_rw, wt_in, tile, offs, mu_all, grp, pdims, na = _pack_in_proj(w_in, rw_mu, dims)
    assert (5 * w) % tile == 0 and (3 * w) % tile == 0 and (3 * d) % tile == 0
    _, ld, la, lg = pdims
    hid = ffn_w_gate.shape[2]
    hp = _rup(hid, 1024) if hid > 1024 else _rup(hid, LANE)
    wt_g = _tile_w(_cast_pad(ffn_w_gate, d, hp), _tn_ffn(hp))
    wt_u = _tile_w(_cast_pad(ffn_w_up, d, hp), _tn_ffn(hp))
    wt_d = _tile_w(_cast_pad(ffn_w_down, hp, d), _tn_wide(d), _tk_of(hp))
    wt_o = _tile_w(w_out, _tn_wide(d), _tk_of(d))
    wt_a, wt_b, wt_c = (_tile_w(t, _tn_wide(d)) for t in (w_branch_a, w_branch_b, w_branch_c))

    for layer in range(depth):
        need_ctx = layer < depth - 1
        mod = _ada(cvec, ada_w, ada_b, layer)[:bsz + 1].reshape(bsz + 1, 1, 6, d)
        mods = [mod[:, :, s, :] for s in range(6)]

        h1 = _modnorm(xs, norm1_g[layer], mods[0], mods[1], n_lat)
        pa = _matmul(h1, wt_rw, layer, 0, na // tile, BF16)
        phg = _matmul(h1, wt_in, layer, offs[0], 5 * w // tile, F32)
        rows = None if need_ctx else n_lat
        phy = _matmul(h1, wt_in, layer, offs[1], 3 * w // tile, BF16, rows)
        pgate = _matmul(h1, wt_in, layer, offs[2], 3 * d // tile, BF16, rows)

        r, v, kn, g, bonus, lw, kd, bb = _rw_prep(
            pa, grp, mu_all[layer], rw_w0[layer].reshape(2, 1, w), _pad_rows(rw_w_up[layer], ld),
            rw_a0[layer].reshape(2, 1, w), _pad_rows(rw_a_up[layer], la), _pad_rows(rw_g_up[layer], lg),
            rw_k_k[layer].reshape(1, w), rw_k_a[layer].reshape(1, w), rw_r_k[layer].reshape(1, w),
            n_lat, pdims)
        lb = lb_cum[layer] - lb_cum[0]
        lbt = jnp.stack([jnp.log(lb), jnp.log1p(-lb), 1.0 - lb], axis=1)
        y_f, y_b, o_f, o_b = _scans(r, v, kn, lw, kd, bb, phg, lbt, n_lat)
        oa = _rw_finish(y_f, y_b, bonus, g, rw_ln_g[layer], rw_ln_b[layer])
        ob = _hg_finish(o_f, o_b, phg, hg_norm_g[layer], w)

        x0, z = _hy_pre(phy, hy_conv_w[layer], hy_conv_b[layer], n_lat)
        w1p = _pad_rows(hy_f_w1[layer], LANE)
        filt = (w1p, hy_f_b1[layer], hy_f_w2[layer], hy_f_b2[layer], hy_f_w3[layer], hy_freq[layer], deltas)
        taps, l1 = _hy_taps(_filter_feats(n_lat, "dft"), *filt)
        y_lat = _long_conv_latent(z[:, :n_lat], taps, 1.0 / l1)
        if need_ctx:
            ext, l1c = _hy_taps(_filter_feats(n_ctx, "lag"), *filt)
            y_ctx = _ctx_conv(z[:, n_lat:].astype(F32), ext, 1.0 / l1c)
            y_lat = jnp.concatenate([y_lat, y_ctx], axis=1)
        oc = _hy_post(y_lat, z, x0, hy_bias[layer])

        ym = _merge(oa, ob, oc, wt_a, wt_b, wt_c, layer, pgate)
        xs = _matmul_resid(ym, wt_o, layer, xs, mods[2], n_lat)

        h2 = _modnorm(xs, norm2_g[layer], mods[3], mods[4], n_lat)
        act = _swiglu_up(h2, wt_g, wt_u, layer)
        xs = _matmul_resid(act, wt_d, layer, xs, mods[5], n_lat)

    return _final_norm(xs, final_norm_g, n_lat)
```

```python
import functools
import math

import numpy as np
import jax
import jax.numpy as jnp
from jax import lax
from jax.experimental import pallas as pl
from jax.experimental.pallas import tpu as pltpu

F32 = jnp.float32
BF16 = jnp.bfloat16

GRID_W = 64
CHUNK = 64
SUB = 8
NORM_EPS = 1e-6
RW_HEAD = 64
RW_GN_EPS = 64e-5
HY_EMB = 33
HY_DECAY_TARGET = 1e-2
HY_FAST_PCT = 0.3
HY_SLOW_PCT = 1.5
LANE = 128
VMEM_CAP = 56 * 1024 * 1024
VMEM_SLACK = 8 * 1024 * 1024


def _params(sem, vmem_bytes):
    return pltpu.CompilerParams(dimension_semantics=sem,
                                vmem_limit_bytes=int(min(max(vmem_bytes + VMEM_SLACK, 16 << 20), VMEM_CAP)))


def _pick(n, cands):
    for c in cands:
        if n % c == 0:
            return c
    return n


def _rup(n, m):
    return -(-n // m) * m


def _bdot(a, b):
    return jnp.dot(a.astype(BF16), b.astype(BF16), preferred_element_type=F32)


def _split(x):
    hi = x.astype(BF16)
    lo = (x - hi.astype(F32)).astype(BF16)
    return hi, lo


def _mask_dot(m, x):
    hi, lo = _split(x)
    return (jnp.dot(m, hi, preferred_element_type=F32) + jnp.dot(m, lo, preferred_element_type=F32))


def _x_mask_dot(x, m):
    hi, lo = _split(x)
    return (jnp.dot(hi, m, preferred_element_type=F32) + jnp.dot(lo, m, preferred_element_type=F32))


def _sigmoid(x):
    return 1.0 / (1.0 + jnp.exp(-x))


def _order_masks(d, n, reps=1):
    row = lax.broadcasted_iota(jnp.int32, (n, reps * n), 0)
    col = lax.broadcasted_iota(jnp.int32, (n, reps * n), 1) % n
    diff = (row - col) * (1 - 2 * d)
    return diff >= 0, diff > 0


def _ada_kernel(c_ref, w_ref, b_ref, o_ref):
    c = c_ref[...]
    o_ref[...] = _bdot(c * _sigmoid(c), w_ref[...]) + b_ref[...]


def _ada(cvec, w, b, layer):
    rows, d = cvec.shape
    n = w.shape[2]
    tn = _pick(n, (1024, 512, 256, 128))
    return pl.pallas_call(
        _ada_kernel, name="ada",
        grid=(n // tn,),
        in_specs=[pl.BlockSpec((rows, d), lambda j: (0, 0)),
                  pl.BlockSpec((None, d, tn), lambda j: (layer, 0, j)),
                  pl.BlockSpec((None, 1, tn), lambda j: (layer, 0, j))],
        out_specs=pl.BlockSpec((rows, tn), lambda j: (0, j)),
        out_shape=jax.ShapeDtypeStruct((rows, n), F32),
        compiler_params=_params(("arbitrary",), 3 * d * tn * 4),
    )(cvec, w, b.reshape(b.shape[0], 1, n))


def _modnorm_kernel(x_ref, g_ref, shl_ref, scl_ref, shc_ref, scc_ref, o_ref, *, n_lat):
    tm = x_ref.shape[1]
    x = x_ref[0]
    y = x * lax.rsqrt(jnp.mean(x * x, axis=-1, keepdims=True) + NORM_EPS) * g_ref[...]
    pos = pl.program_id(1) * tm + lax.broadcasted_iota(jnp.int32, (tm, 1), 0)
    is_ctx = pos >= n_lat
    sc = jnp.where(is_ctx, scc_ref[0], scl_ref[0])
    sh = jnp.where(is_ctx, shc_ref[0], shl_ref[0])
    o_ref[0] = (y * (1.0 + sc) + sh).astype(o_ref.dtype)


def _modnorm(x, gain, shift, scale, n_lat):
    b, t, d = x.shape
    tm = _pick(t, (512, 384, 256, 128, 64))
    vec = lambda f: pl.BlockSpec((1, 1, d), f)
    return pl.pallas_call(
        functools.partial(_modnorm_kernel, n_lat=n_lat), name="modnorm",
        grid=(b, t // tm),
        in_specs=[pl.BlockSpec((1, tm, d), lambda bi, i: (bi, i, 0)),
                  pl.BlockSpec((1, d), lambda bi, i: (0, 0)),
                  vec(lambda bi, i: (bi, 0, 0)), vec(lambda bi, i: (bi, 0, 0)),
                  vec(lambda bi, i: (b, 0, 0)), vec(lambda bi, i: (b, 0, 0))],
        out_specs=pl.BlockSpec((1, tm, d), lambda bi, i: (bi, i, 0)),
        out_shape=jax.ShapeDtypeStruct((b, t, d), BF16),
        compiler_params=_params(("parallel", "arbitrary"), 6 * tm * d * 4),
    )(x, gain.reshape(1, d), shift, scale, shift, scale)


def _rmsnorm_kernel(x_ref, g_ref, o_ref):
    x = x_ref[0]
    o_ref[0] = x * lax.rsqrt(jnp.mean(x * x, axis=-1, keepdims=True) + NORM_EPS) * g_ref[...]


def _final_norm(x, gain, n_lat):
    b, _, d = x.shape
    tm = _pick(n_lat, (512, 384, 256, 128, 64))
    return pl.pallas_call(
        _rmsnorm_kernel, name="final_norm",
        grid=(b, n_lat // tm),
        in_specs=[pl.BlockSpec((1, tm, d), lambda bi, i: (bi, i, 0)),
                  pl.BlockSpec((1, d), lambda bi, i: (0, 0))],
        out_specs=pl.BlockSpec((1, tm, d), lambda bi, i: (bi, i, 0)),
        out_shape=jax.ShapeDtypeStruct((b, n_lat, d), F32),
        compiler_params=_params(("parallel", "arbitrary"), 6 * tm * d * 4),
    )(x, gain.reshape(1, d))


def _mm_kernel(x_ref, w_ref, o_ref):
    o_ref[0] = jnp.dot(x_ref[0], w_ref[...], preferred_element_type=F32).astype(o_ref.dtype)


def _tn_wide(n):
    return _pick(n, (1024, 512, 256, 128))


def _tn_ffn(n):
    return _pick(n, (512, 256, 128))


def _tk_of(k):
    return k if k <= 2048 else _pick(k, (2816, 2048, 1024, 512, 256, 128))


class _TiledW:
    def __init__(self, w, tn, tk=None):
        self.w = w.astype(BF16)
        self.tk = w.shape[1] if tk is None else tk
        self.tn = tn
        self.n = w.shape[2]


def _tile_w(w, tn, tk=None):
    return _TiledW(w, tn, tk)


def _cast_pad_kernel(x_ref, o_ref, *, n_in, cols):
    i = pl.program_id(1)

    @pl.when(i < n_in)
    def _():
        o_ref[0, :, :cols] = x_ref[0].astype(o_ref.dtype)
        if cols < o_ref.shape[2]:
            o_ref[0, :, cols:] = jnp.zeros((o_ref.shape[1], o_ref.shape[2] - cols), o_ref.dtype)

    @pl.when(i >= n_in)
    def _():
        o_ref[0] = jnp.zeros(o_ref.shape[1:], o_ref.dtype)


def _cast_pad(w, rows_out, cols_out):
    l, r, c = w.shape
    tr = _pick(math.gcd(r, rows_out), (256, 128, 64, 32, 16))
    n_in = r // tr
    return pl.pallas_call(
        functools.partial(_cast_pad_kernel, n_in=n_in, cols=c), name="cast_pad",
        grid=(l, rows_out // tr),
        in_specs=[pl.BlockSpec((1, tr, c), lambda li, i: (li, jnp.minimum(i, n_in - 1), 0))],
        out_specs=pl.BlockSpec((1, tr, cols_out), lambda li, i: (li, i, 0)),
        out_shape=jax.ShapeDtypeStruct((l, rows_out, cols_out), BF16),
        compiler_params=_params(("parallel", "arbitrary"), 2 * tr * (c * 4 + cols_out * 2)),
    )(w)


def _wspec(tk, tn, f):
    def idx(*a):
        lay, j, l = f(*a)[:3]
        return lay, l, j
    return pl.BlockSpec((None, tk, tn), idx)


_TM_ROWS = (1024, 768, 512, 384, 256, 128, 64)


def _matmul(x, wt, layer, j0, nj, out_dtype, rows=None):
    b, _, k = x.shape
    t = x.shape[1] if rows is None else rows
    tn = wt.tn
    n = nj * tn
    tm = _pick(t, _TM_ROWS)
    osz = jnp.dtype(out_dtype).itemsize
    return pl.pallas_call(
        _mm_kernel, name="in_proj",
        grid=(b, t // tm, n // tn),
        in_specs=[pl.BlockSpec((1, tm, k), lambda bi, i, j: (bi, i, 0)),
                  _wspec(k, tn, lambda bi, i, j: (layer, j0 + j, 0, 0, 0))],
        out_specs=pl.BlockSpec((1, tm, tn), lambda bi, i, j: (bi, i, j)),
        out_shape=jax.ShapeDtypeStruct((b, t, n), out_dtype),
        compiler_params=_params(("parallel", "parallel", "arbitrary"),
                                2 * (tm * k * 2 + k * tn * 2 + tm * tn * osz) + tm * tn * 4),
    )(x, wt.w)


def _swiglu_kernel(x_ref, wg_ref, wu_ref, o_ref):
    x = x_ref[0]
    g = jnp.dot(x, wg_ref[...], preferred_element_type=F32)
    u = jnp.dot(x, wu_ref[...], preferred_element_type=F32)
    o_ref[0] = (g * _sigmoid(g) * u).astype(o_ref.dtype)


def _swiglu_up(x, wg, wu, layer):
    b, t, k = x.shape
    tn = wg.tn
    n = wg.n
    tm = _pick(t, _TM_ROWS)
    wsp = lambda: _wspec(k, tn, lambda bi, i, j: (layer, j, 0, 0, 0))
    return pl.pallas_call(
        _swiglu_kernel, name="swiglu_up",
        grid=(b, t // tm, n // tn),
        in_specs=[pl.BlockSpec((1, tm, k), lambda bi, i, j: (bi, i, 0)), wsp(), wsp()],
        out_specs=pl.BlockSpec((1, tm, tn), lambda bi, i, j: (bi, i, j)),
        out_shape=jax.ShapeDtypeStruct((b, t, n), BF16),
        compiler_params=_params(("parallel", "parallel", "arbitrary"),
                                2 * (tm * k * 2 + 2 * k * tn * 2 + tm * tn * 2) + 3 * tm * tn * 4),
    )(x, wg.w, wu.w)


def _resid_kernel(x_ref, w_ref, r_ref, gl_ref, gc_ref, o_ref, acc_ref, *, n_lat):
    kk = pl.program_id(3)

    @pl.when(kk == 0)
    def _():
        acc_ref[...] = jnp.zeros_like(acc_ref)

    acc_ref[...] += jnp.dot(x_ref[0], w_ref[...], preferred_element_type=F32)

    @pl.when(kk == pl.num_programs(3) - 1)
    def _():
        tm = acc_ref.shape[0]
        pos = pl.program_id(1) * tm + lax.broadcasted_iota(jnp.int32, (tm, 1), 0)
        gate = jnp.where(pos >= n_lat, gc_ref[0], gl_ref[0])
        o_ref[0] = r_ref[0] + gate * acc_ref[...]


def _matmul_resid(x, w, layer, res, gate, n_lat):
    b, t, k = x.shape
    tk, tn = w.tk, w.tn
    n = w.n
    tm = _pick(t, _TM_ROWS)
    return pl.pallas_call(
        functools.partial(_resid_kernel, n_lat=n_lat), name="proj_resid",
        grid=(b, t // tm, n // tn, k // tk),
        in_specs=[pl.BlockSpec((1, tm, tk), lambda bi, i, j, l: (bi, i, l)),
                  _wspec(tk, tn, lambda bi, i, j, l: (layer, j, l, 0, 0)),
                  pl.BlockSpec((1, tm, tn), lambda bi, i, j, l: (bi, i, j)),
                  pl.BlockSpec((1, 1, tn), lambda bi, i, j, l: (bi, 0, j)),
                  pl.BlockSpec((1, 1, tn), lambda bi, i, j, l: (b, 0, j))],
        out_specs=pl.BlockSpec((1, tm, tn), lambda bi, i, j, l: (bi, i, j)),
        out_shape=jax.ShapeDtypeStruct((b, t, n), F32),
        scratch_shapes=[pltpu.VMEM((tm, tn), F32)],
        compiler_params=_params(("parallel", "parallel", "arbitrary", "arbitrary"),
                                2 * (tm * tk * 2 + tk * tn * 2 + 2 * tm * tn * 4) + 2 * tm * tn * 4),
    )(x, w.w, res, gate, gate)


def _merge_kernel(oa_ref, ob_ref, oc_ref, wa_ref, wb_ref, wc_ref, ga_ref, gb_ref, gc_ref, o_ref):
    y = _sigmoid(ga_ref[0].astype(F32)) * jnp.dot(oa_ref[0], wa_ref[...], preferred_element_type=F32)
    y += _sigmoid(gb_ref[0].astype(F32)) * jnp.dot(ob_ref[0], wb_ref[...], preferred_element_type=F32)
    y += _sigmoid(gc_ref[0].astype(F32)) * jnp.dot(oc_ref[0], wc_ref[...], preferred_element_type=F32)
    o_ref[0] = y.astype(o_ref.dtype)


def _merge(oa, ob, oc, wa, wb, wc, layer, pgate):
    b, _, kw = oa.shape
    t = pgate.shape[1]
    tn = wa.tn
    nj = wa.n // tn
    d = nj * tn
    tm = _pick(t, (768, 512, 384, 256, 128, 64))
    br = lambda: pl.BlockSpec((1, tm, kw), lambda bi, i, j: (bi, i, 0))
    wt = lambda: _wspec(kw, tn, lambda bi, i, j: (layer, j, 0, 0, 0))
    gt = lambda s: pl.BlockSpec((1, tm, tn), lambda bi, i, j: (bi, i, s * nj + j))
    gsz = jnp.dtype(pgate.dtype).itemsize
    return pl.pallas_call(
        _merge_kernel, name="merge",
        grid=(b, t // tm, nj),
        in_specs=[br(), br(), br(), wt(), wt(), wt(), gt(0), gt(1), gt(2)],
        out_specs=pl.BlockSpec((1, tm, tn), lambda bi, i, j: (bi, i, j)),
        out_shape=jax.ShapeDtypeStruct((b, t, d), BF16),
        compiler_params=_params(("parallel", "parallel", "arbitrary"),
                                2 * (3 * tm * kw * 2 + 3 * kw * tn * 2 + 3 * tm * tn * gsz + tm * tn * 2)
                                + 4 * tm * tn * 4),
    )(oa, ob, oc, wa.w, wb.w, wc.w, pgate, pgate, pgate)


def _pair_sum_matrix():
    r = lax.broadcasted_iota(jnp.int32, (LANE, LANE), 0) // RW_HEAD
    c = lax.broadcasted_iota(jnp.int32, (LANE, LANE), 1) // RW_HEAD
    return jnp.where(r == c, 1.0, 0.0).astype(BF16)


def _head_sums(x, hs):
    rows, w = x.shape
    nt = w // LANE
    stacked = jnp.concatenate([x[:, j * LANE:(j + 1) * LANE] for j in range(nt)], axis=0)
    s = _x_mask_dot(stacked, hs)
    return jnp.concatenate([s[j * rows:(j + 1) * rows] for j in range(nt)], axis=1)


def _rw_prep_kernel(prev_ref, cur_ref, next_ref, grp_ref, mu_ref, w0_ref, wup_ref, a0_ref, aup_ref,
                    gup_ref, kk_ref, ka_ref, rk_ref,
                    r_ref, v_ref, kn_ref, g_ref, bon_ref, lw_ref, kd_ref, bb_ref,
                    *, n_lat_chunks, n_chunks, w, ld, la, lg):
    i = pl.program_id(1)
    c = CHUNK
    cur = cur_ref[0].astype(F32)
    prev = prev_ref[0].astype(F32)
    nxt = next_ref[0].astype(F32)
    is_ctx = i >= n_lat_chunks
    first = jnp.logical_or(i == 0, i == n_lat_chunks)
    last = jnp.logical_or(i == n_lat_chunks - 1, i == n_chunks - 1)
    row = lax.broadcasted_iota(jnp.int32, (c, 1), 0)
    carry_in = jnp.where(jnp.logical_and(is_ctx, jnp.logical_not(first)), 1.0, 0.0)
    carry_out = jnp.where(jnp.logical_and(is_ctx, jnp.logical_not(last)), 1.0, 0.0)
    tm1 = jnp.where(row == 0, prev[c - 1:c] * carry_in, pltpu.roll(cur, 1, 0))
    tp1 = jnp.where(row == c - 1, nxt[0:1] * carry_out, pltpu.roll(cur, c - 1, 0))
    lat_up = jnp.where(jnp.logical_or(is_ctx, first), 0.0, 1.0)
    lat_dn = jnp.where(jnp.logical_or(is_ctx, last), 0.0, 1.0)
    code = jnp.where(is_ctx, grp_ref[1:2], grp_ref[0:1])
    shifted = jnp.where(code == 0, tm1,
                        jnp.where(code == 1, tp1,
                                  jnp.where(code == 2, prev * lat_up, nxt * lat_dn)))
    m = cur + mu_ref[...] * (shifted - cur)

    r = m[:, 0:w]
    k = m[:, w:2 * w]
    v = m[:, 2 * w:3 * w]
    o = 3 * w
    wd = (m[:, o:o + ld], m[:, o + ld:o + 2 * ld])
    ad = (m[:, o + 2 * ld:o + 2 * ld + la], m[:, o + 2 * ld + la:o + 2 * ld + 2 * la])
    gd = m[:, o + 2 * ld + 2 * la:o + 2 * ld + 2 * la + lg]

    hs = _pair_sum_matrix()
    g_ref[0] = _bdot(_sigmoid(gd), gup_ref[...]).astype(g_ref.dtype)
    kx = k * kk_ref[...]
    kn = kx * lax.rsqrt(jnp.maximum(_head_sums(kx * kx, hs), 1e-24))
    r_ref[0] = r.astype(r_ref.dtype)
    v_ref[0] = v.astype(v_ref.dtype)
    kn_ref[0] = kn.astype(kn_ref.dtype)
    bonus = jnp.zeros_like(r)
    for d in range(2):
        wl = w0_ref[d] + _bdot(jnp.tanh(wd[d]), wup_ref[d])
        lw_ref[d, 0] = -math.exp(-0.5) * _sigmoid(wl)
        a = _sigmoid(a0_ref[d] + _bdot(ad[d], aup_ref[d]))
        kd = k * (1.0 + (a - 1.0) * ka_ref[...])
        kd_ref[d, 0] = kd.astype(kd_ref.dtype)
        bb_ref[d, 0] = (kn * a).astype(bb_ref.dtype)
        bonus += r * kd * rk_ref[...]
    bon_ref[0] = (_head_sums(bonus, hs) * v).astype(bon_ref.dtype)


def _rw_prep(pa, grp, mu, w0, wup, a0, aup, gup, k_k, k_a, r_k, n_lat, dims):
    b, t, na = pa.shape
    w, ld, la, lg = dims
    nc = t // CHUNK
    nlc = n_lat // CHUNK
    blk = lambda f: pl.BlockSpec((1, CHUNK, na), f)
    full = lambda a: pl.BlockSpec(a.shape, lambda bi, i: (0,) * a.ndim)
    o1 = pl.BlockSpec((1, CHUNK, w), lambda bi, i: (bi, i, 0))
    o2 = pl.BlockSpec((2, 1, CHUNK, w), lambda bi, i: (0, bi, i, 0))
    s1 = jax.ShapeDtypeStruct((b, t, w), BF16)
    s2 = jax.ShapeDtypeStruct((2, b, t, w), BF16)
    s2f = jax.ShapeDtypeStruct((2, b, t, w), F32)
    consts = (grp, mu, w0, wup, a0, aup, gup, k_k, k_a, r_k)
    return pl.pallas_call(
        functools.partial(_rw_prep_kernel, n_lat_chunks=nlc, n_chunks=nc, w=w, ld=ld, la=la, lg=lg),
        name="rw_prep",
        grid=(b, nc),
        in_specs=[blk(lambda bi, i: (bi, jnp.maximum(i - 1, 0), 0)),
                  blk(lambda bi, i: (bi, i, 0)),
                  blk(lambda bi, i: (bi, jnp.minimum(i + 1, nc - 1), 0))] + [full(a) for a in consts],
        out_specs=[o1, o1, o1, o1, o1, o2, o2, o2],
        out_shape=[s1, s1, s1, s1, s1, s2f, s2, s2],
        compiler_params=_params(("parallel", "arbitrary"), 32 << 20),
    )(pa, pa, pa, *consts)


def _pdot(a, b):
    return lax.dot_general(a.astype(BF16), b.astype(BF16), (((2,), (1,)), ((0,), (0,))),
                           preferred_element_type=F32)


def _pdot_nt(a, b):
    return lax.dot_general(a.astype(BF16), b.astype(BF16), (((2,), (2,)), ((0,), (0,))),
                           preferred_element_type=F32)


def _pdot_tn(a, b):
    return lax.dot_general(a.astype(BF16), b.astype(BF16), (((1,), (1,)), ((0,), (0,))),
                           preferred_element_type=F32)


def _rw_scan_dir(d, r_ref, v_ref, kn_ref, lw_ref, kd_ref, bb_ref, y_ref, s_ref, n_pairs):
    c = CHUNK
    wfull = n_pairs * LANE
    incl_bf = jnp.where(_order_masks(d, c)[0], 1.0, 0.0).astype(BF16)
    incl2, strict2 = _order_masks(d, c, 2)
    lane_w = lax.broadcasted_iota(jnp.int32, (1, wfull), 1) % LANE
    w0 = jnp.where(lane_w < RW_HEAD, 1.0, 0.0)
    w1 = 1.0 - w0
    lane = lax.broadcasted_iota(jnp.int32, (1, 1, LANE), 2)
    m0 = jnp.where(lane < RW_HEAD, 1.0, 0.0)
    m1 = 1.0 - m0
    rowh = lax.broadcasted_iota(jnp.int32, (LANE, LANE), 0) // RW_HEAD
    colh = lax.broadcasted_iota(jnp.int32, (LANE, LANE), 1) // RW_HEAD
    blockdiag = rowh == colh
    last_row = jnp.where(d == 0, c - 1, 0)
    rsel = lax.broadcasted_iota(jnp.int32, (c, 1), 0) == last_row

    def pairs(x):
        return jnp.stack([x[:, p * LANE:(p + 1) * LANE] for p in range(n_pairs)], axis=0)

    def stack(x):
        return jnp.concatenate([x * m0, x * m1], axis=1)

    lw = lw_ref[0, 0]
    r = r_ref[0].astype(F32)
    v = v_ref[0].astype(F32)
    kd = kd_ref[0, 0].astype(F32)
    bb = bb_ref[0, 0].astype(F32)
    cw = _mask_dot(incl_bf, lw)
    tot = jnp.sum(jnp.where(rsel, cw, 0.0), axis=0, keepdims=True)
    w_inv = jnp.exp(-cw)
    w_rem = jnp.exp(tot - cw)
    a_t = -kn_ref[0].astype(F32) * jnp.exp(cw - lw)
    r_t = r * jnp.exp(cw)
    b_t = bb * w_inv
    k_t = kd * w_inv
    ar = pairs(jnp.concatenate([a_t, r_t], axis=0))
    bk = pairs(jnp.concatenate([b_t * w0, b_t * w1, k_t * w0, k_t * w1], axis=0))
    vst = pairs(jnp.concatenate([v * w0, v * w1], axis=0))
    uvr = pairs(jnp.concatenate([bb * w_rem, kd * w_rem], axis=0))
    v3 = pairs(v)
    decay = pairs(jnp.exp(tot))
    s0 = s_ref[...]
    yield

    sc = _pdot_nt(ar, bk)
    a_ab = jnp.where(strict2, sc[:, :c, :2 * c], 0.0)
    a_ak = jnp.where(strict2, sc[:, :c, 2 * c:], 0.0)
    r_b = jnp.where(incl2, sc[:, c:, :2 * c], 0.0)
    r_k = jnp.where(incl2, sc[:, c:, 2 * c:], 0.0)

    abd = stack(a_ab)
    yield
    q = _pdot(abd, abd)
    nsum = abd
    for _ in range(4):
        yield
        both = _pdot(q, jnp.concatenate([nsum, q], axis=2))
        nsum = nsum + q + both[:, :, :LANE]
        q = both[:, :, LANE:]
    yield
    nsum = nsum + q + _pdot(q, nsum)
    n_side = nsum[:, :c] + nsum[:, c:]

    yield
    art = _pdot_nt(ar, s0)
    rhs = art[:, :c] + _pdot(a_ak, vst)
    yield
    u = rhs + _pdot(n_side, stack(rhs))
    yield
    y = art[:, c:] + _pdot(jnp.concatenate([r_b, r_k], axis=2), jnp.concatenate([stack(u), vst], axis=1))
    for p in range(n_pairs):
        y_ref[0, :, p * LANE:(p + 1) * LANE] = y[p]
    yield
    upd = _pdot_tn(jnp.concatenate([u, v3], axis=1), uvr)
    s_ref[...] = s0 * decay + jnp.where(blockdiag, upd, 0.0)


def _scan_chunk(d, i, n_lat_chunks, n_chunks):
    nctx = n_chunks - n_lat_chunks
    in_ctx = i < nctx
    fwd = jnp.where(in_ctx, n_lat_chunks + i, i - nctx)
    bwd = jnp.where(in_ctx, n_chunks - 1 - i, n_chunks - 1 - i)
    return jnp.where(d == 0, fwd, bwd)


def _rw_finish_kernel(yf_ref, yb_ref, bon_ref, g_ref, lg_ref, lb_ref, o_ref):
    y = yf_ref[0] + yb_ref[0]
    hs = _pair_sum_matrix()
    mean = _head_sums(y, hs) * (1.0 / RW_HEAD)
    yc = y - mean
    var = _head_sums(yc * yc, hs) * (1.0 / RW_HEAD)
    yn = yc * lax.rsqrt(var + RW_GN_EPS)
    o_ref[0] = ((yn * lg_ref[...] + lb_ref[...] + bon_ref[0].astype(F32)) * g_ref[0].astype(F32)).astype(o_ref.dtype)


def _rw_finish(y_f, y_b, bonus, g, ln_g, ln_b):
    b, t, w = y_f.shape
    tm = _pick(t, _TM_ROWS)
    s1 = pl.BlockSpec((1, tm, w), lambda bi, i: (bi, i, 0))
    vec = pl.BlockSpec((1, w), lambda bi, i: (0, 0))
    return pl.pallas_call(
        _rw_finish_kernel, name="rw_finish",
        grid=(b, t // tm),
        in_specs=[s1, s1, s1, s1, vec, vec],
        out_specs=s1,
        out_shape=jax.ShapeDtypeStruct((b, t, w), BF16),
        compiler_params=_params(("parallel", "arbitrary"), 32 << 20),
    )(y_f, y_b, bonus, g, ln_g.reshape(1, w), ln_b.reshape(1, w))


def _hg_scan_dir(d, q_ref, z_ref, i_ref, lb_ref, o_ref, s_ref, n_heads):
    c = CHUNK
    nb = c // SUB
    sgn = 1 - 2 * d
    fwd = d == 0
    incl_bf = jnp.where(_order_masks(d, c)[0], 1.0, 0.0).astype(BF16)
    row = lax.broadcasted_iota(jnp.int32, (c, c), 0)
    col = lax.broadcasted_iota(jnp.int32, (c, c), 1)
    rb, cbk = row // SUB, col // SUB
    lvl_b = (rb - cbk) * sgn > 0
    lvl_1 = jnp.logical_and(rb == cbk, (row - col) * sgn >= 0)
    last_row = jnp.where(d == 0, c - 1, 0)
    rsel = lax.broadcasted_iota(jnp.int32, (c, 1), 0) == last_row

    def heads(x):
        return jnp.stack([x[:, h * LANE:(h + 1) * LANE] for h in range(n_heads)], axis=0)

    q = q_ref[0]
    z = z_ref[0]
    val = i_ref[0]
    lg_l = lb_ref[d, 0:1]
    lg_1ml = lb_ref[d, 1:2]
    one_ml = lb_ref[d, 2:3]
    s0 = s_ref[...]

    ez = jnp.exp(-jnp.abs(z))
    log_sig = jnp.minimum(z, 0.0) - jnp.log(1.0 + ez)
    x2 = lg_1ml + log_sig
    mx = jnp.maximum(lg_l, x2)
    log_f = mx + jnp.log(jnp.exp(lg_l - mx) + jnp.exp(x2 - mx))
    k = one_ml * jnp.where(z >= 0.0, ez, 1.0) / (1.0 + ez)
    yield

    cb = _mask_dot(incl_bf, log_f)
    tot = jnp.sum(jnp.where(rsel, cb, 0.0), axis=0, keepdims=True)
    wfull = cb.shape[1]
    cb3 = cb.reshape(nb, SUB, wfull)
    blk_end = cb3[:, SUB - 1:SUB] if fwd else cb3[:, 0:1]

    end_own = jnp.broadcast_to(blk_end, (nb, SUB, wfull)).reshape(c, wfull)
    refs = [(blk_end[j], slice(0, c)) for j in range(nb)]
    refs += [(jnp.broadcast_to(cb3[:, j:j + 1], (nb, SUB, wfull)).reshape(c, wfull), slice(c, 2 * c))
             for j in range(SUB)]
    which = jnp.where(lvl_b, cbk, jnp.where(lvl_1, nb + col % SUB, -1))
    ks = heads(jnp.concatenate([k * jnp.exp(end_own - cb), k], axis=0))
    q_bf = q.astype(BF16)
    att = jnp.zeros((n_heads, c, c), F32)
    group = 4
    for g0 in range(0, len(refs), group):
        yield
        part = refs[g0:g0 + group]
        qs = heads(jnp.concatenate([q_bf * jnp.exp(jnp.minimum(cb - e, 0.0)).astype(BF16) for e, _ in part],
                                   axis=0))
        sc = _pdot_nt(qs, ks)
        for n, (_, cols) in enumerate(part):
            att = jnp.where(which == g0 + n, sc[:, n * c:(n + 1) * c, cols], att)
    yield
    v3 = heads(val)
    o = _pdot(att, v3) + _pdot_nt(heads(q * jnp.exp(cb)), s0)
    for h in range(n_heads):
        o_ref[0, :, h * LANE:(h + 1) * LANE] = o[h]
    yield
    s_ref[...] = s0 * heads(jnp.exp(tot)) + _pdot_tn(v3, heads(k * jnp.exp(tot - cb)))


def _scans_kernel(*refs, n_pairs, n_heads):
    rw_in, hg_in = refs[0:12], refs[12:19]
    y_f, y_b, o_f, o_b, s_rw, s_hg = refs[19:25]

    @pl.when(pl.program_id(1) == 0)
    def _():
        s_rw[...] = jnp.zeros_like(s_rw)
        s_hg[...] = jnp.zeros_like(s_hg)

    live = [_rw_scan_dir(0, *rw_in[0:6], y_f, s_rw.at[0], n_pairs),
            _hg_scan_dir(0, *hg_in[0:3], hg_in[6], o_f, s_hg.at[0], n_heads),
            _rw_scan_dir(1, *rw_in[6:12], y_b, s_rw.at[1], n_pairs),
            _hg_scan_dir(1, *hg_in[3:6], hg_in[6], o_b, s_hg.at[1], n_heads)]
    while live:
        for gen in list(live):
            if next(gen, StopIteration) is StopIteration:
                live.remove(gen)


def _scans(r, v, kn, lw, kd, bb, phg, lbt, n_lat):
    b, t, w = r.shape
    nc = t // CHUNK
    nlc = n_lat // CHUNK
    ch = functools.partial(_scan_chunk, n_lat_chunks=nlc, n_chunks=nc)
    col = lambda d, cb_: pl.BlockSpec((1, CHUNK, w), lambda bi, i: (bi, ch(d, i), cb_))
    s2 = lambda d: pl.BlockSpec((1, 1, CHUNK, w), lambda bi, i: (d, bi, ch(d, i), 0))
    rw_dir = lambda d: [col(d, 0), col(d, 0), col(d, 0), s2(d), s2(d), s2(d)]
    out = jax.ShapeDtypeStruct((b, t, w), F32)
    return pl.pallas_call(
        functools.partial(_scans_kernel, n_pairs=w // LANE, n_heads=w // LANE), name="scans",
        grid=(b, nc),
        in_specs=rw_dir(0) + rw_dir(1)
        + [col(0, 0), col(0, 1), col(0, 3), col(1, 0), col(1, 2), col(1, 3),
           pl.BlockSpec((2, 3, w), lambda bi, i: (0, 0, 0))],
        out_specs=[col(0, 0), col(1, 0), col(0, 0), col(1, 0)],
        out_shape=[out, out, out, out],
        scratch_shapes=[pltpu.VMEM((2, w // LANE, LANE, LANE), F32), pltpu.VMEM((2, w // LANE, LANE, LANE), F32)],
        compiler_params=_params(("arbitrary", "arbitrary"), 40 << 20),
    )(r, v, kn, lw, kd, bb, r, v, kn, lw, kd, bb, phg, phg, phg, phg, phg, phg, lbt)


def _hg_finish_kernel(of_ref, ob_ref, g_ref, ng_ref, out_ref, *, n_heads):
    for h in range(n_heads):
        sl = slice(h * LANE, (h + 1) * LANE)
        o = of_ref[0, :, sl] + ob_ref[0, :, sl]
        y = o * lax.rsqrt(jnp.mean(o * o, axis=-1, keepdims=True) + NORM_EPS) * ng_ref[...]
        g = g_ref[0, :, sl]
        out_ref[0, :, sl] = (y * g * _sigmoid(g)).astype(out_ref.dtype)


def _hg_finish(o_f, o_b, phg, norm_g, w):
    b, t, _ = o_f.shape
    tm = _pick(t, _TM_ROWS)
    return pl.pallas_call(
        functools.partial(_hg_finish_kernel, n_heads=w // LANE), name="hg_finish",
        grid=(b, t // tm),
        in_specs=[pl.BlockSpec((1, tm, w), lambda bi, i: (bi, i, 0)),
                  pl.BlockSpec((1, tm, w), lambda bi, i: (bi, i, 0)),
                  pl.BlockSpec((1, tm, w), lambda bi, i: (bi, i, 4)),
                  pl.BlockSpec((1, LANE), lambda bi, i: (0, 0))],
        out_specs=pl.BlockSpec((1, tm, w), lambda bi, i: (bi, i, 0)),
        out_shape=jax.ShapeDtypeStruct((b, t, w), BF16),
        compiler_params=_params(("parallel", "arbitrary"), 32 << 20),
    )(o_f, o_b, phg, norm_g.reshape(1, LANE))


def _hy_pre_kernel(prev_ref, cur_ref, next_ref, cw_ref, cb_ref, x0_ref, z_ref, *, n_lat_blocks, n_blocks, w):
    i = pl.program_id(1)
    tm = cur_ref.shape[1]
    hb = prev_ref.shape[1]
    cur = cur_ref[0].astype(F32)
    first = jnp.logical_or(i == 0, i == n_lat_blocks)
    last = jnp.logical_or(i == n_lat_blocks - 1, i == n_blocks - 1)
    row = lax.broadcasted_iota(jnp.int32, (tm, 1), 0)
    p_last = prev_ref[0, hb - 1:hb].astype(F32) * jnp.where(first, 0.0, 1.0)
    n_first = next_ref[0, 0:1].astype(F32) * jnp.where(last, 0.0, 1.0)
    before = jnp.where(row == 0, p_last, pltpu.roll(cur, 1, 0))
    after = jnp.where(row == tm - 1, n_first, pltpu.roll(cur, tm - 1, 0))
    u = cw_ref[0:1] * before + cw_ref[1:2] * cur + cw_ref[2:3] * after + cb_ref[...]
    x0_ref[0] = u[:, 0:w].astype(x0_ref.dtype)
    z_ref[0] = (u[:, w:2 * w] * u[:, 2 * w:3 * w]).astype(z_ref.dtype)


def _hy_pre(phy, conv_w, conv_b, n_lat):
    b, t, w3 = phy.shape
    w = w3 // 3
    tm = _pick(math.gcd(n_lat, t - n_lat), (256, 128, 64))
    nb = t // tm
    hb = 16
    per = tm // hb
    o = pl.BlockSpec((1, tm, w), lambda bi, i: (bi, i, 0))
    s = jax.ShapeDtypeStruct((b, t, w), BF16)
    return pl.pallas_call(
        functools.partial(_hy_pre_kernel, n_lat_blocks=n_lat // tm, n_blocks=nb, w=w), name="hy_pre",
        grid=(b, nb),
        in_specs=[pl.BlockSpec((1, hb, w3), lambda bi, i: (bi, jnp.maximum(i * per - 1, 0), 0)),
                  pl.BlockSpec((1, tm, w3), lambda bi, i: (bi, i, 0)),
                  pl.BlockSpec((1, hb, w3), lambda bi, i: (bi, jnp.minimum((i + 1) * per, t // hb - 1), 0)),
                  pl.BlockSpec((3, w3), lambda bi, i: (0, 0)),
                  pl.BlockSpec((1, w3), lambda bi, i: (0, 0))],
        out_specs=[o, o],
        out_shape=[s, s],
        compiler_params=_params(("parallel", "arbitrary"), 40 << 20),
    )(phy, phy, phy, conv_w, conv_b.reshape(1, w3))


def _hdot(a, b):
    return jnp.dot(a, b, precision=lax.Precision.HIGHEST, preferred_element_type=F32)


def _hy_taps_kernel(ft_ref, w1_ref, b1_ref, w2_ref, b2_ref, w3f_ref, w3b_ref, fr_ref, dl_ref,
                    tap_ref, sum_ref):
    i = pl.program_id(1)
    ft = ft_ref[...]
    fr = fr_ref[...]
    h = jnp.sin(fr * (_hdot(ft, w1_ref[...]) + b1_ref[...]))
    h = jnp.sin(fr * (_hdot(h, w2_ref[...]) + b2_ref[...]))
    cf = ft[:, HY_EMB:HY_EMB + 1]
    cb = ft[:, HY_EMB + 1:HY_EMB + 2]
    tt = ft[:, 0:1]
    tap = (cf * _bdot(h, w3f_ref[...]) + cb * _bdot(h, w3b_ref[...])) * jnp.exp(-tt * dl_ref[...])
    tap_ref[...] = tap

    @pl.when(i == 0)
    def _():
        sum_ref[...] = jnp.zeros_like(sum_ref)

    sum_ref[...] += jnp.sum(jnp.abs(tap), axis=0, keepdims=True)


def _hy_taps(feats, w1p, b1, w2, b2, w3, freq, deltas):
    rws = feats.shape[0]
    hid = w2.shape[0]
    w = w3.shape[1] // 2
    tr = _pick(rws, (1024, 512, 256, 128))
    ct = w
    nj = w // ct
    c2 = lambda a: pl.BlockSpec(a.shape, lambda j, i: (0, 0))
    return pl.pallas_call(
        _hy_taps_kernel, name="hy_taps",
        grid=(nj, rws // tr),
        in_specs=[pl.BlockSpec((tr, LANE), lambda j, i: (i, 0)),
                  c2(w1p), pl.BlockSpec((1, hid), lambda j, i: (0, 0)),
                  c2(w2), pl.BlockSpec((1, hid), lambda j, i: (0, 0)),
                  pl.BlockSpec((hid, ct), lambda j, i: (0, j)),
                  pl.BlockSpec((hid, ct), lambda j, i: (0, nj + j)),
                  pl.BlockSpec((1, hid), lambda j, i: (0, 0)),
                  pl.BlockSpec((1, ct), lambda j, i: (0, j))],
        out_specs=[pl.BlockSpec((tr, ct), lambda j, i: (i, j)),
                   pl.BlockSpec((1, ct), lambda j, i: (0, j))],
        out_shape=[jax.ShapeDtypeStruct((rws, w), F32), jax.ShapeDtypeStruct((1, w), F32)],
        compiler_params=_params(("parallel", "arbitrary"), 32 << 20),
    )(feats, w1p, b1.reshape(1, hid), w2, b2.reshape(1, hid), w3, w3, freq.reshape(1, hid),
      deltas.reshape(1, w))


def _dft1_kernel(g_ref, x_ref, o_ref):
    o_ref[...] = lax.dot_general(g_ref[...], x_ref[...].astype(BF16), (((2,), (1,)), ((0,), (0,))),
                                 preferred_element_type=F32).astype(o_ref.dtype)


def _dft1(g, x):
    nb, m2, k = g.shape
    c = x.shape[2]
    bt = _pick(nb, (8, 4, 2))
    ct = _pick(c, (256, 128))
    return pl.pallas_call(
        _dft1_kernel, name="dft1",
        grid=(c // ct, nb // bt),
        in_specs=[pl.BlockSpec((bt, m2, k), lambda j, i: (i, 0, 0)),
                  pl.BlockSpec((bt, k, ct), lambda j, i: (i, 0, j))],
        out_specs=pl.BlockSpec((bt, m2, ct), lambda j, i: (i, 0, j)),
        out_shape=jax.ShapeDtypeStruct((nb, m2, c), BF16),
        compiler_params=_params(("parallel", "arbitrary"), 32 << 20),
    )(g, x)


def _spec_kernel(a_ref, f2_ref, sc_ref, o_ref):
    for l in range(a_ref.shape[0]):
        o_ref[l] = jnp.dot(f2_ref[...], a_ref[l].astype(BF16), preferred_element_type=F32) * sc_ref[...]


def _spec(a, f2, scale):
    na, m2, c = a.shape
    kt = _pick(na, (8, 4, 2))
    ct = _pick(c, (256, 128))
    return pl.pallas_call(
        _spec_kernel, name="hy_spec",
        grid=(c // ct, na // kt),
        in_specs=[pl.BlockSpec((kt, m2, ct), lambda j, i: (i, 0, j)),
                  pl.BlockSpec((m2, m2), lambda j, i: (0, 0)),
                  pl.BlockSpec((1, ct), lambda j, i: (0, j))],
        out_specs=pl.BlockSpec((kt, m2, ct), lambda j, i: (i, 0, j)),
        out_shape=jax.ShapeDtypeStruct((na, m2, c), F32),
        compiler_params=_params(("parallel", "arbitrary"), 32 << 20),
    )(a, f2, scale)


def _conv_mid_kernel(a_ref, f2_ref, h_ref, g3_ref, o_ref):
    nbh = a_ref.shape[1] // 2
    for l in range(a_ref.shape[0]):
        x = jnp.dot(f2_ref[...], a_ref[l].astype(BF16), preferred_element_type=F32)
        xr, xi = x[:nbh], x[nbh:]
        hr, hi = h_ref[l, :nbh], h_ref[l, nbh:]
        y = jnp.concatenate([xr * hr - xi * hi, xr * hi + xi * hr], axis=0)
        o_ref[l] = jnp.dot(g3_ref[l], y.astype(BF16), preferred_element_type=F32).astype(o_ref.dtype)


def _conv_mid(a, f2, h, g3):
    na, m2, c = a.shape
    kt = _pick(na, (8, 4, 2))
    ct = _pick(c, (256, 128))
    slab = pl.BlockSpec((kt, m2, ct), lambda j, i: (i, 0, j))
    return pl.pallas_call(
        _conv_mid_kernel, name="conv_mid",
        grid=(c // ct, na // kt),
        in_specs=[slab, pl.BlockSpec((m2, m2), lambda j, i: (0, 0)), slab,
                  pl.BlockSpec((kt, m2, m2), lambda j, i: (i, 0, 0))],
        out_specs=slab,
        out_shape=jax.ShapeDtypeStruct((na, m2, c), BF16),
        compiler_params=_params(("parallel", "arbitrary"), 32 << 20),
    )(a, f2, h, g3)


def _conv_out_kernel(b_ref, f4_ref, o_ref, *, scale):
    for l in range(b_ref.shape[0]):
        o_ref[l] = jnp.dot(f4_ref[...], b_ref[l].astype(BF16), preferred_element_type=F32) * scale


def _conv_out(bm, f4, scale):
    nb, m2, c = bm.shape
    mo = f4.shape[0]
    pt = _pick(nb, (8, 4, 2))
    ct = _pick(c, (256, 128))
    return pl.pallas_call(
        functools.partial(_conv_out_kernel, scale=scale), name="conv_out",
        grid=(c // ct, nb // pt),
        in_specs=[pl.BlockSpec((pt, m2, ct), lambda j, i: (i, 0, j)),
                  pl.BlockSpec((mo, m2), lambda j, i: (0, 0))],
        out_specs=pl.BlockSpec((pt, mo, ct), lambda j, i: (i, 0, j)),
        out_shape=jax.ShapeDtypeStruct((nb, mo, c), F32),
        compiler_params=_params(("parallel", "arbitrary"), 32 << 20),
    )(bm, f4)


def _ctx_conv_kernel(z_ref, ext_ref, sc_ref, o_ref):
    n = z_ref.shape[1]

    def body(s, acc):
        return acc + ext_ref[pl.ds(n - s, n), :] * z_ref[0, pl.ds(s, 1), :]

    acc = lax.fori_loop(0, n, body, jnp.zeros(o_ref.shape[1:], F32))
    o_ref[0] = acc * sc_ref[...]


def _ctx_conv(z, ext, scale):
    b, n, w = z.shape
    ct = LANE
    return pl.pallas_call(
        _ctx_conv_kernel, name="ctx_conv",
        grid=(b, w // ct),
        in_specs=[pl.BlockSpec((1, n, ct), lambda bi, j: (bi, 0, j)),
                  pl.BlockSpec((2 * n, ct), lambda bi, j: (0, j)),
                  pl.BlockSpec((1, ct), lambda bi, j: (0, j))],
        out_specs=pl.BlockSpec((1, n, ct), lambda bi, j: (bi, 0, j)),
        out_shape=jax.ShapeDtypeStruct((b, n, w), F32),
        compiler_params=_params(("parallel", "arbitrary"), 16 << 20),
    )(z, ext, scale)


def _hy_post_kernel(y_ref, z_ref, x0_ref, bias_ref, o_ref):
    o_ref[0] = (x0_ref[0].astype(F32) * (y_ref[0] + bias_ref[...] * z_ref[0].astype(F32))).astype(o_ref.dtype)


def _hy_post(y, z, x0, bias):
    b, t, w = y.shape
    tm = _pick(t, _TM_ROWS)
    s = pl.BlockSpec((1, tm, w), lambda bi, i: (bi, i, 0))
    return pl.pallas_call(
        _hy_post_kernel, name="hy_post",
        grid=(b, t // tm),
        in_specs=[s, s, s, pl.BlockSpec((1, w), lambda bi, i: (0, 0))],
        out_specs=s,
        out_shape=jax.ShapeDtypeStruct((b, t, w), BF16),
        compiler_params=_params(("parallel", "arbitrary"), 32 << 20),
    )(y, z, x0, bias.reshape(1, w))


def _dft_factors(n):
    m = 2 * n
    na = 1 << ((m.bit_length() - 1) // 2)
    return na, m // na


def _cis(num, den, sign):
    ang = (num % den).astype(F32) * (2.0 * math.pi / den)
    return jnp.cos(ang), sign * jnp.sin(ang)


def _cblock(cr, ci):
    return jnp.concatenate([jnp.concatenate([cr, -ci], axis=-1),
                            jnp.concatenate([ci, cr], axis=-1)], axis=-2)


def _dft_tables(n):
    na, nb = _dft_factors(n)
    m = na * nb
    ah = na // 2
    ar = lambda k: jnp.arange(k, dtype=jnp.int32)
    ka, bb = ar(na)[None, :, None], ar(nb)[:, None, None]
    g1c = _cblock(*_cis(ka * (nb * ar(ah)[None, None, :] + bb), m, -1.0))
    g1t = jnp.concatenate(_cis(ka * (nb * ar(na)[None, None, :] + bb), m, -1.0), axis=-2)
    f2 = _cblock(*_cis(ar(nb)[:, None] * ar(nb)[None, :], nb, -1.0))
    g3 = _cblock(*_cis(ar(nb)[None, :, None] * (ar(na)[:, None, None] + na * ar(nb)[None, None, :]), m, 1.0))
    f4 = _cblock(*_cis(ar(ah)[:, None] * ar(na)[None, :], na, 1.0))
    return tuple(t.astype(BF16) for t in (g1c, g1t, f2, g3, f4))


def _filter_feats(n, order):
    bands_n = (HY_EMB - 1) // 2
    t = jnp.linspace(0.0, 1.0, n, dtype=F32)[:, None]
    lag = jnp.arange(n, dtype=F32)[:, None]
    bands = jnp.linspace(1e-4, bands_n - 1, bands_n, dtype=F32)[None, :]
    ang = 2.0 * math.pi * lag * bands / n
    zf = jnp.concatenate([t, jnp.cos(ang), -jnp.sin(ang)], axis=-1)
    if order == "dft":
        na, nb = _dft_factors(n)
        m = (np.arange(na)[None, :] * nb + np.arange(nb)[:, None]).reshape(-1)
    else:
        m = (np.arange(2 * n) - n) % (2 * n)
    lag_of = np.where(m < n, m, np.where(m == n, 0, 2 * n - m))
    cf = np.where(m == 0, 0.5, np.where(m < n, 1.0, 0.0)).astype(np.float32)
    cb = np.where(m == 0, 0.5, np.where(m > n, 1.0, 0.0)).astype(np.float32)
    pad = jnp.zeros((2 * n, LANE - HY_EMB - 2), F32)
    return jnp.concatenate([zf[lag_of], jnp.asarray(cf)[:, None], jnp.asarray(cb)[:, None], pad], axis=-1)


def _hy_deltas(w):
    return jnp.abs(jnp.linspace(math.log(HY_DECAY_TARGET) / HY_SLOW_PCT,
                                math.log(HY_DECAY_TARGET) / HY_FAST_PCT, w, dtype=F32))


def _long_conv_latent(z, taps_perm, inv_l1):
    b, n, w = z.shape
    na, nb = _dft_factors(n)
    ah = na // 2
    g1c, g1t, f2, g3, f4 = _dft_tables(n)
    h = _spec(_dft1(g1t, taps_perm.reshape(nb, na, w)).reshape(nb, 2, na, w).transpose(2, 1, 0, 3)
              .reshape(na, 2 * nb, w), f2, inv_l1)
    outs = []
    for pair in range(b // 2):
        zz = z[2 * pair:2 * pair + 2].astype(BF16).reshape(2, ah, nb, w).transpose(2, 0, 1, 3).reshape(nb, na, w)
        a = _dft1(g1c, zz).reshape(nb, 2, na, w).transpose(2, 1, 0, 3).reshape(na, 2 * nb, w)
        bm = _conv_mid(a, f2, h, g3).reshape(na, 2, nb, w).transpose(2, 1, 0, 3).reshape(nb, 2 * na, w)
        y = _conv_out(bm, f4, 1.0 / (na * nb))
        outs.append(y.reshape(nb, 2, ah, w).transpose(1, 2, 0, 3).reshape(2, n, w))
    return jnp.concatenate(outs, axis=0) if len(outs) > 1 else outs[0]


def _pack_in_proj(w_in, rw_mu, dims):
    depth, d_model, _ = w_in.shape
    w, dl, al, gl = dims
    ld, la, lg = _rup(dl, LANE), _rup(al, LANE), _rup(gl, LANE)
    rw_cols = 3 * w + 2 * dl + 2 * al + gl
    tile = 1024 if d_model >= 2048 else LANE
    na = _rup(3 * w + 2 * ld + 2 * la + lg, tile)
    pieces = [(0, 3 * w, 0), (3 * w, dl, 3 * w), (3 * w + dl, dl, 3 * w + ld),
              (3 * w + 2 * dl, al, 3 * w + 2 * ld), (3 * w + 2 * dl + al, al, 3 * w + 2 * ld + la),
              (3 * w + 2 * dl + 2 * al, gl, 3 * w + 2 * ld + 2 * la)]
    cols, mus = [], []
    orig = np.full((na,), rw_cols - 1, np.int64)
    pos = 0
    for src, width, dst in pieces + [(rw_cols, 0, na)]:
        if dst > pos:
            cols.append(jnp.zeros((depth, d_model, dst - pos), BF16))
            mus.append(jnp.zeros((depth, dst - pos), F32))
        cols.append(w_in[:, :, src:src + width].astype(BF16))
        mus.append(rw_mu[:, src:src + width])
        orig[dst:dst + width] = np.arange(src, src + width)
        pos = dst + width
    wa = _tile_w(jnp.concatenate(cols, axis=2), tile)
    shift = _rup(rw_cols, tile) - rw_cols
    total = shift + w_in.shape[2]
    wrest = _tile_w(jnp.pad(w_in, ((0, 0), (0, 0), (shift, _rup(total, tile) - total))), tile)
    starts = (rw_cols, rw_cols + 5 * w, rw_cols + 8 * w)
    offs = [(shift + s) // tile for s in starts]
    mu = jnp.concatenate(mus, axis=1).reshape(depth, 1, na)
    grp = np.stack([orig // (rw_cols // 4), orig // (rw_cols // 2)]).astype(np.int32)
    grp = np.minimum(grp, np.array([[3], [1]])).astype(np.int32)
    return wa, wrest, tile, offs, mu, jnp.asarray(grp), (w, ld, la, lg), na


def _pad_rows(a, rows):
    return jnp.pad(a, [(0, 0)] * (a.ndim - 2) + [(0, rows - a.shape[-2]), (0, 0)])


def kernel(x, c, ctx, c_ctx, ada_w, ada_b, norm1_g, norm2_g, w_in, rw_mu, rw_w0, rw_w_up, rw_a0, rw_a_up, rw_g_up, rw_k_k, rw_k_a, rw_r_k, rw_ln_g, rw_ln_b, hg_lower_bounds, hg_norm_g, hy_conv_w, hy_conv_b, hy_f_w1, hy_f_b1, hy_f_w2, hy_f_b2, hy_f_w3, hy_freq, hy_bias, w_branch_a, w_branch_b, w_branch_c, w_out, ffn_w_gate, ffn_w_up, ffn_w_down, final_norm_g):
    bsz, n_lat, d = x.shape
    n_ctx = ctx.shape[1]
    depth = w_in.shape[0]
    w = rw_k_k.shape[1]
    dims = (w, rw_w_up.shape[2], rw_a_up.shape[2], rw_g_up.shape[1])
    assert bsz % 2 == 0 and n_lat % GRID_W == 0 and n_ctx % CHUNK == 0 and w % LANE == 0

    xs = jnp.concatenate([x, ctx], axis=1)
    cvec = jnp.zeros((_rup(bsz + 1, 8), d), F32).at[:bsz].set(c).at[bsz].set(c_ctx)
    lb_cum = jnp.cumsum(jax.nn.softmax(hg_lower_bounds.astype(F32), axis=0), axis=0)
    deltas = _hy_deltas(w)

    wt_rw, wt_in, tile, offs, mu_all, grp, pdims, na = _pack_in_proj(w_in, rw_mu, dims)
    assert (5 * w) % tile == 0 and (3 * w) % tile == 0 and (3 * d) % tile == 0
    _, ld, la, lg = pdims
    hid = ffn_w_gate.shape[2]
    hp = _rup(hid, 1024) if hid > 1024 else _rup(hid, LANE)
    wt_g = _tile_w(_cast_pad(ffn_w_gate, d, hp), _tn_ffn(hp))
    wt_u = _tile_w(_cast_pad(ffn_w_up, d, hp), _tn_ffn(hp))
    wt_d = _tile_w(_cast_pad(ffn_w_down, hp, d), _tn_wide(d), _tk_of(hp))
    wt_o = _tile_w(w_out, _tn_wide(d), _tk_of(d))
    wt_a, wt_b, wt_c = (_tile_w(t, _tn_wide(d)) for t in (w_branch_a, w_branch_b, w_branch_c))

    for layer in range(depth):
        need_ctx = layer < depth - 1
        mod = _ada(cvec, ada_w, ada_b, layer)[:bsz + 1].reshape(bsz + 1, 1, 6, d)
        mods = [mod[:, :, s, :] for s in range(6)]

        h1 = _modnorm(xs, norm1_g[layer], mods[0], mods[1], n_lat)
        pa = _matmul(h1, wt_rw, layer, 0, na // tile, BF16)
        phg = _matmul(h1, wt_in, layer, offs[0], 5 * w // tile, F32)
        rows = None if need_ctx else n_lat
        phy = _matmul(h1, wt_in, layer, offs[1], 3 * w // tile, BF16, rows)
        pgate = _matmul(h1, wt_in, layer, offs[2], 3 * d // tile, BF16, rows)

        r, v, kn, g, bonus, lw, kd, bb = _rw_prep(
            pa, grp, mu_all[layer], rw_w0[layer].reshape(2, 1, w), _pad_rows(rw_w_up[layer], ld),
            rw_a0[layer].reshape(2, 1, w), _pad_rows(rw_a_up[layer], la), _pad_rows(rw_g_up[layer], lg),
            rw_k_k[layer].reshape(1, w), rw_k_a[layer].reshape(1, w), rw_r_k[layer].reshape(1, w),
            n_lat, pdims)
        lb = lb_cum[layer] - lb_cum[0]
        lbt = jnp.stack([jnp.log(lb), jnp.log1p(-lb), 1.0 - lb], axis=1)
        y_f, y_b, o_f, o_b = _scans(r, v, kn, lw, kd, bb, phg, lbt, n_lat)
        oa = _rw_finish(y_f, y_b, bonus, g, rw_ln_g[layer], rw_ln_b[layer])
        ob = _hg_finish(o_f, o_b, phg, hg_norm_g[layer], w)

        x0, z = _hy_pre(phy, hy_conv_w[layer], hy_conv_b[layer], n_lat)
        w1p = _pad_rows(hy_f_w1[layer], LANE)
        filt = (w1p, hy_f_b1[layer], hy_f_w2[layer], hy_f_b2[layer], hy_f_w3[layer], hy_freq[layer], deltas)
        taps, l1 = _hy_taps(_filter_feats(n_lat, "dft"), *filt)
        y_lat = _long_conv_latent(z[:, :n_lat], taps, 1.0 / l1)
        if need_ctx:
            ext, l1c = _hy_taps(_filter_feats(n_ctx, "lag"), *filt)
            y_ctx = _ctx_conv(z[:, n_lat:].astype(F32), ext, 1.0 / l1c)
            y_lat = jnp.concatenate([y_lat, y_ctx], axis=1)
        oc = _hy_post(y_lat, z, x0, hy_bias[layer])

        ym = _merge(oa, ob, oc, wt_a, wt_b, wt_c, layer, pgate)
        xs = _matmul_resid(ym, wt_o, layer, xs, mods[2], n_lat)

        h2 = _modnorm(xs, norm2_g[layer], mods[3], mods[4], n_lat)
        act = _swiglu_up(h2, wt_g, wt_u, layer)
        xs = _matmul_resid(act, wt_d, layer, xs, mods[5], n_lat)

    return _final_norm(xs, final_norm_g, n_lat)
```

```python
import functools
import math

import numpy as np
import jax
import jax.numpy as jnp
from jax import lax
from jax.experimental import pallas as pl
from jax.experimental.pallas import tpu as pltpu

F32 = jnp.float32
BF16 = jnp.bfloat16

GRID_W = 64
CHUNK = 64
SUB = 8
NORM_EPS = 1e-6
RW_HEAD = 64
RW_GN_EPS = 64e-5
HY_EMB = 33
HY_DECAY_TARGET = 1e-2
HY_FAST_PCT = 0.3
HY_SLOW_PCT = 1.5
LANE = 128
VMEM_CAP = 56 * 1024 * 1024
VMEM_SLACK = 8 * 1024 * 1024


def _params(sem, vmem_bytes):
    return pltpu.CompilerParams(dimension_semantics=sem,
                                vmem_limit_bytes=int(min(max(vmem_bytes + VMEM_SLACK, 16 << 20), VMEM_CAP)))


def _pick(n, cands):
    for c in cands:
        if n % c == 0:
            return c
    return n


def _rup(n, m):
    return -(-n // m) * m


def _bdot(a, b):
    return jnp.dot(a.astype(BF16), b.astype(BF16), preferred_element_type=F32)


def _split(x):
    hi = x.astype(BF16)
    lo = (x - hi.astype(F32)).astype(BF16)
    return hi, lo


def _mask_dot(m, x):
    hi, lo = _split(x)
    return (jnp.dot(m, hi, preferred_element_type=F32) + jnp.dot(m, lo, preferred_element_type=F32))


def _x_mask_dot(x, m):
    hi, lo = _split(x)
    return (jnp.dot(hi, m, preferred_element_type=F32) + jnp.dot(lo, m, preferred_element_type=F32))


def _sigmoid(x):
    return 1.0 / (1.0 + jnp.exp(-x))


def _order_masks(d, n, reps=1):
    row = lax.broadcasted_iota(jnp.int32, (n, reps * n), 0)
    col = lax.broadcasted_iota(jnp.int32, (n, reps * n), 1) % n
    diff = (row - col) * (1 - 2 * d)
    return diff >= 0, diff > 0


def _ada_kernel(c_ref, w_ref, b_ref, o_ref):
    c = c_ref[...]
    o_ref[...] = _bdot(c * _sigmoid(c), w_ref[...]) + b_ref[...]


def _ada(cvec, w, b, layer):
    rows, d = cvec.shape
    n = w.shape[2]
    tn = _pick(n, (1024, 512, 256, 128))
    return pl.pallas_call(
        _ada_kernel, name="ada",
        grid=(n // tn,),
        in_specs=[pl.BlockSpec((rows, d), lambda j: (0, 0)),
                  pl.BlockSpec((None, d, tn), lambda j: (layer, 0, j)),
                  pl.BlockSpec((None, 1, tn), lambda j: (layer, 0, j))],
        out_specs=pl.BlockSpec((rows, tn), lambda j: (0, j)),
        out_shape=jax.ShapeDtypeStruct((rows, n), F32),
        compiler_params=_params(("arbitrary",), 3 * d * tn * 4),
    )(cvec, w, b.reshape(b.shape[0], 1, n))


def _modnorm_kernel(x_ref, g_ref, shl_ref, scl_ref, shc_ref, scc_ref, o_ref, *, n_lat):
    tm = x_ref.shape[1]
    x = x_ref[0]
    y = x * lax.rsqrt(jnp.mean(x * x, axis=-1, keepdims=True) + NORM_EPS)
    pos = pl.program_id(1) * tm + lax.broadcasted_iota(jnp.int32, (tm, 1), 0)
    is_ctx = pos >= n_lat
    g = g_ref[...]
    mul = jnp.where(is_ctx, g * (1.0 + scc_ref[0]), g * (1.0 + scl_ref[0]))
    sh = jnp.where(is_ctx, shc_ref[0], shl_ref[0])
    o_ref[0] = (y * mul + sh).astype(o_ref.dtype)


def _modnorm(x, gain, shift, scale, n_lat):
    b, t, d = x.shape
    tm = _pick(t, (512, 384, 256, 128, 64))
    vec = lambda f: pl.BlockSpec((1, 1, d), f)
    return pl.pallas_call(
        functools.partial(_modnorm_kernel, n_lat=n_lat), name="modnorm",
        grid=(b, t // tm),
        in_specs=[pl.BlockSpec((1, tm, d), lambda bi, i: (bi, i, 0)),
                  pl.BlockSpec((1, d), lambda bi, i: (0, 0)),
                  vec(lambda bi, i: (bi, 0, 0)), vec(lambda bi, i: (bi, 0, 0)),
                  vec(lambda bi, i: (b, 0, 0)), vec(lambda bi, i: (b, 0, 0))],
        out_specs=pl.BlockSpec((1, tm, d), lambda bi, i: (bi, i, 0)),
        out_shape=jax.ShapeDtypeStruct((b, t, d), BF16),
        compiler_params=_params(("parallel", "arbitrary"), 6 * tm * d * 4),
    )(x, gain.reshape(1, d), shift, scale, shift, scale)


def _rmsnorm_kernel(x_ref, g_ref, o_ref):
    x = x_ref[0]
    o_ref[0] = x * lax.rsqrt(jnp.mean(x * x, axis=-1, keepdims=True) + NORM_EPS) * g_ref[...]


def _final_norm(x, gain, n_lat):
    b, _, d = x.shape
    tm = _pick(n_lat, (512, 384, 256, 128, 64))
    return pl.pallas_call(
        _rmsnorm_kernel, name="final_norm",
        grid=(b, n_lat // tm),
        in_specs=[pl.BlockSpec((1, tm, d), lambda bi, i: (bi, i, 0)),
                  pl.BlockSpec((1, d), lambda bi, i: (0, 0))],
        out_specs=pl.BlockSpec((1, tm, d), lambda bi, i: (bi, i, 0)),
        out_shape=jax.ShapeDtypeStruct((b, n_lat, d), F32),
        compiler_params=_params(("parallel", "arbitrary"), 6 * tm * d * 4),
    )(x, gain.reshape(1, d))


def _mm_kernel(x_ref, w_ref, o_ref):
    o_ref[0] = jnp.dot(x_ref[0], w_ref[...], preferred_element_type=F32).astype(o_ref.dtype)


def _tn_wide(n):
    return _pick(n, (1024, 512, 256, 128))


def _tn_ffn(n):
    return _pick(n, (512, 256, 128))


def _tk_of(k):
    return k if k <= 2048 else _pick(k, (2816, 2048, 1024, 512, 256, 128))


class _TiledW:
    def __init__(self, w, tn, tk=None):
        self.w = w.astype(BF16)
        self.tk = w.shape[1] if tk is None else tk
        self.tn = tn
        self.n = w.shape[2]


def _tile_w(w, tn, tk=None):
    return _TiledW(w, tn, tk)


def _cast_pad_kernel(x_ref, o_ref, *, n_in, cols):
    i = pl.program_id(1)

    @pl.when(i < n_in)
    def _():
        o_ref[0, :, :cols] = x_ref[0].astype(o_ref.dtype)
        if cols < o_ref.shape[2]:
            o_ref[0, :, cols:] = jnp.zeros((o_ref.shape[1], o_ref.shape[2] - cols), o_ref.dtype)

    @pl.when(i >= n_in)
    def _():
        o_ref[0] = jnp.zeros(o_ref.shape[1:], o_ref.dtype)


def _cast_pad(w, rows_out, cols_out):
    l, r, c = w.shape
    tr = _pick(math.gcd(r, rows_out), (256, 128, 64, 32, 16))
    n_in = r // tr
    return pl.pallas_call(
        functools.partial(_cast_pad_kernel, n_in=n_in, cols=c), name="cast_pad",
        grid=(l, rows_out // tr),
        in_specs=[pl.BlockSpec((1, tr, c), lambda li, i: (li, jnp.minimum(i, n_in - 1), 0))],
        out_specs=pl.BlockSpec((1, tr, cols_out), lambda li, i: (li, i, 0)),
        out_shape=jax.ShapeDtypeStruct((l, rows_out, cols_out), BF16),
        compiler_params=_params(("parallel", "arbitrary"), 2 * tr * (c * 4 + cols_out * 2)),
    )(w)


def _wspec(tk, tn, f):
    def idx(*a):
        lay, j, l = f(*a)[:3]
        return lay, l, j
    return pl.BlockSpec((None, tk, tn), idx)


_TM_ROWS = (1024, 768, 512, 384, 256, 128, 64)


def _matmul(x, wt, layer, j0, nj, out_dtype, rows=None):
    b, _, k = x.shape
    t = x.shape[1] if rows is None else rows
    tn = wt.tn
    n = nj * tn
    tm = _pick(t, _TM_ROWS)
    osz = jnp.dtype(out_dtype).itemsize
    return pl.pallas_call(
        _mm_kernel, name="in_proj",
        grid=(b, t // tm, n // tn),
        in_specs=[pl.BlockSpec((1, tm, k), lambda bi, i, j: (bi, i, 0)),
                  _wspec(k, tn, lambda bi, i, j: (layer, j0 + j, 0, 0, 0))],
        out_specs=pl.BlockSpec((1, tm, tn), lambda bi, i, j: (bi, i, j)),
        out_shape=jax.ShapeDtypeStruct((b, t, n), out_dtype),
        compiler_params=_params(("parallel", "parallel", "arbitrary"),
                                2 * (tm * k * 2 + k * tn * 2 + tm * tn * osz) + tm * tn * 4),
    )(x, wt.w)


def _swiglu_kernel(x_ref, wg_ref, wu_ref, o_ref):
    x = x_ref[0]
    g = jnp.dot(x, wg_ref[...], preferred_element_type=F32)
    u = jnp.dot(x, wu_ref[...], preferred_element_type=F32)
    o_ref[0] = (g * _sigmoid(g) * u).astype(o_ref.dtype)


def _swiglu_up(x, wg, wu, layer):
    b, t, k = x.shape
    tn = wg.tn
    n = wg.n
    tm = _pick(t, _TM_ROWS)
    wsp = lambda: _wspec(k, tn, lambda bi, i, j: (layer, j, 0, 0, 0))
    return pl.pallas_call(
        _swiglu_kernel, name="swiglu_up",
        grid=(b, t // tm, n // tn),
        in_specs=[pl.BlockSpec((1, tm, k), lambda bi, i, j: (bi, i, 0)), wsp(), wsp()],
        out_specs=pl.BlockSpec((1, tm, tn), lambda bi, i, j: (bi, i, j)),
        out_shape=jax.ShapeDtypeStruct((b, t, n), BF16),
        compiler_params=_params(("parallel", "parallel", "arbitrary"),
                                2 * (tm * k * 2 + 2 * k * tn * 2 + tm * tn * 2) + 3 * tm * tn * 4),
    )(x, wg.w, wu.w)


def _resid_kernel(x_ref, w_ref, r_ref, gl_ref, gc_ref, o_ref, acc_ref, *, n_lat):
    kk = pl.program_id(3)

    @pl.when(kk == 0)
    def _():
        acc_ref[...] = jnp.zeros_like(acc_ref)

    acc_ref[...] += jnp.dot(x_ref[0], w_ref[...], preferred_element_type=F32)

    @pl.when(kk == pl.num_programs(3) - 1)
    def _():
        tm = acc_ref.shape[0]
        pos = pl.program_id(1) * tm + lax.broadcasted_iota(jnp.int32, (tm, 1), 0)
        gate = jnp.where(pos >= n_lat, gc_ref[0], gl_ref[0])
        o_ref[0] = r_ref[0] + gate * acc_ref[...]


def _matmul_resid(x, w, layer, res, gate, n_lat):
    b, t, k = x.shape
    tk, tn = w.tk, w.tn
    n = w.n
    tm = _pick(t, _TM_ROWS)
    return pl.pallas_call(
        functools.partial(_resid_kernel, n_lat=n_lat), name="proj_resid",
        grid=(b, t // tm, n // tn, k // tk),
        in_specs=[pl.BlockSpec((1, tm, tk), lambda bi, i, j, l: (bi, i, l)),
                  _wspec(tk, tn, lambda bi, i, j, l: (layer, j, l, 0, 0)),
                  pl.BlockSpec((1, tm, tn), lambda bi, i, j, l: (bi, i, j)),
                  pl.BlockSpec((1, 1, tn), lambda bi, i, j, l: (bi, 0, j)),
                  pl.BlockSpec((1, 1, tn), lambda bi, i, j, l: (b, 0, j))],
        out_specs=pl.BlockSpec((1, tm, tn), lambda bi, i, j, l: (bi, i, j)),
        out_shape=jax.ShapeDtypeStruct((b, t, n), F32),
        scratch_shapes=[pltpu.VMEM((tm, tn), F32)],
        compiler_params=_params(("parallel", "parallel", "arbitrary", "arbitrary"),
                                2 * (tm * tk * 2 + tk * tn * 2 + 2 * tm * tn * 4) + 2 * tm * tn * 4),
    )(x, w.w, res, gate, gate)


def _merge_kernel(oa_ref, ob_ref, oc_ref, wa_ref, wb_ref, wc_ref, ga_ref, gb_ref, gc_ref, o_ref):
    y = _sigmoid(ga_ref[0].astype(F32)) * jnp.dot(oa_ref[0], wa_ref[...], preferred_element_type=F32)
    y += _sigmoid(gb_ref[0].astype(F32)) * jnp.dot(ob_ref[0], wb_ref[...], preferred_element_type=F32)
    y += _sigmoid(gc_ref[0].astype(F32)) * jnp.dot(oc_ref[0], wc_ref[...], preferred_element_type=F32)
    o_ref[0] = y.astype(o_ref.dtype)


def _merge(oa, ob, oc, wa, wb, wc, layer, pgate):
    b, _, kw = oa.shape
    t = pgate.shape[1]
    tn = wa.tn
    nj = wa.n // tn
    d = nj * tn
    tm = _pick(t, (768, 512, 384, 256, 128, 64))
    br = lambda: pl.BlockSpec((1, tm, kw), lambda bi, i, j: (bi, i, 0))
    wt = lambda: _wspec(kw, tn, lambda bi, i, j: (layer, j, 0, 0, 0))
    gt = lambda s: pl.BlockSpec((1, tm, tn), lambda bi, i, j: (bi, i, s * nj + j))
    gsz = jnp.dtype(pgate.dtype).itemsize
    return pl.pallas_call(
        _merge_kernel, name="merge",
        grid=(b, t // tm, nj),
        in_specs=[br(), br(), br(), wt(), wt(), wt(), gt(0), gt(1), gt(2)],
        out_specs=pl.BlockSpec((1, tm, tn), lambda bi, i, j: (bi, i, j)),
        out_shape=jax.ShapeDtypeStruct((b, t, d), BF16),
        compiler_params=_params(("parallel", "parallel", "arbitrary"),
                                2 * (3 * tm * kw * 2 + 3 * kw * tn * 2 + 3 * tm * tn * gsz + tm * tn * 2)
                                + 4 * tm * tn * 4),
    )(oa, ob, oc, wa.w, wb.w, wc.w, pgate, pgate, pgate)


def _pair_sum_matrix():
    r = lax.broadcasted_iota(jnp.int32, (LANE, LANE), 0) // RW_HEAD
    c = lax.broadcasted_iota(jnp.int32, (LANE, LANE), 1) // RW_HEAD
    return jnp.where(r == c, 1.0, 0.0).astype(BF16)


def _head_sums(x, hs):
    rows, w = x.shape
    nt = w // LANE
    stacked = jnp.concatenate([x[:, j * LANE:(j + 1) * LANE] for j in range(nt)], axis=0)
    s = _x_mask_dot(stacked, hs)
    return jnp.concatenate([s[j * rows:(j + 1) * rows] for j in range(nt)], axis=1)


def _rw_prep_kernel(prev_ref, cur_ref, next_ref, grp_ref, mu_ref, w0_ref, wup_ref, a0_ref, aup_ref,
                    gup_ref, kk_ref, ka_ref, rk_ref,
                    r_ref, v_ref, kn_ref, g_ref, bon_ref, lw_ref, kd_ref, bb_ref,
                    *, n_lat_chunks, n_chunks, w, ld, la, lg):
    i = pl.program_id(1)
    c = CHUNK
    cur = cur_ref[0].astype(F32)
    prev = prev_ref[0].astype(F32)
    nxt = next_ref[0].astype(F32)
    is_ctx = i >= n_lat_chunks
    first = jnp.logical_or(i == 0, i == n_lat_chunks)
    last = jnp.logical_or(i == n_lat_chunks - 1, i == n_chunks - 1)
    row = lax.broadcasted_iota(jnp.int32, (c, 1), 0)
    carry_in = jnp.where(jnp.logical_and(is_ctx, jnp.logical_not(first)), 1.0, 0.0)
    carry_out = jnp.where(jnp.logical_and(is_ctx, jnp.logical_not(last)), 1.0, 0.0)
    tm1 = jnp.where(row == 0, prev[c - 1:c] * carry_in, pltpu.roll(cur, 1, 0))
    tp1 = jnp.where(row == c - 1, nxt[0:1] * carry_out, pltpu.roll(cur, c - 1, 0))
    lat_up = jnp.where(jnp.logical_or(is_ctx, first), 0.0, 1.0)
    lat_dn = jnp.where(jnp.logical_or(is_ctx, last), 0.0, 1.0)
    code = jnp.where(is_ctx, grp_ref[1:2], grp_ref[0:1])
    shifted = jnp.where(code == 0, tm1,
                        jnp.where(code == 1, tp1,
                                  jnp.where(code == 2, prev * lat_up, nxt * lat_dn)))
    m = cur + mu_ref[...] * (shifted - cur)

    r = m[:, 0:w]
    k = m[:, w:2 * w]
    v = m[:, 2 * w:3 * w]
    o = 3 * w
    wd = (m[:, o:o + ld], m[:, o + ld:o + 2 * ld])
    ad = (m[:, o + 2 * ld:o + 2 * ld + la], m[:, o + 2 * ld + la:o + 2 * ld + 2 * la])
    gd = m[:, o + 2 * ld + 2 * la:o + 2 * ld + 2 * la + lg]

    hs = _pair_sum_matrix()
    g_ref[0] = _bdot(_sigmoid(gd), gup_ref[...]).astype(g_ref.dtype)
    kx = k * kk_ref[...]
    kn = kx * lax.rsqrt(jnp.maximum(_head_sums(kx * kx, hs), 1e-24))
    r_ref[0] = r.astype(r_ref.dtype)
    v_ref[0] = v.astype(v_ref.dtype)
    kn_ref[0] = kn.astype(kn_ref.dtype)
    bonus = jnp.zeros_like(r)
    for d in range(2):
        wl = w0_ref[d] + _bdot(jnp.tanh(wd[d]), wup_ref[d])
        lw_ref[d, 0] = -math.exp(-0.5) * _sigmoid(wl)
        a = _sigmoid(a0_ref[d] + _bdot(ad[d], aup_ref[d]))
        kd = k * (1.0 + (a - 1.0) * ka_ref[...])
        kd_ref[d, 0] = kd.astype(kd_ref.dtype)
        bb_ref[d, 0] = (kn * a).astype(bb_ref.dtype)
        bonus += r * kd * rk_ref[...]
    bon_ref[0] = (_head_sums(bonus, hs) * v).astype(bon_ref.dtype)


def _rw_prep(pa, grp, mu, w0, wup, a0, aup, gup, k_k, k_a, r_k, n_lat, dims):
    b, t, na = pa.shape
    w, ld, la, lg = dims
    nc = t // CHUNK
    nlc = n_lat // CHUNK
    blk = lambda f: pl.BlockSpec((1, CHUNK, na), f)
    full = lambda a: pl.BlockSpec(a.shape, lambda bi, i: (0,) * a.ndim)
    o1 = pl.BlockSpec((1, CHUNK, w), lambda bi, i: (bi, i, 0))
    o2 = pl.BlockSpec((2, 1, CHUNK, w), lambda bi, i: (0, bi, i, 0))
    s1 = jax.ShapeDtypeStruct((b, t, w), BF16)
    s2 = jax.ShapeDtypeStruct((2, b, t, w), BF16)
    s2f = jax.ShapeDtypeStruct((2, b, t, w), F32)
    consts = (grp, mu, w0, wup, a0, aup, gup, k_k, k_a, r_k)
    return pl.pallas_call(
        functools.partial(_rw_prep_kernel, n_lat_chunks=nlc, n_chunks=nc, w=w, ld=ld, la=la, lg=lg),
        name="rw_prep",
        grid=(b, nc),
        in_specs=[blk(lambda bi, i: (bi, jnp.maximum(i - 1, 0), 0)),
                  blk(lambda bi, i: (bi, i, 0)),
                  blk(lambda bi, i: (bi, jnp.minimum(i + 1, nc - 1), 0))] + [full(a) for a in consts],
        out_specs=[o1, o1, o1, o1, o1, o2, o2, o2],
        out_shape=[s1, s1, s1, s1, s1, s2f, s2, s2],
        compiler_params=_params(("parallel", "arbitrary"), 32 << 20),
    )(pa, pa, pa, *consts)


def _pdot(a, b):
    return lax.dot_general(a.astype(BF16), b.astype(BF16), (((2,), (1,)), ((0,), (0,))),
                           preferred_element_type=F32)


def _pdot_nt(a, b):
    return lax.dot_general(a.astype(BF16), b.astype(BF16), (((2,), (2,)), ((0,), (0,))),
                           preferred_element_type=F32)


def _pdot_tn(a, b):
    return lax.dot_general(a.astype(BF16), b.astype(BF16), (((1,), (1,)), ((0,), (0,))),
                           preferred_element_type=F32)


def _rw_scan_dir(d, r_ref, v_ref, kn_ref, lw_ref, kd_ref, bb_ref, y_ref, s_ref, n_pairs):
    c = CHUNK
    wfull = n_pairs * LANE
    incl_bf = jnp.where(_order_masks(d, c)[0], 1.0, 0.0).astype(BF16)
    incl2, strict2 = _order_masks(d, c, 2)
    lane_w = lax.broadcasted_iota(jnp.int32, (1, wfull), 1) % LANE
    w0 = jnp.where(lane_w < RW_HEAD, 1.0, 0.0)
    w1 = 1.0 - w0
    lane = lax.broadcasted_iota(jnp.int32, (1, 1, LANE), 2)
    m0 = jnp.where(lane < RW_HEAD, 1.0, 0.0)
    m1 = 1.0 - m0
    rowh = lax.broadcasted_iota(jnp.int32, (LANE, LANE), 0) // RW_HEAD
    colh = lax.broadcasted_iota(jnp.int32, (LANE, LANE), 1) // RW_HEAD
    blockdiag = rowh == colh
    last_row = jnp.where(d == 0, c - 1, 0)
    rsel = lax.broadcasted_iota(jnp.int32, (c, 1), 0) == last_row

    def pairs(x):
        return jnp.stack([x[:, p * LANE:(p + 1) * LANE] for p in range(n_pairs)], axis=0)

    def stack(x):
        return jnp.concatenate([x * m0, x * m1], axis=1)

    lw = lw_ref[0, 0]
    r = r_ref[0].astype(F32)
    v = v_ref[0].astype(F32)
    kd = kd_ref[0, 0].astype(F32)
    bb = bb_ref[0, 0].astype(F32)
    cw = _mask_dot(incl_bf, lw)
    tot = jnp.sum(jnp.where(rsel, cw, 0.0), axis=0, keepdims=True)
    w_inv = jnp.exp(-cw)
    w_rem = jnp.exp(tot - cw)
    a_t = -kn_ref[0].astype(F32) * jnp.exp(cw - lw)
    r_t = r * jnp.exp(cw)
    b_t = bb * w_inv
    k_t = kd * w_inv
    ar = pairs(jnp.concatenate([a_t, r_t], axis=0))
    bk = pairs(jnp.concatenate([b_t * w0, b_t * w1, k_t * w0, k_t * w1], axis=0))
    vst = pairs(jnp.concatenate([v * w0, v * w1], axis=0))
    uvr = pairs(jnp.concatenate([bb * w_rem, kd * w_rem], axis=0))
    v3 = pairs(v)
    decay = pairs(jnp.exp(tot))
    s0 = s_ref[...]
    yield

    sc = _pdot_nt(ar, bk)
    a_ab = jnp.where(strict2, sc[:, :c, :2 * c], 0.0)
    a_ak = jnp.where(strict2, sc[:, :c, 2 * c:], 0.0)
    r_b = jnp.where(incl2, sc[:, c:, :2 * c], 0.0)
    r_k = jnp.where(incl2, sc[:, c:, 2 * c:], 0.0)

    abd = stack(a_ab)
    yield
    q = _pdot(abd, abd)
    nsum = abd
    for _ in range(4):
        yield
        both = _pdot(q, jnp.concatenate([nsum, q], axis=2))
        nsum = nsum + q + both[:, :, :LANE]
        q = both[:, :, LANE:]
    yield
    nsum = nsum + q + _pdot(q, nsum)
    n_side = nsum[:, :c] + nsum[:, c:]

    yield
    art = _pdot_nt(ar, s0)
    rhs = art[:, :c] + _pdot(a_ak, vst)
    yield
    u = rhs + _pdot(n_side, stack(rhs))
    yield
    y = art[:, c:] + _pdot(jnp.concatenate([r_b, r_k], axis=2), jnp.concatenate([stack(u), vst], axis=1))
    for p in range(n_pairs):
        y_ref[0, :, p * LANE:(p + 1) * LANE] = y[p]
    yield
    upd = _pdot_tn(jnp.concatenate([u, v3], axis=1), uvr)
    s_ref[...] = s0 * decay + jnp.where(blockdiag, upd, 0.0)


def _scan_chunk(d, i, n_lat_chunks, n_chunks):
    nctx = n_chunks - n_lat_chunks
    in_ctx = i < nctx
    fwd = jnp.where(in_ctx, n_lat_chunks + i, i - nctx)
    bwd = jnp.where(in_ctx, n_chunks - 1 - i, n_chunks - 1 - i)
    return jnp.where(d == 0, fwd, bwd)


def _rw_finish_kernel(yf_ref, yb_ref, bon_ref, g_ref, lg_ref, lb_ref, o_ref):
    y = yf_ref[0] + yb_ref[0]
    hs = _pair_sum_matrix()
    mean = _head_sums(y, hs) * (1.0 / RW_HEAD)
    yc = y - mean
    var = _head_sums(yc * yc, hs) * (1.0 / RW_HEAD)
    yn = yc * lax.rsqrt(var + RW_GN_EPS)
    o_ref[0] = ((yn * lg_ref[...] + lb_ref[...] + bon_ref[0].astype(F32)) * g_ref[0].astype(F32)).astype(o_ref.dtype)


def _rw_finish(y_f, y_b, bonus, g, ln_g, ln_b):
    b, t, w = y_f.shape
    tm = _pick(t, _TM_ROWS)
    s1 = pl.BlockSpec((1, tm, w), lambda bi, i: (bi, i, 0))
    vec = pl.BlockSpec((1, w), lambda bi, i: (0, 0))
    return pl.pallas_call(
        _rw_finish_kernel, name="rw_finish",
        grid=(b, t // tm),
        in_specs=[s1, s1, s1, s1, vec, vec],
        out_specs=s1,
        out_shape=jax.ShapeDtypeStruct((b, t, w), BF16),
        compiler_params=_params(("parallel", "arbitrary"), 32 << 20),
    )(y_f, y_b, bonus, g, ln_g.reshape(1, w), ln_b.reshape(1, w))


def _hg_scan_dir(d, q_ref, z_ref, i_ref, lb_ref, o_ref, s_ref, n_heads):
    c = CHUNK
    nb = c // SUB
    sgn = 1 - 2 * d
    fwd = d == 0
    incl_bf = jnp.where(_order_masks(d, c)[0], 1.0, 0.0).astype(BF16)
    row = lax.broadcasted_iota(jnp.int32, (c, c), 0)
    col = lax.broadcasted_iota(jnp.int32, (c, c), 1)
    rb, cbk = row // SUB, col // SUB
    lvl_b = (rb - cbk) * sgn > 0
    lvl_1 = jnp.logical_and(rb == cbk, (row - col) * sgn >= 0)
    last_row = jnp.where(d == 0, c - 1, 0)
    rsel = lax.broadcasted_iota(jnp.int32, (c, 1), 0) == last_row

    def heads(x):
        return jnp.stack([x[:, h * LANE:(h + 1) * LANE] for h in range(n_heads)], axis=0)

    q = q_ref[0]
    z = z_ref[0]
    val = i_ref[0]
    lg_l = lb_ref[d, 0:1]
    lg_1ml = lb_ref[d, 1:2]
    one_ml = lb_ref[d, 2:3]
    s0 = s_ref[...]

    ez = jnp.exp(-jnp.abs(z))
    log_sig = jnp.minimum(z, 0.0) - jnp.log(1.0 + ez)
    x2 = lg_1ml + log_sig
    mx = jnp.maximum(lg_l, x2)
    log_f = mx + jnp.log(jnp.exp(lg_l - mx) + jnp.exp(x2 - mx))
    k = one_ml * jnp.where(z >= 0.0, ez, 1.0) / (1.0 + ez)
    yield

    cb = _mask_dot(incl_bf, log_f)
    tot = jnp.sum(jnp.where(rsel, cb, 0.0), axis=0, keepdims=True)
    wfull = cb.shape[1]
    cb3 = cb.reshape(nb, SUB, wfull)
    blk_end = cb3[:, SUB - 1:SUB] if fwd else cb3[:, 0:1]

    end_own = jnp.broadcast_to(blk_end, (nb, SUB, wfull)).reshape(c, wfull)
    refs = [(blk_end[j], slice(0, c)) for j in range(nb)]
    refs += [(jnp.broadcast_to(cb3[:, j:j + 1], (nb, SUB, wfull)).reshape(c, wfull), slice(c, 2 * c))
             for j in range(SUB)]
    which = jnp.where(lvl_b, cbk, jnp.where(lvl_1, nb + col % SUB, -1))
    ks = heads(jnp.concatenate([k * jnp.exp(end_own - cb), k], axis=0))
    q_bf = q.astype(BF16)
    att = jnp.zeros((n_heads, c, c), F32)
    group = 4
    for g0 in range(0, len(refs), group):
        yield
        part = refs[g0:g0 + group]
        qs = heads(jnp.concatenate([q_bf * jnp.exp(jnp.minimum(cb - e, 0.0)).astype(BF16) for e, _ in part],
                                   axis=0))
        sc = _pdot_nt(qs, ks)
        for n, (_, cols) in enumerate(part):
            att = jnp.where(which == g0 + n, sc[:, n * c:(n + 1) * c, cols], att)
    yield
    v3 = heads(val)
    o = _pdot(att, v3) + _pdot_nt(heads(q * jnp.exp(cb)), s0)
    for h in range(n_heads):
        o_ref[0, :, h * LANE:(h + 1) * LANE] = o[h]
    yield
    s_ref[...] = s0 * heads(jnp.exp(tot)) + _pdot_tn(v3, heads(k * jnp.exp(tot - cb)))


def _scans_kernel(*refs, n_pairs, n_heads):
    rw_in, hg_in = refs[0:12], refs[12:19]
    y_f, y_b, o_f, o_b, s_rw, s_hg = refs[19:25]

    @pl.when(pl.program_id(1) == 0)
    def _():
        s_rw[...] = jnp.zeros_like(s_rw)
        s_hg[...] = jnp.zeros_like(s_hg)

    live = [_hg_scan_dir(0, *hg_in[0:3], hg_in[6], o_f, s_hg.at[0], n_heads),
            _rw_scan_dir(0, *rw_in[0:6], y_f, s_rw.at[0], n_pairs),
            _hg_scan_dir(1, *hg_in[3:6], hg_in[6], o_b, s_hg.at[1], n_heads),
            _rw_scan_dir(1, *rw_in[6:12], y_b, s_rw.at[1], n_pairs)]
    while live:
        for gen in list(live):
            if next(gen, StopIteration) is StopIteration:
                live.remove(gen)


def _scans(r, v, kn, lw, kd, bb, phg, lbt, n_lat):
    b, t, w = r.shape
    nc = t // CHUNK
    nlc = n_lat // CHUNK
    ch = functools.partial(_scan_chunk, n_lat_chunks=nlc, n_chunks=nc)
    col = lambda d, cb_: pl.BlockSpec((1, CHUNK, w), lambda bi, i: (bi, ch(d, i), cb_))
    s2 = lambda d: pl.BlockSpec((1, 1, CHUNK, w), lambda bi, i: (d, bi, ch(d, i), 0))
    rw_dir = lambda d: [col(d, 0), col(d, 0), col(d, 0), s2(d), s2(d), s2(d)]
    out = jax.ShapeDtypeStruct((b, t, w), F32)
    return pl.pallas_call(
        functools.partial(_scans_kernel, n_pairs=w // LANE, n_heads=w // LANE), name="scans",
        grid=(b, nc),
        in_specs=rw_dir(0) + rw_dir(1)
        + [col(0, 0), col(0, 1), col(0, 3), col(1, 0), col(1, 2), col(1, 3),
           pl.BlockSpec((2, 3, w), lambda bi, i: (0, 0, 0))],
        out_specs=[col(0, 0), col(1, 0), col(0, 0), col(1, 0)],
        out_shape=[out, out, out, out],
        scratch_shapes=[pltpu.VMEM((2, w // LANE, LANE, LANE), F32), pltpu.VMEM((2, w // LANE, LANE, LANE), F32)],
        compiler_params=_params(("arbitrary", "arbitrary"), 40 << 20),
    )(r, v, kn, lw, kd, bb, r, v, kn, lw, kd, bb, phg, phg, phg, phg, phg, phg, lbt)


def _hg_finish_kernel(of_ref, ob_ref, g_ref, ng_ref, out_ref, *, n_heads):
    for h in range(n_heads):
        sl = slice(h * LANE, (h + 1) * LANE)
        o = of_ref[0, :, sl] + ob_ref[0, :, sl]
        y = o * lax.rsqrt(jnp.mean(o * o, axis=-1, keepdims=True) + NORM_EPS) * ng_ref[...]
        g = g_ref[0, :, sl]
        out_ref[0, :, sl] = (y * g * _sigmoid(g)).astype(out_ref.dtype)


def _hg_finish(o_f, o_b, phg, norm_g, w):
    b, t, _ = o_f.shape
    tm = _pick(t, _TM_ROWS)
    return pl.pallas_call(
        functools.partial(_hg_finish_kernel, n_heads=w // LANE), name="hg_finish",
        grid=(b, t // tm),
        in_specs=[pl.BlockSpec((1, tm, w), lambda bi, i: (bi, i, 0)),
                  pl.BlockSpec((1, tm, w), lambda bi, i: (bi, i, 0)),
                  pl.BlockSpec((1, tm, w), lambda bi, i: (bi, i, 4)),
                  pl.BlockSpec((1, LANE), lambda bi, i: (0, 0))],
        out_specs=pl.BlockSpec((1, tm, w), lambda bi, i: (bi, i, 0)),
        out_shape=jax.ShapeDtypeStruct((b, t, w), BF16),
        compiler_params=_params(("parallel", "arbitrary"), 32 << 20),
    )(o_f, o_b, phg, norm_g.reshape(1, LANE))


def _hy_pre_kernel(prev_ref, cur_ref, next_ref, cw_ref, cb_ref, x0_ref, z_ref, *, n_lat_blocks, n_blocks, w):
    i = pl.program_id(1)
    tm = cur_ref.shape[1]
    hb = prev_ref.shape[1]
    cur = cur_ref[0].astype(F32)
    first = jnp.logical_or(i == 0, i == n_lat_blocks)
    last = jnp.logical_or(i == n_lat_blocks - 1, i == n_blocks - 1)
    row = lax.broadcasted_iota(jnp.int32, (tm, 1), 0)
    p_last = prev_ref[0, hb - 1:hb].astype(F32) * jnp.where(first, 0.0, 1.0)
    n_first = next_ref[0, 0:1].astype(F32) * jnp.where(last, 0.0, 1.0)
    before = jnp.where(row == 0, p_last, pltpu.roll(cur, 1, 0))
    after = jnp.where(row == tm - 1, n_first, pltpu.roll(cur, tm - 1, 0))
    u = cw_ref[0:1] * before + cw_ref[1:2] * cur + cw_ref[2:3] * after + cb_ref[...]
    x0_ref[0] = u[:, 0:w].astype(x0_ref.dtype)
    z_ref[0] = (u[:, w:2 * w] * u[:, 2 * w:3 * w]).astype(z_ref.dtype)


def _hy_pre(phy, conv_w, conv_b, n_lat):
    b, t, w3 = phy.shape
    w = w3 // 3
    tm = _pick(math.gcd(n_lat, t - n_lat), (256, 128, 64))
    nb = t // tm
    hb = 16
    per = tm // hb
    o = pl.BlockSpec((1, tm, w), lambda bi, i: (bi, i, 0))
    s = jax.ShapeDtypeStruct((b, t, w), BF16)
    return pl.pallas_call(
        functools.partial(_hy_pre_kernel, n_lat_blocks=n_lat // tm, n_blocks=nb, w=w), name="hy_pre",
        grid=(b, nb),
        in_specs=[pl.BlockSpec((1, hb, w3), lambda bi, i: (bi, jnp.maximum(i * per - 1, 0), 0)),
                  pl.BlockSpec((1, tm, w3), lambda bi, i: (bi, i, 0)),
                  pl.BlockSpec((1, hb, w3), lambda bi, i: (bi, jnp.minimum((i + 1) * per, t // hb - 1), 0)),
                  pl.BlockSpec((3, w3), lambda bi, i: (0, 0)),
                  pl.BlockSpec((1, w3), lambda bi, i: (0, 0))],
        out_specs=[o, o],
        out_shape=[s, s],
        compiler_params=_params(("parallel", "arbitrary"), 40 << 20),
    )(phy, phy, phy, conv_w, conv_b.reshape(1, w3))


def _hdot(a, b):
    return jnp.dot(a, b, precision=lax.Precision.HIGHEST, preferred_element_type=F32)


def _hy_taps_kernel(ft_ref, w1_ref, b1_ref, w2_ref, b2_ref, w3f_ref, w3b_ref, fr_ref, dl_ref,
                    tap_ref, sum_ref):
    i = pl.program_id(1)
    ft = ft_ref[...]
    fr = fr_ref[...]
    h = jnp.sin(fr * (_hdot(ft, w1_ref[...]) + b1_ref[...]))
    h = jnp.sin(fr * (_hdot(h, w2_ref[...]) + b2_ref[...]))
    cf = ft[:, HY_EMB:HY_EMB + 1]
    cb = ft[:, HY_EMB + 1:HY_EMB + 2]
    tt = ft[:, 0:1]
    tap = (cf * _bdot(h, w3f_ref[...]) + cb * _bdot(h, w3b_ref[...])) * jnp.exp(-tt * dl_ref[...])
    tap_ref[...] = tap

    @pl.when(i == 0)
    def _():
        sum_ref[...] = jnp.zeros_like(sum_ref)

    sum_ref[...] += jnp.sum(jnp.abs(tap), axis=0, keepdims=True)


def _hy_taps(feats, w1p, b1, w2, b2, w3, freq, deltas):
    rws = feats.shape[0]
    hid = w2.shape[0]
    w = w3.shape[1] // 2
    tr = _pick(rws, (1024, 512, 256, 128))
    ct = w
    nj = w // ct
    c2 = lambda a: pl.BlockSpec(a.shape, lambda j, i: (0, 0))
    return pl.pallas_call(
        _hy_taps_kernel, name="hy_taps",
        grid=(nj, rws // tr),
        in_specs=[pl.BlockSpec((tr, LANE), lambda j, i: (i, 0)),
                  c2(w1p), pl.BlockSpec((1, hid), lambda j, i: (0, 0)),
                  c2(w2), pl.BlockSpec((1, hid), lambda j, i: (0, 0)),
                  pl.BlockSpec((hid, ct), lambda j, i: (0, j)),
                  pl.BlockSpec((hid, ct), lambda j, i: (0, nj + j)),
                  pl.BlockSpec((1, hid), lambda j, i: (0, 0)),
                  pl.BlockSpec((1, ct), lambda j, i: (0, j))],
        out_specs=[pl.BlockSpec((tr, ct), lambda j, i: (i, j)),
                   pl.BlockSpec((1, ct), lambda j, i: (0, j))],
        out_shape=[jax.ShapeDtypeStruct((rws, w), F32), jax.ShapeDtypeStruct((1, w), F32)],
        compiler_params=_params(("parallel", "arbitrary"), 32 << 20),
    )(feats, w1p, b1.reshape(1, hid), w2, b2.reshape(1, hid), w3, w3, freq.reshape(1, hid),
      deltas.reshape(1, w))


def _dft1_kernel(g_ref, x_ref, o_ref):
    o_ref[...] = lax.dot_general(g_ref[...], x_ref[...].astype(BF16), (((2,), (1,)), ((0,), (0,))),
                                 preferred_element_type=F32).astype(o_ref.dtype)


def _dft1(g, x):
    nb, m2, k = g.shape
    c = x.shape[2]
    bt = _pick(nb, (8, 4, 2))
    ct = _pick(c, (256, 128))
    return pl.pallas_call(
        _dft1_kernel, name="dft1",
        grid=(c // ct, nb // bt),
        in_specs=[pl.BlockSpec((bt, m2, k), lambda j, i: (i, 0, 0)),
                  pl.BlockSpec((bt, k, ct), lambda j, i: (i, 0, j))],
        out_specs=pl.BlockSpec((bt, m2, ct), lambda j, i: (i, 0, j)),
        out_shape=jax.ShapeDtypeStruct((nb, m2, c), BF16),
        compiler_params=_params(("parallel", "arbitrary"), 32 << 20),
    )(g, x)


def _spec_kernel(a_ref, f2_ref, sc_ref, o_ref):
    for l in range(a_ref.shape[0]):
        o_ref[l] = jnp.dot(f2_ref[...], a_ref[l].astype(BF16), preferred_element_type=F32) * sc_ref[...]


def _spec(a, f2, scale):
    na, m2, c = a.shape
    kt = _pick(na, (8, 4, 2))
    ct = _pick(c, (256, 128))
    return pl.pallas_call(
        _spec_kernel, name="hy_spec",
        grid=(c // ct, na // kt),
        in_specs=[pl.BlockSpec((kt, m2, ct), lambda j, i: (i, 0, j)),
                  pl.BlockSpec((m2, m2), lambda j, i: (0, 0)),
                  pl.BlockSpec((1, ct), lambda j, i: (0, j))],
        out_specs=pl.BlockSpec((kt, m2, ct), lambda j, i: (i, 0, j)),
        out_shape=jax.ShapeDtypeStruct((na, m2, c), F32),
        compiler_params=_params(("parallel", "arbitrary"), 32 << 20),
    )(a, f2, scale)


def _conv_mid_kernel(a_ref, f2_ref, h_ref, g3_ref, o_ref):
    nbh = a_ref.shape[1] // 2
    for l in range(a_ref.shape[0]):
        x = jnp.dot(f2_ref[...], a_ref[l].astype(BF16), preferred_element_type=F32)
        xr, xi = x[:nbh], x[nbh:]
        hr, hi = h_ref[l, :nbh], h_ref[l, nbh:]
        y = jnp.concatenate([xr * hr - xi * hi, xr * hi + xi * hr], axis=0)
        o_ref[l] = jnp.dot(g3_ref[l], y.astype(BF16), preferred_element_type=F32).astype(o_ref.dtype)


def _conv_mid(a, f2, h, g3):
    na, m2, c = a.shape
    kt = _pick(na, (8, 4, 2))
    ct = _pick(c, (256, 128))
    slab = pl.BlockSpec((kt, m2, ct), lambda j, i: (i, 0, j))
    return pl.pallas_call(
        _conv_mid_kernel, name="conv_mid",
        grid=(c // ct, na // kt),
        in_specs=[slab, pl.BlockSpec((m2, m2), lambda j, i: (0, 0)), slab,
                  pl.BlockSpec((kt, m2, m2), lambda j, i: (i, 0, 0))],
        out_specs=slab,
        out_shape=jax.ShapeDtypeStruct((na, m2, c), BF16),
        compiler_params=_params(("parallel", "arbitrary"), 32 << 20),
    )(a, f2, h, g3)


def _conv_out_kernel(b_ref, f4_ref, o_ref, *, scale):
    for l in range(b_ref.shape[0]):
        o_ref[l] = jnp.dot(f4_ref[...], b_ref[l].astype(BF16), preferred_element_type=F32) * scale


def _conv_out(bm, f4, scale):
    nb, m2, c = bm.shape
    mo = f4.shape[0]
    pt = _pick(nb, (8, 4, 2))
    ct = _pick(c, (256, 128))
    return pl.pallas_call(
        functools.partial(_conv_out_kernel, scale=scale), name="conv_out",
        grid=(c // ct, nb // pt),
        in_specs=[pl.BlockSpec((pt, m2, ct), lambda j, i: (i, 0, j)),
                  pl.BlockSpec((mo, m2), lambda j, i: (0, 0))],
        out_specs=pl.BlockSpec((pt, mo, ct), lambda j, i: (i, 0, j)),
        out_shape=jax.ShapeDtypeStruct((nb, mo, c), F32),
        compiler_params=_params(("parallel", "arbitrary"), 32 << 20),
    )(bm, f4)


def _ctx_conv_kernel(z_ref, ext_ref, sc_ref, o_ref):
    n = z_ref.shape[1]

    def body(s, acc):
        return acc + ext_ref[pl.ds(n - s, n), :] * z_ref[0, pl.ds(s, 1), :]

    acc = lax.fori_loop(0, n, body, jnp.zeros(o_ref.shape[1:], F32))
    o_ref[0] = acc * sc_ref[...]


def _ctx_conv(z, ext, scale):
    b, n, w = z.shape
    ct = LANE
    return pl.pallas_call(
        _ctx_conv_kernel, name="ctx_conv",
        grid=(b, w // ct),
        in_specs=[pl.BlockSpec((1, n, ct), lambda bi, j: (bi, 0, j)),
                  pl.BlockSpec((2 * n, ct), lambda bi, j: (0, j)),
                  pl.BlockSpec((1, ct), lambda bi, j: (0, j))],
        out_specs=pl.BlockSpec((1, n, ct), lambda bi, j: (bi, 0, j)),
        out_shape=jax.ShapeDtypeStruct((b, n, w), F32),
        compiler_params=_params(("parallel", "arbitrary"), 16 << 20),
    )(z, ext, scale)


def _hy_post_kernel(y_ref, z_ref, x0_ref, bias_ref, o_ref):
    o_ref[0] = (x0_ref[0].astype(F32) * (y_ref[0] + bias_ref[...] * z_ref[0].astype(F32))).astype(o_ref.dtype)


def _hy_post(y, z, x0, bias):
    b, t, w = y.shape
    tm = _pick(t, _TM_ROWS)
    s = pl.BlockSpec((1, tm, w), lambda bi, i: (bi, i, 0))
    return pl.pallas_call(
        _hy_post_kernel, name="hy_post",
        grid=(b, t // tm),
        in_specs=[s, s, s, pl.BlockSpec((1, w), lambda bi, i: (0, 0))],
        out_specs=s,
        out_shape=jax.ShapeDtypeStruct((b, t, w), BF16),
        compiler_params=_params(("parallel", "arbitrary"), 32 << 20),
    )(y, z, x0, bias.reshape(1, w))


def _dft_factors(n):
    m = 2 * n
    na = 1 << ((m.bit_length() - 1) // 2)
    return na, m // na


def _cis(num, den, sign):
    ang = (num % den).astype(F32) * (2.0 * math.pi / den)
    return jnp.cos(ang), sign * jnp.sin(ang)


def _cblock(cr, ci):
    return jnp.concatenate([jnp.concatenate([cr, -ci], axis=-1),
                            jnp.concatenate([ci, cr], axis=-1)], axis=-2)


def _dft_tables(n):
    na, nb = _dft_factors(n)
    m = na * nb
    ah = na // 2
    ar = lambda k: jnp.arange(k, dtype=jnp.int32)
    ka, bb = ar(na)[None, :, None], ar(nb)[:, None, None]
    g1c = _cblock(*_cis(ka * (nb * ar(ah)[None, None, :] + bb), m, -1.0))
    g1t = jnp.concatenate(_cis(ka * (nb * ar(na)[None, None, :] + bb), m, -1.0), axis=-2)
    f2 = _cblock(*_cis(ar(nb)[:, None] * ar(nb)[None, :], nb, -1.0))
    g3 = _cblock(*_cis(ar(nb)[None, :, None] * (ar(na)[:, None, None] + na * ar(nb)[None, None, :]), m, 1.0))
    f4 = _cblock(*_cis(ar(ah)[:, None] * ar(na)[None, :], na, 1.0))
    return tuple(t.astype(BF16) for t in (g1c, g1t, f2, g3, f4))


def _filter_feats(n, order):
    bands_n = (HY_EMB - 1) // 2
    t = jnp.linspace(0.0, 1.0, n, dtype=F32)[:, None]
    lag = jnp.arange(n, dtype=F32)[:, None]
    bands = jnp.linspace(1e-4, bands_n - 1, bands_n, dtype=F32)[None, :]
    ang = 2.0 * math.pi * lag * bands / n
    zf = jnp.concatenate([t, jnp.cos(ang), -jnp.sin(ang)], axis=-1)
    if order == "dft":
        na, nb = _dft_factors(n)
        m = (np.arange(na)[None, :] * nb + np.arange(nb)[:, None]).reshape(-1)
    else:
        m = (np.arange(2 * n) - n) % (2 * n)
    lag_of = np.where(m < n, m, np.where(m == n, 0, 2 * n - m))
    cf = np.where(m == 0, 0.5, np.where(m < n, 1.0, 0.0)).astype(np.float32)
    cb = np.where(m == 0, 0.5, np.where(m > n, 1.0, 0.0)).astype(np.float32)
    pad = jnp.zeros((2 * n, LANE - HY_EMB - 2), F32)
    return jnp.concatenate([zf[lag_of], jnp.asarray(cf)[:, None], jnp.asarray(cb)[:, None], pad], axis=-1)


def _hy_deltas(w):
    return jnp.abs(jnp.linspace(math.log(HY_DECAY_TARGET) / HY_SLOW_PCT,
                                math.log(HY_DECAY_TARGET) / HY_FAST_PCT, w, dtype=F32))


def _long_conv_latent(z, taps_perm, inv_l1):
    b, n, w = z.shape
    na, nb = _dft_factors(n)
    ah = na // 2
    g1c, g1t, f2, g3, f4 = _dft_tables(n)
    h = _spec(_dft1(g1t, taps_perm.reshape(nb, na, w)).reshape(nb, 2, na, w).transpose(2, 1, 0, 3)
              .reshape(na, 2 * nb, w), f2, inv_l1)
    outs = []
    for pair in range(b // 2):
        zz = z[2 * pair:2 * pair + 2].astype(BF16).reshape(2, ah, nb, w).transpose(2, 0, 1, 3).reshape(nb, na, w)
        a = _dft1(g1c, zz).reshape(nb, 2, na, w).transpose(2, 1, 0, 3).reshape(na, 2 * nb, w)
        bm = _conv_mid(a, f2, h, g3).reshape(na, 2, nb, w).transpose(2, 1, 0, 3).reshape(nb, 2 * na, w)
        y = _conv_out(bm, f4, 1.0 / (na * nb))
        outs.append(y.reshape(nb, 2, ah, w).transpose(1, 2, 0, 3).reshape(2, n, w))
    return jnp.concatenate(outs, axis=0) if len(outs) > 1 else outs[0]


def _pack_in_proj(w_in, rw_mu, dims):
    depth, d_model, _ = w_in.shape
    w, dl, al, gl = dims
    ld, la, lg = _rup(dl, LANE), _rup(al, LANE), _rup(gl, LANE)
    rw_cols = 3 * w + 2 * dl + 2 * al + gl
    tile = 1024 if d_model >= 2048 else LANE
    na = _rup(3 * w + 2 * ld + 2 * la + lg, tile)
    pieces = [(0, 3 * w, 0), (3 * w, dl, 3 * w), (3 * w + dl, dl, 3 * w + ld),
              (3 * w + 2 * dl, al, 3 * w + 2 * ld), (3 * w + 2 * dl + al, al, 3 * w + 2 * ld + la),
              (3 * w + 2 * dl + 2 * al, gl, 3 * w + 2 * ld + 2 * la)]
    cols, mus = [], []
    orig = np.full((na,), rw_cols - 1, np.int64)
    pos = 0
    for src, width, dst in pieces + [(rw_cols, 0, na)]:
        if dst > pos:
            cols.append(jnp.zeros((depth, d_model, dst - pos), BF16))
            mus.append(jnp.zeros((depth, dst - pos), F32))
        cols.append(w_in[:, :, src:src + width].astype(BF16))
        mus.append(rw_mu[:, src:src + width])
        orig[dst:dst + width] = np.arange(src, src + width)
        pos = dst + width
    wa = _tile_w(jnp.concatenate(cols, axis=2), tile)
    shift = _rup(rw_cols, tile) - rw_cols
    total = shift + w_in.shape[2]
    wrest = _tile_w(jnp.pad(w_in, ((0, 0), (0, 0), (shift, _rup(total, tile) - total))), tile)
    starts = (rw_cols, rw_cols + 5 * w, rw_cols + 8 * w)
    offs = [(shift + s) // tile for s in starts]
    mu = jnp.concatenate(mus, axis=1).reshape(depth, 1, na)
    grp = np.stack([orig // (rw_cols // 4), orig // (rw_cols // 2)]).astype(np.int32)
    grp = np.minimum(grp, np.array([[3], [1]])).astype(np.int32)
    return wa, wrest, tile, offs, mu, jnp.asarray(grp), (w, ld, la, lg), na


def _pad_rows(a, rows):
    return jnp.pad(a, [(0, 0)] * (a.ndim - 2) + [(0, rows - a.shape[-2]), (0, 0)])


def kernel(x, c, ctx, c_ctx, ada_w, ada_b, norm1_g, norm2_g, w_in, rw_mu, rw_w0, rw_w_up, rw_a0, rw_a_up, rw_g_up, rw_k_k, rw_k_a, rw_r_k, rw_ln_g, rw_ln_b, hg_lower_bounds, hg_norm_g, hy_conv_w, hy_conv_b, hy_f_w1, hy_f_b1, hy_f_w2, hy_f_b2, hy_f_w3, hy_freq, hy_bias, w_branch_a, w_branch_b, w_branch_c, w_out, ffn_w_gate, ffn_w_up, ffn_w_down, final_norm_g):
    bsz, n_lat, d = x.shape
    n_ctx = ctx.shape[1]
    depth = w_in.shape[0]
    w = rw_k_k.shape[1]
    dims = (w, rw_w_up.shape[2], rw_a_up.shape[2], rw_g_up.shape[1])
    assert bsz % 2 == 0 and n_lat % GRID_W == 0 and n_ctx % CHUNK == 0 and w % LANE == 0

    xs = jnp.concatenate([x, ctx], axis=1)
    cvec = jnp.zeros((_rup(bsz + 1, 8), d), F32).at[:bsz].set(c).at[bsz].set(c_ctx)
    lb_cum = jnp.cumsum(jax.nn.softmax(hg_lower_bounds.astype(F32), axis=0), axis=0)
    deltas = _hy_deltas(w)

    wt_rw, wt_in, tile, offs, mu_all, grp, pdims, na = _pack_in_proj(w_in, rw_mu, dims)
    assert (5 * w) % tile == 0 and (3 * w) % tile == 0 and (3 * d) % tile == 0
    _, ld, la, lg = pdims
    hid = ffn_w_gate.shape[2]
    hp = _rup(hid, 1024) if hid > 1024 else _rup(hid, LANE)
    wt_g = _tile_w(_cast_pad(ffn_w_gate, d, hp), _tn_ffn(hp))
    wt_u = _tile_w(_cast_pad(ffn_w_up, d, hp), _tn_ffn(hp))
    wt_d = _tile_w(_cast_pad(ffn_w_down, hp, d), _tn_wide(d), _tk_of(hp))
    wt_o = _tile_w(w_out, _tn_wide(d), _tk_of(d))
    wt_a, wt_b, wt_c = (_tile_w(t, _tn_wide(d)) for t in (w_branch_a, w_branch_b, w_branch_c))

    for layer in range(depth):
        need_ctx = layer < depth - 1
        mod = _ada(cvec, ada_w, ada_b, layer)[:bsz + 1].reshape(bsz + 1, 1, 6, d)
        mods = [mod[:, :, s, :] for s in range(6)]

        h1 = _modnorm(xs, norm1_g[layer], mods[0], mods[1], n_lat)
        pa = _matmul(h1, wt_rw, layer, 0, na // tile, BF16)
        phg = _matmul(h1, wt_in, layer, offs[0], 5 * w // tile, F32)
        rows = None if need_ctx else n_lat
        phy = _matmul(h1, wt_in, layer, offs[1], 3 * w // tile, BF16, rows)
        pgate = _matmul(h1, wt_in, layer, offs[2], 3 * d // tile, BF16, rows)

        r, v, kn, g, bonus, lw, kd, bb = _rw_prep(
            pa, grp, mu_all[layer], rw_w0[layer].reshape(2, 1, w), _pad_rows(rw_w_up[layer], ld),
            rw_a0[layer].reshape(2, 1, w), _pad_rows(rw_a_up[layer], la), _pad_rows(rw_g_up[layer], lg),
            rw_k_k[layer].reshape(1, w), rw_k_a[layer].reshape(1, w), rw_r_k[layer].reshape(1, w),
            n_lat, pdims)
        lb = lb_cum[layer] - lb_cum[0]
        lbt = jnp.stack([jnp.log(lb), jnp.log1p(-lb), 1.0 - lb], axis=1)
        y_f, y_b, o_f, o_b = _scans(r, v, kn, lw, kd, bb, phg, lbt, n_lat)
        oa = _rw_finish(y_f, y_b, bonus, g, rw_ln_g[layer], rw_ln_b[layer])
        ob = _hg_finish(o_f, o_b, phg, hg_norm_g[layer], w)

        x0, z = _hy_pre(phy, hy_conv_w[layer], hy_conv_b[layer], n_lat)
        w1p = _pad_rows(hy_f_w1[layer], LANE)
        filt = (w1p, hy_f_b1[layer], hy_f_w2[layer], hy_f_b2[layer], hy_f_w3[layer], hy_freq[layer], deltas)
        taps, l1 = _hy_taps(_filter_feats(n_lat, "dft"), *filt)
        y_lat = _long_conv_latent(z[:, :n_lat], taps, 1.0 / l1)
        if need_ctx:
            ext, l1c = _hy_taps(_filter_feats(n_ctx, "lag"), *filt)
            y_ctx = _ctx_conv(z[:, n_lat:].astype(F32), ext, 1.0 / l1c)
            y_lat = jnp.concatenate([y_lat, y_ctx], axis=1)
        oc = _hy_post(y_lat, z, x0, hy_bias[layer])

        ym = _merge(oa, ob, oc, wt_a, wt_b, wt_c, layer, pgate)
        xs = _matmul_resid(ym, wt_o, layer, xs, mods[2], n_lat)

        h2 = _modnorm(xs, norm2_g[layer], mods[3], mods[4], n_lat)
        act = _swiglu_up(h2, wt_g, wt_u, layer)
        xs = _matmul_resid(act, wt_d, layer, xs, mods[5], n_lat)

    return _final_norm(xs, final_norm_g, n_lat)
```
